```python
import jax, jax.numpy as jnp
from jax import lax
import numpy as np

D_MODEL = 2048
BATCH = 4
SEQ = 2048
DEPTH = 1
DEC_BATCH = 128
DEC_SEQ = 8
PAST_LEN = 16384
PAGE_SIZE = 128

N_META = 16
LRU_WIDTH = D_MODEL // 2
LRU_BLOCKS = 16
LRU_BLOCK = LRU_WIDTH // LRU_BLOCKS
CONV_W = 4
LRU_C = 8.0
RET_HEADS = 8
RET_WIDTH = D_MODEL // 2
RET_DK = RET_WIDTH // RET_HEADS
RET_DV = RET_WIDTH // RET_HEADS
CHUNK = 128
D_FF = 4 * D_MODEL
ROPE_BASE = 10000.0
EPS = 1e-6
PROJ_SIZES = (LRU_WIDTH, LRU_WIDTH, RET_WIDTH, RET_WIDTH, RET_WIDTH, RET_WIDTH, D_MODEL, D_MODEL)
D_PROJ = 2 * LRU_WIDTH + 4 * RET_WIDTH + 2 * D_MODEL

kernel_name = 'hawk_retnet_meta_hybrid_step'

F32 = jnp.float32


def rmsnorm(x, g):
    xf = x.astype(F32)
    y = xf * lax.rsqrt(jnp.mean(xf * xf, axis=-1, keepdims=True) + EPS)
    return (y * g.astype(F32)).astype(x.dtype)


def split_proj(proj):
    idx, acc = [], 0
    for s in PROJ_SIZES[:-1]:
        acc += s
        idx.append(acc)
    return jnp.split(proj, idx, axis=-1)


def rope(x, pos):
    d = x.shape[-1]
    inv = ROPE_BASE ** (-jnp.arange(0, d, 2, dtype=F32) / d)
    ang = pos.astype(F32)[:, None] * inv[None, :]
    cos = jnp.cos(ang)[None, :, None, :]
    sin = jnp.sin(ang)[None, :, None, :]
    xf = x.astype(F32)
    x1, x2 = xf[..., : d // 2], xf[..., d // 2:]
    return jnp.concatenate([x1 * cos - x2 * sin, x2 * cos + x1 * sin], axis=-1)


def causal_conv(u, buf, w, b):
    T = u.shape[1]
    full = jnp.concatenate([buf.astype(u.dtype), u], axis=1)
    out = full[:, 0:T] * w[0]
    for j in range(1, CONV_W):
        out = out + full[:, j:j + T] * w[j]
    return out + b, full[:, T:]


def rglru(u, h0, pos, wa, ba, wx, bx, lam):
    B, T, _ = u.shape
    uf = u.astype(F32)
    ub = uf.reshape(B, T, LRU_BLOCKS, LRU_BLOCK)
    rec_gate = jax.nn.sigmoid(jnp.einsum('btnc,ncd->btnd', ub, wa.astype(F32)).reshape(B, T, LRU_WIDTH) + ba)
    in_gate = jax.nn.sigmoid(jnp.einsum('btnc,ncd->btnd', ub, wx.astype(F32)).reshape(B, T, LRU_WIDTH) + bx)
    log_a = LRU_C * rec_gate * jax.nn.log_sigmoid(lam.astype(F32))
    a = jnp.exp(log_a)
    mult = jnp.sqrt(-jnp.expm1(2.0 * log_a))
    mult = jnp.where((pos == 0)[None, :, None], 1.0, mult)
    bterm = mult * in_gate * uf
    bterm = bterm.at[:, 0].add(a[:, 0] * h0.astype(F32))

    def comb(left, right):
        a1, b1 = left
        a2, b2 = right
        return a1 * a2, a2 * b1 + b2

    _, h = lax.associative_scan(comb, (a, bterm), axis=1)
    return h.astype(u.dtype), h[:, -1].astype(h0.dtype)


def ret_chunk(S, q, k, v, log_g):
    L = q.shape[1]
    n = jnp.arange(L, dtype=F32)
    diff = n[:, None] - n[None, :]
    causal = diff >= 0
    decay = jnp.where(causal[None], jnp.exp(log_g[:, None, None] * jnp.where(causal, diff, 0.0)[None]), 0.0)
    scores = jnp.einsum('blhd,bmhd->bhlm', q, k) * decay[None]
    intra = jnp.einsum('bhlm,bmhv->blhv', scores, v)
    inter = jnp.einsum('blhd,bhdv->blhv', q, S) * jnp.exp((n + 1.0)[:, None] * log_g[None, :])[None, :, :, None]
    kd = k * jnp.exp((L - 1.0 - n)[:, None] * log_g[None, :])[None, :, :, None]
    S_new = jnp.exp(L * log_g)[None, :, None, None] * S + jnp.einsum('blhd,blhv->bhdv', kd, v)
    return S_new, intra + inter


def retention(q, k, v, S0, lead):
    B, T = q.shape[0], q.shape[1]
    log_g = jnp.log1p(-jnp.exp2(-5.0 - jnp.arange(RET_HEADS, dtype=F32)))
    q, k, v = q.astype(F32), k.astype(F32), v.astype(F32)
    S = S0.astype(F32)
    outs = []
    if lead > 0:
        S, o_lead = ret_chunk(S, q[:, :lead], k[:, :lead], v[:, :lead], log_g)
        outs.append(o_lead)
    rest = T - lead
    c = CHUNK if rest % CHUNK == 0 else rest
    nc = rest // c

    def to_chunks(t):
        return jnp.moveaxis(t[:, lead:].reshape(B, nc, c, t.shape[2], t.shape[3]), 1, 0)

    def step(S_c, xs):
        qc, kc, vc = xs
        return ret_chunk(S_c, qc, kc, vc, log_g)

    S, o = lax.scan(step, S, (to_chunks(q), to_chunks(k), to_chunks(v)))
    outs.append(jnp.moveaxis(o, 0, 1).reshape(B, rest, RET_HEADS, RET_DV))
    return jnp.concatenate(outs, axis=1), S.astype(S0.dtype)


def head_norm(o, g):
    mu = jnp.mean(o, axis=-1, keepdims=True)
    var = jnp.mean(jnp.square(o - mu), axis=-1, keepdims=True)
    return (o - mu) * lax.rsqrt(var + EPS) * g.astype(F32).reshape(RET_HEADS, RET_DV)


def mixer(h, pos, lead, conv_buf, lru_h0, ret_S0, w_in, conv_w, conv_b, lru_wa, lru_ba, lru_wx, lru_bx,
          lru_lam, ret_norm_g, p_a, p_b, w_out):
    B, T, _ = h.shape
    xa, ga, q, k, v, g, gate_a, gate_b = split_proj(h @ w_in)
    xc, new_buf = causal_conv(xa, conv_buf, conv_w, conv_b)
    y_lru, h_last = rglru(xc, lru_h0, pos, lru_wa, lru_ba, lru_wx, lru_bx, lru_lam)
    ya = y_lru * jax.nn.gelu(ga)
    qh = rope(q.reshape(B, T, RET_HEADS, RET_DK), pos)
    kh = rope(k.reshape(B, T, RET_HEADS, RET_DK), pos) * (RET_DK ** -0.5)
    vh = v.reshape(B, T, RET_HEADS, RET_DV)
    o, S_last = retention(qh, kh, vh, ret_S0, lead)
    o = head_norm(o, ret_norm_g).reshape(B, T, RET_WIDTH).astype(h.dtype)
    yb = jax.nn.silu(g) * o
    merged = jax.nn.sigmoid(gate_a) * (ya @ p_a) + jax.nn.sigmoid(gate_b) * (yb @ p_b)
    return merged @ w_out, new_buf, h_last, S_last


def trunk(x, pos, lead, conv_bufs, lru_hs, ret_Ss, norm_mix_g, w_in, conv_w, conv_b, lru_wa, lru_ba,
          lru_wx, lru_bx, lru_lam, ret_norm_g, p_a, p_b, w_out, norm_ffn_g, w_up, w_down, norm_f_g):
    new_conv, new_lru, new_ret = [], [], []
    for l in range(DEPTH):
        h = rmsnorm(x, norm_mix_g[l])
        mix, cb, lh, rs = mixer(h, pos, lead, conv_bufs[l], lru_hs[l], ret_Ss[l], w_in[l], conv_w[l],
                                conv_b[l], lru_wa[l], lru_ba[l], lru_wx[l], lru_bx[l], lru_lam[l],
                                ret_norm_g[l], p_a[l], p_b[l], w_out[l])
        x = x + mix
        h = rmsnorm(x, norm_ffn_g[l])
        x = x + jnp.square(jax.nn.relu(h @ w_up[l])) @ w_down[l]
        new_conv.append(cb)
        new_lru.append(lh)
        new_ret.append(rs)
    return rmsnorm(x, norm_f_g), jnp.stack(new_conv), jnp.stack(new_lru), jnp.stack(new_ret)


def setup_inputs(seed: int = 0) -> dict:
    key = jax.random.key(seed)
    ks = jax.random.split(key, 24)

    def nrm(k, shape, scale):
        return jax.random.normal(k, shape, F32) * scale

    a_c = jax.random.uniform(ks[14], (DEPTH, LRU_WIDTH), F32, 0.9, 0.999)
    s = a_c ** (1.0 / LRU_C)
    return {
        'x_prompt': nrm(ks[0], (BATCH, SEQ, D_MODEL), 1.0),
        'x_sample': nrm(ks[1], (DEC_BATCH, DEC_SEQ, D_MODEL), 1.0),
        'state_conv': nrm(ks[2], (DEPTH, DEC_BATCH, CONV_W - 1, LRU_WIDTH), 1.0),
        'state_lru': nrm(ks[3], (DEPTH, DEC_BATCH, LRU_WIDTH), 1.0),
        'state_ret': nrm(ks[4], (DEPTH, DEC_BATCH, RET_HEADS, RET_DK, RET_DV), 1.0),
        'meta_tokens': nrm(ks[5], (N_META, D_MODEL), 1.0),
        'norm_mix_g': 1.0 + nrm(ks[6], (DEPTH, D_MODEL), 0.02),
        'w_in': nrm(ks[7], (DEPTH, D_MODEL, D_PROJ), D_MODEL ** -0.5),
        'conv_w': nrm(ks[8], (DEPTH, CONV_W, LRU_WIDTH), CONV_W ** -0.5),
        'conv_b': nrm(ks[9], (DEPTH, LRU_WIDTH), 0.02),
        'lru_wa': nrm(ks[10], (DEPTH, LRU_BLOCKS, LRU_BLOCK, LRU_BLOCK), LRU_BLOCK ** -0.5),
        'lru_ba': nrm(ks[11], (DEPTH, LRU_WIDTH), 0.02),
        'lru_wx': nrm(ks[12], (DEPTH, LRU_BLOCKS, LRU_BLOCK, LRU_BLOCK), LRU_BLOCK ** -0.5),
        'lru_bx': nrm(ks[13], (DEPTH, LRU_WIDTH), 0.02),
        'lru_lam': jnp.log(s) - jnp.log1p(-s),
        'ret_norm_g': 1.0 + nrm(ks[15], (DEPTH, RET_WIDTH), 0.02),
        'p_a': nrm(ks[16], (DEPTH, LRU_WIDTH, D_MODEL), LRU_WIDTH ** -0.5),
        'p_b': nrm(ks[17], (DEPTH, RET_WIDTH, D_MODEL), RET_WIDTH ** -0.5),
        'w_out': nrm(ks[18], (DEPTH, D_MODEL, D_MODEL), D_MODEL ** -0.5),
        'norm_ffn_g': 1.0 + nrm(ks[19], (DEPTH, D_MODEL), 0.02),
        'w_up': nrm(ks[20], (DEPTH, D_MODEL, D_FF), D_MODEL ** -0.5),
        'w_down': nrm(ks[21], (DEPTH, D_FF, D_MODEL), D_FF ** -0.5),
        'norm_f_g': 1.0 + nrm(ks[22], (D_MODEL,), 0.02),
    }


def reference(x_prompt, x_sample, state_conv, state_lru, state_ret, meta_tokens, norm_mix_g, w_in, conv_w,
              conv_b, lru_wa, lru_ba, lru_wx, lru_bx, lru_lam, ret_norm_g, p_a, p_b, w_out, norm_ffn_g,
              w_up, w_down, norm_f_g):
    weights = (norm_mix_g, w_in, conv_w, conv_b, lru_wa, lru_ba, lru_wx, lru_bx, lru_lam, ret_norm_g,
               p_a, p_b, w_out, norm_ffn_g, w_up, w_down, norm_f_g)
    B = x_prompt.shape[0]
    meta = jnp.broadcast_to(meta_tokens.astype(x_prompt.dtype)[None], (B, N_META, D_MODEL))
    xp = jnp.concatenate([meta, x_prompt], axis=1)
    pos_p = jnp.arange(N_META + x_prompt.shape[1], dtype=jnp.int32)
    conv0 = jnp.zeros((DEPTH, B, CONV_W - 1, LRU_WIDTH), state_conv.dtype)
    lru0 = jnp.zeros((DEPTH, B, LRU_WIDTH), state_lru.dtype)
    ret0 = jnp.zeros((DEPTH, B, RET_HEADS, RET_DK, RET_DV), state_ret.dtype)
    yp, new_conv_prompt, new_lru_prompt, new_ret_prompt = trunk(xp, pos_p, N_META, conv0, lru0, ret0, *weights)
    y_prompt = yp[:, N_META:]
    pos_s = PAST_LEN + jnp.arange(x_sample.shape[1], dtype=jnp.int32)
    y_sample, new_conv_sample, new_lru_sample, new_ret_sample = trunk(
        x_sample, pos_s, 0, state_conv, state_lru, state_ret, *weights)
    return (y_prompt, y_sample, new_conv_prompt, new_lru_prompt, new_ret_prompt,
            new_conv_sample, new_lru_sample, new_ret_sample)
```

```python
import functools
import math

import jax
import jax.numpy as jnp
from jax import lax
from jax.experimental import pallas as pl
from jax.experimental.pallas import tpu as pltpu

F32 = jnp.float32
BF16 = jnp.bfloat16

N_META = 16
LRU_BLOCKS = 16
CONV_W = 4
LRU_C = 8.0
RET_HEADS = 8
HEAD_DIM = 128
CHUNK = 128
ROPE_BASE = 10000.0
EPS = 1e-6

LANES = 128
VMEM_LIMIT = 56 << 20


def _cparams(n_axes):
    return pltpu.CompilerParams(dimension_semantics=("arbitrary",) * n_axes, vmem_limit_bytes=VMEM_LIMIT)


def _rmsnorm(x, g):
    return x * lax.rsqrt(jnp.mean(x * x, axis=-1, keepdims=True) + EPS) * g


def _sigmoid(x):
    return 1.0 / (1.0 + jnp.exp(-x))


_COL_GATE, _COL_Q, _COL_K, _COL_V, _COL_G, _COL_GATE_A = 1, 2, 3, 4, 5, 6
_ACT_GELU, _ACT_Q, _ACT_K, _ACT_V, _ACT_SGA, _ACT_SGB, _ACT_SILU = 0, 1, 2, 3, 4, 6, 8


def _act_block(j):
    return jnp.where(j <= _COL_V, jnp.maximum(j - 1, 0), jnp.where(j == _COL_G, _ACT_SILU, j - 2))


def _inproj_kernel(x_ref, g_ref, w_ref, cos_ref, sin_ref, xa_ref, act_ref, xn_ref):
    j = pl.program_id(1)

    @pl.when(j == 0)
    def _():
        xn_ref[...] = _rmsnorm(x_ref[...], g_ref[...]).astype(BF16)

    acc = jnp.dot(xn_ref[...], w_ref[...], preferred_element_type=F32)

    def rope(scale):
        cos, sin = cos_ref[...], sin_ref[...]
        for h in range(RET_HEADS):
            a = acc[:, h * HEAD_DIM:(h + 1) * HEAD_DIM]
            r = a * cos + pltpu.roll(a, HEAD_DIM // 2, axis=1) * sin
            if scale is not None:
                r = r * scale
            act_ref[:, h * HEAD_DIM:(h + 1) * HEAD_DIM] = r.astype(BF16)

    @pl.when(j == 0)
    def _():
        xa_ref[...] = acc

    @pl.when(j == _COL_GATE)
    def _():
        act_ref[...] = jax.nn.gelu(acc).astype(BF16)

    @pl.when(j == _COL_Q)
    def _():
        rope(None)

    @pl.when(j == _COL_K)
    def _():
        rope(HEAD_DIM ** -0.5)

    @pl.when(j == _COL_V)
    def _():
        act_ref[...] = acc.astype(BF16)

    @pl.when(j == _COL_G)
    def _():
        act_ref[...] = (acc * _sigmoid(acc)).astype(BF16)

    @pl.when(j >= _COL_GATE_A)
    def _():
        act_ref[...] = _sigmoid(acc).astype(BF16)


def _inproj(x, g, w_bf16, cos, sin, bm, table_blocks):
    m, d = x.shape
    n = w_bf16.shape[1]
    bn = 1024
    nj = n // bn
    return pl.pallas_call(
        _inproj_kernel,
        name="inproj",
        grid=(m // bm, nj),
        in_specs=[
            pl.BlockSpec((bm, d), lambda i, j: (i, 0)),
            pl.BlockSpec((1, d), lambda i, j: (0, 0)),
            pl.BlockSpec((d, bn), lambda i, j: (0, j)),
            pl.BlockSpec((bm, LANES), lambda i, j: (i % table_blocks, 0)),
            pl.BlockSpec((bm, LANES), lambda i, j: (i % table_blocks, 0)),
        ],
        out_specs=[
            pl.BlockSpec((bm, bn), lambda i, j: (i, 0)),
            pl.BlockSpec((bm, bn), lambda i, j: (i, _act_block(j))),
        ],
        out_shape=[jax.ShapeDtypeStruct((m, bn), F32), jax.ShapeDtypeStruct((m, n - bn), BF16)],
        scratch_shapes=[pltpu.VMEM((bm, d), BF16)],
        compiler_params=_cparams(2),
    )(x, g, w_bf16, cos, sin)


def _lru_coeffs(xc, wbd_ref, bg_ref, cl_row, group):
    gates = jnp.dot(xc.astype(BF16), wbd_ref[group], preferred_element_type=F32) + bg_ref[group]
    rec = _sigmoid(gates[:, :LANES])
    ing = _sigmoid(gates[:, LANES:])
    log_a = rec * cl_row
    a = jnp.exp(log_a)
    mult = jnp.sqrt(jnp.tanh(-log_a) * (a * a + 1.0))
    return a, mult, ing


def _lru_seq_kernel(xa_ref, ga_ref, conv0_ref, h0_ref, cw_ref, cb_ref, wbd_ref, bg_ref, lam_ref,
                    *refs, tr, emit_y, first_pos_zero):
    if emit_y:
        ya_ref, convout_ref, hout_ref, halo_ref, h_ref, a_ref, b_ref = refs
    else:
        convout_ref, hout_ref, halo_ref, h_ref, a_ref, b_ref = refs
    t = pl.program_id(1)
    halo = 8

    @pl.when(t == 0)
    def _():
        halo_ref[halo - 3:halo, :] = conv0_ref[0]
        h_ref[...] = h0_ref[0]

    halo_ref[halo:halo + tr, :] = xa_ref[...]
    xc = cb_ref[...] + cw_ref[0:1, :] * halo_ref[halo - 3:halo - 3 + tr, :]
    for j in range(1, CONV_W):
        xc = xc + cw_ref[j:j + 1, :] * halo_ref[halo - 3 + j:halo - 3 + j + tr, :]
    halo_ref[halo - 3:halo, :] = halo_ref[halo + tr - 3:halo + tr, :]

    cl = LRU_C * jax.nn.log_sigmoid(lam_ref[...])
    for g in range(xc.shape[1] // LANES):
        sl = slice(g * LANES, (g + 1) * LANES)
        a, mult, ing = _lru_coeffs(xc[:, sl], wbd_ref, bg_ref, cl[:, sl], g)
        if first_pos_zero:
            row = lax.broadcasted_iota(jnp.int32, a.shape, 0) + t * tr
            mult = jnp.where(row == 0, 1.0, mult)
        a_ref[:, sl] = a
        b_ref[:, sl] = mult * ing * xc[:, sl]

    def step(r, h):
        h = a_ref[pl.ds(r, 1), :] * h + b_ref[pl.ds(r, 1), :]
        b_ref[pl.ds(r, 1), :] = h
        return h

    h_ref[...] = lax.fori_loop(0, tr, step, h_ref[...])
    if emit_y:
        ya_ref[...] = (b_ref[...] * ga_ref[...].astype(F32)).astype(BF16)

    @pl.when(t == pl.num_programs(1) - 1)
    def _():
        convout_ref[0] = halo_ref[halo - 3:halo, :]
        hout_ref[0] = h_ref[...]


def _lru_seq(xa, act, row0, nseq, seqlen, tr, conv0, h0, lw, emit_y, first_pos_zero):
    w = xa.shape[1]
    nt = seqlen // tr
    rb0 = row0 // tr
    ng = w // LANES
    state_map = (lambda b, t: (b, 0, 0)) if conv0.shape[0] == nseq else (lambda b, t: (0, 0, 0))
    const2 = lambda b, t: (0, 0)
    const3 = lambda b, t: (0, 0, 0)
    out_specs = [pl.BlockSpec((1, CONV_W - 1, w), lambda b, t: (b, 0, 0)),
                 pl.BlockSpec((1, 1, w), lambda b, t: (b, 0, 0))]
    out_shape = [jax.ShapeDtypeStruct((nseq, CONV_W - 1, w), F32), jax.ShapeDtypeStruct((nseq, 1, w), F32)]
    if emit_y:
        out_specs = [pl.BlockSpec((tr, w), lambda b, t: (b * nt + t, 0))] + out_specs
        out_shape = [jax.ShapeDtypeStruct((nseq * seqlen, w), BF16)] + out_shape
    return pl.pallas_call(
        functools.partial(_lru_seq_kernel, tr=tr, emit_y=emit_y, first_pos_zero=first_pos_zero),
        name="lru_seq",
        grid=(nseq, nt),
        in_specs=[
            pl.BlockSpec((tr, w), lambda b, t: (rb0 + b * nt + t, 0)),
            pl.BlockSpec((tr, w), lambda b, t: (rb0 + b * nt + t, _ACT_GELU)),
            pl.BlockSpec((1, CONV_W - 1, w), state_map),
            pl.BlockSpec((1, 1, w), state_map),
            pl.BlockSpec((CONV_W, w), const2),
            pl.BlockSpec((1, w), const2),
            pl.BlockSpec((ng, LANES, 2 * LANES), const3),
            pl.BlockSpec((ng, 1, 2 * LANES), const3),
            pl.BlockSpec((1, w), const2),
        ],
        out_specs=out_specs,
        out_shape=out_shape,
        scratch_shapes=[pltpu.VMEM((8 + tr, w), F32), pltpu.VMEM((1, w), F32),
                        pltpu.VMEM((tr, w), F32), pltpu.VMEM((tr, w), F32)],
        compiler_params=_cparams(2),
    )(xa, act, conv0, h0, lw["conv_w"], lw["conv_b"], lw["wbd"], lw["bg"], lw["lam"])


def _lru_step_kernel(xa_ref, ga_ref, conv0_ref, h0_ref, cw_ref, cb_ref, wbd_ref, bg_ref, lam_ref,
                     ya_ref, convout_ref, hout_ref, hs_ref, *, nseq, seqlen):
    cl = LRU_C * jax.nn.log_sigmoid(lam_ref[...])
    full = [conv0_ref[j] for j in range(CONV_W - 1)]
    full += [xa_ref[pl.ds(t, nseq, stride=seqlen), :] for t in range(seqlen)]
    h = h0_ref[...]
    for t in range(seqlen):
        xc = cb_ref[...] + cw_ref[0:1, :] * full[t]
        for j in range(1, CONV_W):
            xc = xc + cw_ref[j:j + 1, :] * full[t + j]
        a, mult, ing = _lru_coeffs(xc, wbd_ref, bg_ref, cl, 0)
        h = a * h + mult * ing * xc
        hs_ref[pl.ds(t, nseq, stride=seqlen), :] = h
    ya_ref[...] = (hs_ref[...] * ga_ref[...].astype(F32)).astype(BF16)
    for j in range(CONV_W - 1):
        convout_ref[j] = full[seqlen + j]
    hout_ref[...] = h


def _lru_step(xa, act, nseq, seqlen, conv0_t, h0, lw):
    w = xa.shape[1]
    rows = nseq * seqlen
    ng = w // LANES
    col = lambda g: (0, g)
    return pl.pallas_call(
        functools.partial(_lru_step_kernel, nseq=nseq, seqlen=seqlen),
        name="lru_step",
        grid=(ng,),
        in_specs=[
            pl.BlockSpec((rows, LANES), col),
            pl.BlockSpec((rows, LANES), col),
            pl.BlockSpec((CONV_W - 1, nseq, LANES), lambda g: (0, 0, g)),
            pl.BlockSpec((nseq, LANES), col),
            pl.BlockSpec((CONV_W, LANES), col),
            pl.BlockSpec((1, LANES), col),
            pl.BlockSpec((1, LANES, 2 * LANES), lambda g: (g, 0, 0)),
            pl.BlockSpec((1, 1, 2 * LANES), lambda g: (g, 0, 0)),
            pl.BlockSpec((1, LANES), col),
        ],
        out_specs=[
            pl.BlockSpec((rows, LANES), col),
            pl.BlockSpec((CONV_W - 1, nseq, LANES), lambda g: (0, 0, g)),
            pl.BlockSpec((nseq, LANES), col),
        ],
        out_shape=[jax.ShapeDtypeStruct((rows, w), BF16),
                   jax.ShapeDtypeStruct((CONV_W - 1, nseq, w), F32),
                   jax.ShapeDtypeStruct((nseq, w), F32)],
        scratch_shapes=[pltpu.VMEM((rows, LANES), F32)],
        compiler_params=_cparams(1),
    )(xa, act, conv0_t, h0, lw["conv_w"], lw["conv_b"], lw["wbd"], lw["bg"], lw["lam"])


def _log_gamma(h):
    return math.log1p(-(2.0 ** (-5.0 - h)))


def _ret_kernel(q_ref, k_ref, v_ref, sg_ref, s0_ref, ng_ref, *refs, nb, clen, emit_y):
    if emit_y:
        y_ref, sout_ref, s_ref, dec_ref, rdec_ref = refs
    else:
        sout_ref, s_ref, dec_ref, rdec_ref = refs
    c = pl.program_id(1)
    rows = nb * clen

    @pl.when((pl.program_id(0) == 0) & (c == 0))
    def _():
        li = lax.broadcasted_iota(jnp.int32, (rows, rows), 0)
        mi = lax.broadcasted_iota(jnp.int32, (rows, rows), 1)
        keep = (li // clen == mi // clen) & (li >= mi)
        diff = jnp.where(keep, li - mi, 0).astype(F32)
        tpos = (lax.broadcasted_iota(jnp.int32, (rows, LANES), 0) % clen).astype(F32)
        for h in range(RET_HEADS):
            lg = _log_gamma(h)
            dec_ref[h] = jnp.where(keep, jnp.exp(lg * diff), 0.0)
            rdec_ref[0, h] = jnp.exp(lg * (tpos + 1.0))
            rdec_ref[1, h] = jnp.exp(lg * (clen - 1.0 - tpos))

    @pl.when(c == 0)
    def _():
        for n in range(nb):
            s_ref[n] = s0_ref[n if s0_ref.shape[0] == nb else 0]

    seq_of_row = lax.broadcasted_iota(jnp.int32, (rows, nb * HEAD_DIM), 0) // clen
    seq_of_col = lax.broadcasted_iota(jnp.int32, (rows, nb * HEAD_DIM), 1) // HEAD_DIM
    own = seq_of_row == seq_of_col
    tn = (((0,), (0,)), ((), ()))
    nt = (((1,), (1,)), ((), ()))
    for h in range(RET_HEADS):
        sl = slice(h * HEAD_DIM, (h + 1) * HEAD_DIM)
        q, k, v = q_ref[:, sl], k_ref[:, sl], v_ref[:, sl]
        s_old = [s_ref[n, h] for n in range(nb)]
        if emit_y:
            scores = lax.dot_general(q, k, nt, preferred_element_type=F32) * dec_ref[h]
            o = jnp.dot(scores.astype(BF16), v, preferred_element_type=F32)
            s_cat = jnp.concatenate(s_old, axis=1).astype(BF16) if nb > 1 else s_old[0].astype(BF16)
            qs = jnp.dot(q, s_cat, preferred_element_type=F32)
            if nb > 1:
                qs = jnp.where(own, qs, 0.0)
                inter = qs[:, 0:HEAD_DIM]
                for n in range(1, nb):
                    inter = inter + qs[:, n * HEAD_DIM:(n + 1) * HEAD_DIM]
            else:
                inter = qs
            o = o + inter * rdec_ref[0, h]
            mu = jnp.mean(o, axis=-1, keepdims=True)
            dev = o - mu
            var = jnp.mean(dev * dev, axis=-1, keepdims=True)
            normed = dev * lax.rsqrt(var + EPS) * ng_ref[:, sl]
            y_ref[:, sl] = (sg_ref[:, sl].astype(F32) * normed).astype(BF16)
        kd = (k.astype(F32) * rdec_ref[1, h]).astype(BF16)
        if nb > 1:
            v_bd = jnp.where(own, jnp.concatenate([v.astype(F32)] * nb, axis=1), 0.0).astype(BF16)
        else:
            v_bd = v
        upd = lax.dot_general(kd, v_bd, tn, preferred_element_type=F32)
        g_chunk = math.exp(clen * _log_gamma(h))
        for n in range(nb):
            s_ref[n, h] = g_chunk * s_old[n] + upd[:, n * HEAD_DIM:(n + 1) * HEAD_DIM]

    @pl.when(c == pl.num_programs(1) - 1)
    def _():
        sout_ref[...] = s_ref[...]


def _retention(act, row0, nseq, seqlen, nb, clen, s0, norm_g, emit_y):
    w = RET_HEADS * HEAD_DIM
    rows = nb * clen
    nc = seqlen // clen
    assert nb == 1 or nc == 1
    rb0 = row0 // rows
    rmap = lambda col: (lambda b, c: (rb0 + b * nc + c, col))
    shared = s0.shape[0] != nseq
    s_map = (lambda b, c: (0, 0, 0, 0)) if shared else (lambda b, c: (b, 0, 0, 0))
    s_block = (1 if shared else nb, RET_HEADS, HEAD_DIM, HEAD_DIM)
    out_specs = [pl.BlockSpec((nb, RET_HEADS, HEAD_DIM, HEAD_DIM), lambda b, c: (b, 0, 0, 0))]
    out_shape = [jax.ShapeDtypeStruct((nseq, RET_HEADS, HEAD_DIM, HEAD_DIM), F32)]
    if emit_y:
        out_specs = [pl.BlockSpec((rows, w), lambda b, c: (b * nc + c, 0))] + out_specs
        out_shape = [jax.ShapeDtypeStruct((nseq * seqlen, w), BF16)] + out_shape
    return pl.pallas_call(
        functools.partial(_ret_kernel, nb=nb, clen=clen, emit_y=emit_y),
        name="retention",
        grid=(nseq // nb, nc),
        in_specs=[
            pl.BlockSpec((rows, w), rmap(_ACT_Q)),
            pl.BlockSpec((rows, w), rmap(_ACT_K)),
            pl.BlockSpec((rows, w), rmap(_ACT_V)),
            pl.BlockSpec((rows, w), rmap(_ACT_SILU)),
            pl.BlockSpec(s_block, s_map),
            pl.BlockSpec((1, w), lambda b, c: (0, 0)),
        ],
        out_specs=out_specs,
        out_shape=out_shape,
        scratch_shapes=[pltpu.VMEM((nb, RET_HEADS, HEAD_DIM, HEAD_DIM), F32),
                        pltpu.VMEM((RET_HEADS, rows, rows), F32),
                        pltpu.VMEM((2, RET_HEADS, rows, LANES), F32)],
        compiler_params=_cparams(2),
    )(act, act, act, act, s0, norm_g)


def _outproj_kernel(x_ref, ya_ref, yb_ref, sga_ref, sgb_ref, pa_ref, pb_ref, wo_ref, o_ref):
    ma = jnp.dot(ya_ref[...], pa_ref[...], preferred_element_type=F32)
    mb = jnp.dot(yb_ref[...], pb_ref[...], preferred_element_type=F32)
    merged = sga_ref[...].astype(F32) * ma + sgb_ref[...].astype(F32) * mb
    o_ref[...] = x_ref[...] + jnp.dot(merged.astype(BF16), wo_ref[...], preferred_element_type=F32)


def _outproj(x, ya, yb, act, pa, pb, wo, bm):
    m, d = x.shape
    w = ya.shape[1]
    nd = d // 1024
    row = lambda i: (i, 0)
    const = lambda i: (0, 0)
    single = pl.Buffered(1)
    return pl.pallas_call(
        _outproj_kernel,
        name="outproj",
        grid=(m // bm,),
        in_specs=[
            pl.BlockSpec((bm, d), row),
            pl.BlockSpec((bm, w), row),
            pl.BlockSpec((bm, w), row),
            pl.BlockSpec((bm, d), lambda i: (i, _ACT_SGA // nd)),
            pl.BlockSpec((bm, d), lambda i: (i, _ACT_SGB // nd)),
            pl.BlockSpec((w, d), const, pipeline_mode=single),
            pl.BlockSpec((w, d), const, pipeline_mode=single),
            pl.BlockSpec((d, d), const, pipeline_mode=single),
        ],
        out_specs=pl.BlockSpec((bm, d), row),
        out_shape=jax.ShapeDtypeStruct((m, d), F32),
        compiler_params=_cparams(1),
    )(x, ya, yb, act, act, pa, pb, wo)


def _ffn_kernel(x_ref, g_ref, wu_ref, wd_ref, gf_ref, o_ref, hn_ref):
    j = pl.program_id(1)

    @pl.when(j == 0)
    def _():
        x = x_ref[...]
        hn_ref[...] = _rmsnorm(x, g_ref[...]).astype(BF16)
        o_ref[...] = x

    u = jnp.dot(hn_ref[...], wu_ref[...], preferred_element_type=F32)
    r = jnp.square(jnp.maximum(u, 0.0)).astype(BF16)
    o_ref[...] += jnp.dot(r, wd_ref[...], preferred_element_type=F32)

    @pl.when(j == pl.num_programs(1) - 1)
    def _():
        o_ref[...] = _rmsnorm(o_ref[...], gf_ref[...])


def _ffn(x, g, wu, wd, gf, bm, bf):
    m, d = x.shape
    dff = wu.shape[1]
    return pl.pallas_call(
        _ffn_kernel,
        name="ffn",
        grid=(m // bm, dff // bf),
        in_specs=[
            pl.BlockSpec((bm, d), lambda i, j: (i, 0), pipeline_mode=pl.Buffered(1)),
            pl.BlockSpec((1, d), lambda i, j: (0, 0)),
            pl.BlockSpec((d, bf), lambda i, j: (0, j)),
            pl.BlockSpec((bf, d), lambda i, j: (j, 0)),
            pl.BlockSpec((1, d), lambda i, j: (0, 0)),
        ],
        out_specs=pl.BlockSpec((bm, d), lambda i, j: (i, 0)),
        out_shape=jax.ShapeDtypeStruct((m, d), F32),
        scratch_shapes=[pltpu.VMEM((bm, d), BF16)],
        compiler_params=_cparams(2),
    )(x, g, wu, wd, gf)


def _rope_tables(pos):
    inv = ROPE_BASE ** (-jnp.arange(0, HEAD_DIM, 2, dtype=F32) / HEAD_DIM)
    ang = pos.astype(F32)[:, None] * inv[None, :]
    cos, sin = jnp.cos(ang), jnp.sin(ang)
    return jnp.concatenate([cos, cos], axis=1), jnp.concatenate([-sin, sin], axis=1)


def _gate_weights(wa, wx, ba, bx):
    nblk, blk, _ = wa.shape
    per = LANES // blk
    ng = nblk // per

    def bd(wt):
        wt = wt.reshape(ng, per, blk, blk)
        eye = jnp.eye(per, dtype=wt.dtype)
        return jnp.einsum("gpcd,pq->gpcqd", wt, eye).reshape(ng, LANES, LANES)

    wbd = jnp.concatenate([bd(wa), bd(wx)], axis=2).astype(BF16)
    bg = jnp.concatenate([ba.reshape(ng, 1, LANES), bx.reshape(ng, 1, LANES)], axis=2)
    return wbd, bg


def kernel(x_prompt, x_sample, state_conv, state_lru, state_ret, meta_tokens, norm_mix_g, w_in, conv_w,
           conv_b, lru_wa, lru_ba, lru_wx, lru_bx, lru_lam, ret_norm_g, p_a, p_b, w_out, norm_ffn_g,
           w_up, w_down, norm_f_g):
    depth = w_in.shape[0]
    assert depth == 1
    nb_p, t_p, d = x_prompt.shape
    nb_s, t_s, _ = x_sample.shape
    w = conv_w.shape[-1]
    rows_p, rows_s = nb_p * t_p, nb_s * t_s
    past_len = 16384

    w_in_b = w_in[0].astype(BF16)
    pa_b, pb_b, wo_b = p_a[0].astype(BF16), p_b[0].astype(BF16), w_out[0].astype(BF16)
    wu_b, wd_b = w_up[0].astype(BF16), w_down[0].astype(BF16)
    g_mix, g_ffn, g_f = norm_mix_g[0][None], norm_ffn_g[0][None], norm_f_g[None]
    wbd, bg = _gate_weights(lru_wa[0], lru_wx[0], lru_ba[0], lru_bx[0])
    lw = dict(conv_w=conv_w[0], conv_b=conv_b[0][None], wbd=wbd, bg=bg, lam=lru_lam[0][None])
    ret_g = ret_norm_g[0][None]

    cos_p, sin_p = _rope_tables(N_META + jnp.arange(t_p, dtype=jnp.int32))
    pos_sm = jnp.concatenate([jnp.tile(past_len + jnp.arange(t_s, dtype=jnp.int32), nb_s),
                              jnp.arange(N_META, dtype=jnp.int32)])
    cos_s, sin_s = _rope_tables(pos_sm)
    x_p2 = x_prompt.reshape(rows_p, d)
    x_s2 = x_sample.reshape(rows_s, d)
    x_sm = jnp.concatenate([x_s2, meta_tokens.astype(x_sample.dtype)], axis=0)
    bm_p = 1024
    xa_p, act_p = _inproj(x_p2, g_mix, w_in_b, cos_p, sin_p, bm_p, t_p // bm_p)
    xa_s, act_s = _inproj(x_sm, g_mix, w_in_b, cos_s, sin_s, rows_s + N_META, 1)

    zc = jnp.zeros((1, CONV_W - 1, w), F32)
    zh = jnp.zeros((1, 1, w), F32)
    zs = jnp.zeros((1, RET_HEADS, HEAD_DIM, HEAD_DIM), F32)
    conv_m, h_m = _lru_seq(xa_s, act_s, rows_s, 1, N_META, N_META, zc, zh, lw, emit_y=False, first_pos_zero=True)
    (s_m,) = _retention(act_s, rows_s, 1, N_META, 1, N_META, zs, ret_g, emit_y=False)

    ya_p, conv_p, h_p = _lru_seq(xa_p, act_p, 0, nb_p, t_p, 256, conv_m, h_m, lw, emit_y=True, first_pos_zero=False)
    yb_p, s_p = _retention(act_p, 0, nb_p, t_p, 1, CHUNK, s_m, ret_g, emit_y=True)

    conv0_t = jnp.transpose(state_conv[0], (1, 0, 2))
    ya_s, conv_s_t, h_s = _lru_step(xa_s, act_s, nb_s, t_s, conv0_t, state_lru[0], lw)
    yb_s, s_s = _retention(act_s, 0, nb_s, t_s, 8, t_s, state_ret[0], ret_g, emit_y=True)

    x1_p = _outproj(x_p2, ya_p, yb_p, act_p, pa_b, pb_b, wo_b, 256)
    x1_s = _outproj(x_s2, ya_s, yb_s, act_s, pa_b, pb_b, wo_b, 256)
    y_p = _ffn(x1_p, g_ffn, wu_b, wd_b, g_f, 1024, 512)
    y_s = _ffn(x1_s, g_ffn, wu_b, wd_b, g_f, 1024, 512)

    return (y_p.reshape(nb_p, t_p, d), y_s.reshape(nb_s, t_s, d),
            conv_p[None], h_p.reshape(1, nb_p, w), s_p[None],
            jnp.transpose(conv_s_t, (1, 0, 2))[None], h_s[None], s_s[None])
```

```python
import functools
import math

import jax
import jax.numpy as jnp
from jax import lax
from jax.experimental import pallas as pl
from jax.experimental.pallas import tpu as pltpu

F32 = jnp.float32
BF16 = jnp.bfloat16

N_META = 16
PAST_LEN = 16384
LRU_BLOCKS = 16
CONV_W = 4
LRU_C = 8.0
RET_HEADS = 8
HEAD_DIM = 128
CHUNK = 128
ROPE_BASE = 10000.0
EPS = 1e-6

LANES = 128
VMEM_LIMIT = 56 << 20


def _cparams(n_axes):
    return pltpu.CompilerParams(dimension_semantics=("arbitrary",) * n_axes, vmem_limit_bytes=VMEM_LIMIT)


def _rmsnorm(x, g):
    return x * lax.rsqrt(jnp.mean(x * x, axis=-1, keepdims=True) + EPS) * g


def _sigmoid(x):
    return 1.0 / (1.0 + jnp.exp(-x))


def _row_chunks(rows, size):
    n = max(rows // size, 1)
    return [(c * size, size if c < n - 1 else rows - c * size) for c in range(n)]


_COL_GATE, _COL_Q, _COL_K, _COL_V, _COL_G, _COL_GATE_A = 1, 2, 3, 4, 5, 6
_ACT_GELU, _ACT_Q, _ACT_K, _ACT_V, _ACT_SGA, _ACT_SGB, _ACT_SILU = 0, 1, 2, 3, 4, 6, 8


def _act_block(kind):
    return jnp.where(kind <= _COL_V, kind - 1, jnp.where(kind == _COL_G, _ACT_SILU, kind - 2))


def _inproj_kernel(x_ref, g_ref, w_ref, cos_ref, sin_ref, *refs, nper, cast_w, n_side):
    side_src, refs = refs[:n_side], refs[n_side:]
    xa_ref, act_ref = refs[0], refs[1]
    refs = refs[2:]
    if cast_w:
        wb_ref, refs = refs[0], refs[1:]
    side_dst, (xn_ref,) = refs[:n_side], refs[n_side:]
    j = pl.program_id(1)
    kind = j // nper
    bm, bn = xa_ref.shape
    chunks = _row_chunks(bm, 256)

    for src, dst in zip(side_src, side_dst):
        dst[...] = src[...].astype(BF16)
    if cast_w:
        wb_ref[...] = w_ref[...].astype(BF16)
        w_bf = wb_ref
    else:
        w_bf = w_ref

    def project(epilogue, normalize=False):
        for r0, nr in chunks:
            rows = slice(r0, r0 + nr)
            if normalize:
                xn_ref[rows, :] = _rmsnorm(x_ref[rows, :], g_ref[...]).astype(BF16)
            epilogue(jnp.dot(xn_ref[rows, :], w_bf[...], preferred_element_type=F32), rows)

    def to_xa(acc, rows):
        xa_ref[rows, :] = acc

    def rope(scale):
        def epilogue(acc, rows):
            cos, sin = cos_ref[rows, :], sin_ref[rows, :]
            for h in range(bn // HEAD_DIM):
                a = acc[:, h * HEAD_DIM:(h + 1) * HEAD_DIM]
                r = a * cos + pltpu.roll(a, HEAD_DIM // 2, axis=1) * sin
                if scale is not None:
                    r = r * scale
                act_ref[rows, h * HEAD_DIM:(h + 1) * HEAD_DIM] = r.astype(BF16)
        return epilogue

    def elementwise(fn):
        def epilogue(acc, rows):
            act_ref[rows, :] = fn(acc).astype(BF16)
        return epilogue

    pl.when(j == 0)(lambda: project(to_xa, normalize=True))
    if nper > 1:
        pl.when((j > 0) & (kind == 0))(lambda: project(to_xa))
    pl.when(kind == _COL_GATE)(lambda: project(elementwise(jax.nn.gelu)))
    pl.when(kind == _COL_Q)(lambda: project(rope(None)))
    pl.when(kind == _COL_K)(lambda: project(rope(HEAD_DIM ** -0.5)))
    pl.when(kind == _COL_V)(lambda: project(elementwise(lambda a: a)))
    pl.when(kind == _COL_G)(lambda: project(elementwise(lambda a: a * _sigmoid(a))))
    pl.when(kind >= _COL_GATE_A)(lambda: project(elementwise(_sigmoid)))


def _inproj(x, g, w, cos, sin, bm, bn, table_blocks, side=()):
    m, d = x.shape
    n = w.shape[1]
    width = 1024
    nper = width // bn
    cast_w = w.dtype != BF16

    def act_map(i, j):
        kind = j // nper
        return i, jnp.where(kind == 0, 0, _act_block(kind) * nper + j % nper)

    in_specs = [
        pl.BlockSpec((bm, d), lambda i, j: (i, 0)),
        pl.BlockSpec((1, d), lambda i, j: (0, 0)),
        pl.BlockSpec((d, bn), lambda i, j: (0, j)),
        pl.BlockSpec((bm, LANES), lambda i, j: (i % table_blocks, 0)),
        pl.BlockSpec((bm, LANES), lambda i, j: (i % table_blocks, 0)),
    ]
    out_specs = [
        pl.BlockSpec((bm, bn), lambda i, j: (i, jnp.minimum(j, nper - 1))),
        pl.BlockSpec((bm, bn), act_map),
    ]
    out_shape = [jax.ShapeDtypeStruct((m, width), F32), jax.ShapeDtypeStruct((m, n - width), BF16)]
    if cast_w:
        out_specs.append(pl.BlockSpec((d, bn), lambda i, j: (0, j)))
        out_shape.append(jax.ShapeDtypeStruct(w.shape, BF16))
    for arr, block, index_map in side:
        in_specs.append(pl.BlockSpec(block, index_map))
        out_specs.append(pl.BlockSpec(block, index_map))
        out_shape.append(jax.ShapeDtypeStruct(arr.shape, BF16))
    return pl.pallas_call(
        functools.partial(_inproj_kernel, nper=nper, cast_w=cast_w, n_side=len(side)),
        name="inproj",
        grid=(m // bm, n // bn),
        in_specs=in_specs,
        out_specs=out_specs,
        out_shape=out_shape,
        scratch_shapes=[pltpu.VMEM((bm, d), BF16)],
        compiler_params=_cparams(2),
    )(x, g, w, cos, sin, *[arr for arr, _, _ in side])


def _lru_coeffs(xc, wbd_ref, bg_ref, cl_row, group):
    gates = jnp.dot(xc.astype(BF16), wbd_ref[group], preferred_element_type=F32) + bg_ref[group]
    rec = _sigmoid(gates[:, :LANES])
    ing = _sigmoid(gates[:, LANES:])
    log_a = rec * cl_row
    a = jnp.exp(log_a)
    mult = jnp.sqrt(jnp.tanh(-log_a) * (a * a + 1.0))
    return a, mult, ing


def _lru_seq_kernel(xa_ref, ga_ref, conv0_ref, h0_ref, cw_ref, cb_ref, wbd_ref, bg_ref, lam_ref,
                    *refs, tr, emit_y, first_pos_zero):
    if emit_y:
        ya_ref, convout_ref, hout_ref, halo_ref, h_ref, a_ref, b_ref = refs
    else:
        convout_ref, hout_ref, halo_ref, h_ref, a_ref, b_ref = refs
    t = pl.program_id(1)
    halo = 8

    @pl.when(t == 0)
    def _():
        halo_ref[halo - 3:halo, :] = conv0_ref[0]
        h_ref[...] = h0_ref[0]

    halo_ref[halo:halo + tr, :] = xa_ref[...]
    xc = cb_ref[...] + cw_ref[0:1, :] * halo_ref[halo - 3:halo - 3 + tr, :]
    for j in range(1, CONV_W):
        xc = xc + cw_ref[j:j + 1, :] * halo_ref[halo - 3 + j:halo - 3 + j + tr, :]
    halo_ref[halo - 3:halo, :] = halo_ref[halo + tr - 3:halo + tr, :]

    cl = LRU_C * jax.nn.log_sigmoid(lam_ref[...])
    for g in range(xc.shape[1] // LANES):
        sl = slice(g * LANES, (g + 1) * LANES)
        a, mult, ing = _lru_coeffs(xc[:, sl], wbd_ref, bg_ref, cl[:, sl], g)
        if first_pos_zero:
            row = lax.broadcasted_iota(jnp.int32, a.shape, 0) + t * tr
            mult = jnp.where(row == 0, 1.0, mult)
        a_ref[:, sl] = a
        b_ref[:, sl] = mult * ing * xc[:, sl]

    def step(r, h):
        h = a_ref[pl.ds(r, 1), :] * h + b_ref[pl.ds(r, 1), :]
        b_ref[pl.ds(r, 1), :] = h
        return h

    h_ref[...] = lax.fori_loop(0, tr, step, h_ref[...])
    if emit_y:
        ya_ref[...] = (b_ref[...] * ga_ref[...].astype(F32)).astype(BF16)

    @pl.when(t == pl.num_programs(1) - 1)
    def _():
        convout_ref[0] = halo_ref[halo - 3:halo, :]
        hout_ref[0] = h_ref[...]


def _lru_seq(xa, act, row0, nseq, seqlen, tr, conv0, h0, lw, emit_y, first_pos_zero):
    w = xa.shape[1]
    nt = seqlen // tr
    rb0 = row0 // tr
    ng = w // LANES
    state_map = (lambda b, t: (b, 0, 0)) if conv0.shape[0] == nseq else (lambda b, t: (0, 0, 0))
    const2 = lambda b, t: (0, 0)
    const3 = lambda b, t: (0, 0, 0)
    out_specs = [pl.BlockSpec((1, CONV_W - 1, w), lambda b, t: (b, 0, 0)),
                 pl.BlockSpec((1, 1, w), lambda b, t: (b, 0, 0))]
    out_shape = [jax.ShapeDtypeStruct((nseq, CONV_W - 1, w), F32), jax.ShapeDtypeStruct((nseq, 1, w), F32)]
    if emit_y:
        out_specs = [pl.BlockSpec((tr, w), lambda b, t: (b * nt + t, 0))] + out_specs
        out_shape = [jax.ShapeDtypeStruct((nseq * seqlen, w), BF16)] + out_shape
    return pl.pallas_call(
        functools.partial(_lru_seq_kernel, tr=tr, emit_y=emit_y, first_pos_zero=first_pos_zero),
        name="lru_seq",
        grid=(nseq, nt),
        in_specs=[
            pl.BlockSpec((tr, w), lambda b, t: (rb0 + b * nt + t, 0)),
            pl.BlockSpec((tr, w), lambda b, t: (rb0 + b * nt + t, _ACT_GELU)),
            pl.BlockSpec((1, CONV_W - 1, w), state_map),
            pl.BlockSpec((1, 1, w), state_map),
            pl.BlockSpec((CONV_W, w), const2),
            pl.BlockSpec((1, w), const2),
            pl.BlockSpec((ng, LANES, 2 * LANES), const3),
            pl.BlockSpec((ng, 1, 2 * LANES), const3),
            pl.BlockSpec((1, w), const2),
        ],
        out_specs=out_specs,
        out_shape=out_shape,
        scratch_shapes=[pltpu.VMEM((8 + tr, w), F32), pltpu.VMEM((1, w), F32),
                        pltpu.VMEM((tr, w), F32), pltpu.VMEM((tr, w), F32)],
        compiler_params=_cparams(2),
    )(xa, act, conv0, h0, lw["conv_w"], lw["conv_b"], lw["wbd"], lw["bg"], lw["lam"])


def _lru_step_kernel(xa_ref, ga_ref, conv0_ref, h0_ref, cw_ref, cb_ref, wbd_ref, bg_ref, lam_ref,
                     ya_ref, convout_ref, hout_ref, hs_ref, *, nseq, seqlen):
    cl = LRU_C * jax.nn.log_sigmoid(lam_ref[...])
    full = [conv0_ref[j] for j in range(CONV_W - 1)]
    full += [xa_ref[pl.ds(t, nseq, stride=seqlen), :] for t in range(seqlen)]
    h = h0_ref[...]
    for t in range(seqlen):
        xc = cb_ref[...] + cw_ref[0:1, :] * full[t]
        for j in range(1, CONV_W):
            xc = xc + cw_ref[j:j + 1, :] * full[t + j]
        a, mult, ing = _lru_coeffs(xc, wbd_ref, bg_ref, cl, 0)
        h = a * h + mult * ing * xc
        hs_ref[pl.ds(t, nseq, stride=seqlen), :] = h
    ya_ref[...] = (hs_ref[...] * ga_ref[...].astype(F32)).astype(BF16)
    for j in range(CONV_W - 1):
        convout_ref[j] = full[seqlen + j]
    hout_ref[...] = h


def _lru_step(xa, act, nseq, seqlen, conv0_t, h0, lw):
    w = xa.shape[1]
    rows = nseq * seqlen
    ng = w // LANES
    col = lambda g: (0, g)
    return pl.pallas_call(
        functools.partial(_lru_step_kernel, nseq=nseq, seqlen=seqlen),
        name="lru_step",
        grid=(ng,),
        in_specs=[
            pl.BlockSpec((rows, LANES), col),
            pl.BlockSpec((rows, LANES), col),
            pl.BlockSpec((CONV_W - 1, nseq, LANES), lambda g: (0, 0, g)),
            pl.BlockSpec((nseq, LANES), col),
            pl.BlockSpec((CONV_W, LANES), col),
            pl.BlockSpec((1, LANES), col),
            pl.BlockSpec((1, LANES, 2 * LANES), lambda g: (g, 0, 0)),
            pl.BlockSpec((1, 1, 2 * LANES), lambda g: (g, 0, 0)),
            pl.BlockSpec((1, LANES), col),
        ],
        out_specs=[
            pl.BlockSpec((rows, LANES), col),
            pl.BlockSpec((CONV_W - 1, nseq, LANES), lambda g: (0, 0, g)),
            pl.BlockSpec((nseq, LANES), col),
        ],
        out_shape=[jax.ShapeDtypeStruct((rows, w), BF16),
                   jax.ShapeDtypeStruct((CONV_W - 1, nseq, w), F32),
                   jax.ShapeDtypeStruct((nseq, w), F32)],
        scratch_shapes=[pltpu.VMEM((rows, LANES), F32)],
        compiler_params=_cparams(1),
    )(xa, act, conv0_t, h0, lw["conv_w"], lw["conv_b"], lw["wbd"], lw["bg"], lw["lam"])


def _log_gamma(h):
    return math.log1p(-(2.0 ** (-5.0 - h)))


def _ret_kernel(q_ref, k_ref, v_ref, sg_ref, s0_ref, ng_ref, *refs, nb, clen, emit_y):
    if emit_y:
        y_ref, sout_ref, s_ref, dec_ref, rdec_ref = refs
    else:
        sout_ref, s_ref, dec_ref, rdec_ref = refs
    c = pl.program_id(1)
    rows = nb * clen
    heads = range(RET_HEADS)

    @pl.when((pl.program_id(0) == 0) & (c == 0))
    def _():
        li = lax.broadcasted_iota(jnp.int32, (rows, rows), 0)
        mi = lax.broadcasted_iota(jnp.int32, (rows, rows), 1)
        keep = (li // clen == mi // clen) & (li >= mi)
        diff = jnp.where(keep, li - mi, 0).astype(F32)
        tpos = (lax.broadcasted_iota(jnp.int32, (rows, LANES), 0) % clen).astype(F32)
        for h in heads:
            lg = _log_gamma(h)
            dec_ref[h] = jnp.where(keep, jnp.exp(lg * diff), 0.0)
            rdec_ref[0, h] = jnp.exp(lg * (tpos + 1.0))
            rdec_ref[1, h] = jnp.exp(lg * (clen - 1.0 - tpos))

    @pl.when(c == 0)
    def _():
        for n in range(nb):
            s_ref[n] = s0_ref[n if s0_ref.shape[0] == nb else 0]

    if nb > 1:
        seq_of_row = lax.broadcasted_iota(jnp.int32, (rows, nb * HEAD_DIM), 0) // clen
        seq_of_col = lax.broadcasted_iota(jnp.int32, (rows, nb * HEAD_DIM), 1) // HEAD_DIM
        own = seq_of_row == seq_of_col
    tn = (((0,), (0,)), ((), ()))
    nt = (((1,), (1,)), ((), ()))
    hsl = [slice(h * HEAD_DIM, (h + 1) * HEAD_DIM) for h in heads]
    q = [q_ref[:, sl] for sl in hsl]
    k = [k_ref[:, sl] for sl in hsl]
    v = [v_ref[:, sl] for sl in hsl]
    s_old = [[s_ref[n, h] for n in range(nb)] for h in heads]

    if emit_y:
        scores = [lax.dot_general(q[h], k[h], nt, preferred_element_type=F32) for h in heads]
        inter = []
        for h in heads:
            s_cat = jnp.concatenate(s_old[h], axis=1) if nb > 1 else s_old[h][0]
            qs = jnp.dot(q[h], s_cat.astype(BF16), preferred_element_type=F32)
            if nb > 1:
                qs = jnp.where(own, qs, 0.0)
                qs = sum(qs[:, n * HEAD_DIM:(n + 1) * HEAD_DIM] for n in range(nb))
            inter.append(qs)
    for h in heads:
        kd = (k[h].astype(F32) * rdec_ref[1, h]).astype(BF16)
        if nb > 1:
            v_bd = jnp.where(own, jnp.concatenate([v[h].astype(F32)] * nb, axis=1), 0.0).astype(BF16)
        else:
            v_bd = v[h]
        upd = lax.dot_general(kd, v_bd, tn, preferred_element_type=F32)
        g_chunk = math.exp(clen * _log_gamma(h))
        for n in range(nb):
            s_ref[n, h] = g_chunk * s_old[h][n] + upd[:, n * HEAD_DIM:(n + 1) * HEAD_DIM]

    if emit_y:
        for h in heads:
            p = (scores[h] * dec_ref[h]).astype(BF16)
            o = jnp.dot(p, v[h], preferred_element_type=F32) + inter[h] * rdec_ref[0, h]
            mu = jnp.mean(o, axis=-1, keepdims=True)
            dev = o - mu
            var = jnp.mean(dev * dev, axis=-1, keepdims=True)
            normed = dev * lax.rsqrt(var + EPS) * ng_ref[:, hsl[h]]
            y_ref[:, hsl[h]] = (sg_ref[:, hsl[h]].astype(F32) * normed).astype(BF16)

    @pl.when(c == pl.num_programs(1) - 1)
    def _():
        sout_ref[...] = s_ref[...]


def _retention(act, row0, nseq, seqlen, nb, clen, s0, norm_g, emit_y):
    w = RET_HEADS * HEAD_DIM
    rows = nb * clen
    nc = seqlen // clen
    assert nb == 1 or nc == 1
    rb0 = row0 // rows
    rmap = lambda col: (lambda b, c: (rb0 + b * nc + c, col))
    shared = s0.shape[0] != nseq
    s_map = (lambda b, c: (0, 0, 0, 0)) if shared else (lambda b, c: (b, 0, 0, 0))
    s_block = (1 if shared else nb, RET_HEADS, HEAD_DIM, HEAD_DIM)
    out_specs = [pl.BlockSpec((nb, RET_HEADS, HEAD_DIM, HEAD_DIM), lambda b, c: (b, 0, 0, 0))]
    out_shape = [jax.ShapeDtypeStruct((nseq, RET_HEADS, HEAD_DIM, HEAD_DIM), F32)]
    if emit_y:
        out_specs = [pl.BlockSpec((rows, w), lambda b, c: (b * nc + c, 0))] + out_specs
        out_shape = [jax.ShapeDtypeStruct((nseq * seqlen, w), BF16)] + out_shape
    return pl.pallas_call(
        functools.partial(_ret_kernel, nb=nb, clen=clen, emit_y=emit_y),
        name="retention",
        grid=(nseq // nb, nc),
        in_specs=[
            pl.BlockSpec((rows, w), rmap(_ACT_Q)),
            pl.BlockSpec((rows, w), rmap(_ACT_K)),
            pl.BlockSpec((rows, w), rmap(_ACT_V)),
            pl.BlockSpec((rows, w), rmap(_ACT_SILU)),
            pl.BlockSpec(s_block, s_map),
            pl.BlockSpec((1, w), lambda b, c: (0, 0)),
        ],
        out_specs=out_specs,
        out_shape=out_shape,
        scratch_shapes=[pltpu.VMEM((nb, RET_HEADS, HEAD_DIM, HEAD_DIM), F32),
                        pltpu.VMEM((RET_HEADS, rows, rows), F32),
                        pltpu.VMEM((2, RET_HEADS, rows, LANES), F32)],
        compiler_params=_cparams(2),
    )(act, act, act, act, s0, norm_g)


def _outproj_kernel(x_ref, ya_ref, yb_ref, sga_ref, sgb_ref, pa_ref, pb_ref, wo_ref, *refs, n_side):
    side_src, o_ref, side_dst = refs[:n_side], refs[n_side], refs[n_side + 1:]
    for src, dst in zip(side_src, side_dst):
        dst[...] = src[...].astype(BF16)
    ma = jnp.dot(ya_ref[...], pa_ref[...], preferred_element_type=F32)
    mb = jnp.dot(yb_ref[...], pb_ref[...], preferred_element_type=F32)
    merged = sga_ref[...].astype(F32) * ma + sgb_ref[...].astype(F32) * mb
    o_ref[...] = x_ref[...] + jnp.dot(merged.astype(BF16), wo_ref[...], preferred_element_type=F32)


def _outproj(x, ya, yb, act, pa, pb, wo, bm, side=()):
    m, d = x.shape
    w = ya.shape[1]
    nd = d // 1024
    row = lambda i: (i, 0)
    const = lambda i: (0, 0)
    single = pl.Buffered(1)
    in_specs = [
        pl.BlockSpec((bm, d), row),
        pl.BlockSpec((bm, w), row),
        pl.BlockSpec((bm, w), row),
        pl.BlockSpec((bm, d), lambda i: (i, _ACT_SGA // nd)),
        pl.BlockSpec((bm, d), lambda i: (i, _ACT_SGB // nd)),
        pl.BlockSpec((w, d), const, pipeline_mode=single),
        pl.BlockSpec((w, d), const, pipeline_mode=single),
        pl.BlockSpec((d, d), const, pipeline_mode=single),
    ]
    out_specs = [pl.BlockSpec((bm, d), row)]
    out_shape = [jax.ShapeDtypeStruct((m, d), F32)]
    for arr, block, index_map in side:
        in_specs.append(pl.BlockSpec(block, index_map))
        out_specs.append(pl.BlockSpec(block, index_map))
        out_shape.append(jax.ShapeDtypeStruct(arr.shape, BF16))
    return pl.pallas_call(
        functools.partial(_outproj_kernel, n_side=len(side)),
        name="outproj",
        grid=(m // bm,),
        in_specs=in_specs,
        out_specs=out_specs,
        out_shape=out_shape,
        compiler_params=_cparams(1),
    )(x, ya, yb, act, act, pa, pb, wo, *[arr for arr, _, _ in side])


def _ffn_kernel(x_ref, g_ref, wu_ref, wd_ref, gf_ref, o_ref, hn_ref):
    j = pl.program_id(1)

    @pl.when(j == 0)
    def _():
        x = x_ref[...]
        hn_ref[...] = _rmsnorm(x, g_ref[...]).astype(BF16)
        o_ref[...] = x

    u = jnp.dot(hn_ref[...], wu_ref[...], preferred_element_type=F32)
    r = jnp.square(jnp.maximum(u, 0.0)).astype(BF16)
    o_ref[...] += jnp.dot(r, wd_ref[...], preferred_element_type=F32)

    @pl.when(j == pl.num_programs(1) - 1)
    def _():
        o_ref[...] = _rmsnorm(o_ref[...], gf_ref[...])


def _ffn(x, g, wu, wd, gf, bm, bf):
    m, d = x.shape
    dff = wu.shape[1]
    return pl.pallas_call(
        _ffn_kernel,
        name="ffn",
        grid=(m // bm, dff // bf),
        in_specs=[
            pl.BlockSpec((bm, d), lambda i, j: (i, 0), pipeline_mode=pl.Buffered(1)),
            pl.BlockSpec((1, d), lambda i, j: (0, 0)),
            pl.BlockSpec((d, bf), lambda i, j: (0, j)),
            pl.BlockSpec((bf, d), lambda i, j: (j, 0)),
            pl.BlockSpec((1, d), lambda i, j: (0, 0)),
        ],
        out_specs=pl.BlockSpec((bm, d), lambda i, j: (i, 0)),
        out_shape=jax.ShapeDtypeStruct((m, d), F32),
        scratch_shapes=[pltpu.VMEM((bm, d), BF16)],
        compiler_params=_cparams(2),
    )(x, g, wu, wd, gf)


def _rope_tables(pos):
    inv = ROPE_BASE ** (-jnp.arange(0, HEAD_DIM, 2, dtype=F32) / HEAD_DIM)
    ang = pos.astype(F32)[:, None] * inv[None, :]
    cos, sin = jnp.cos(ang), jnp.sin(ang)
    return jnp.concatenate([cos, cos], axis=1), jnp.concatenate([-sin, sin], axis=1)


def _gate_weights(wa, wx, ba, bx):
    nblk, blk, _ = wa.shape
    per = LANES // blk
    ng = nblk // per

    def bd(wt):
        wt = wt.reshape(ng, per, blk, blk)
        eye = jnp.eye(per, dtype=wt.dtype)
        return jnp.einsum("gpcd,pq->gpcqd", wt, eye).reshape(ng, LANES, LANES)

    wbd = jnp.concatenate([bd(wa), bd(wx)], axis=2).astype(BF16)
    bg = jnp.concatenate([ba.reshape(ng, 1, LANES), bx.reshape(ng, 1, LANES)], axis=2)
    return wbd, bg


def kernel(x_prompt, x_sample, state_conv, state_lru, state_ret, meta_tokens, norm_mix_g, w_in, conv_w,
           conv_b, lru_wa, lru_ba, lru_wx, lru_bx, lru_lam, ret_norm_g, p_a, p_b, w_out, norm_ffn_g,
           w_up, w_down, norm_f_g):
    assert w_in.shape[0] == 1
    nb_p, t_p, d = x_prompt.shape
    nb_s, t_s, _ = x_sample.shape
    w = conv_w.shape[-1]
    dff = w_up.shape[-1]
    rows_p, rows_s = nb_p * t_p, nb_s * t_s

    pa_b, pb_b, wo_b = p_a[0].astype(BF16), p_b[0].astype(BF16), w_out[0].astype(BF16)
    g_mix, g_ffn, g_f = norm_mix_g[0][None], norm_ffn_g[0][None], norm_f_g[None]
    wbd, bg = _gate_weights(lru_wa[0], lru_wx[0], lru_ba[0], lru_bx[0])
    lw = dict(conv_w=conv_w[0], conv_b=conv_b[0][None], wbd=wbd, bg=bg, lam=lru_lam[0][None])
    ret_g = ret_norm_g[0][None]

    cos_p, sin_p = _rope_tables(N_META + jnp.arange(t_p, dtype=jnp.int32))
    pos_sm = jnp.concatenate([jnp.tile(PAST_LEN + jnp.arange(t_s, dtype=jnp.int32), nb_s),
                              jnp.arange(N_META, dtype=jnp.int32)])
    cos_s, sin_s = _rope_tables(pos_sm)
    x_p2 = x_prompt.reshape(rows_p, d)
    x_s2 = x_sample.reshape(rows_s, d)
    x_sm = jnp.concatenate([x_s2, meta_tokens.astype(x_sample.dtype)], axis=0)
    xa_s, act_s, w_in_b = _inproj(x_sm, g_mix, w_in[0], cos_s, sin_s, rows_s + N_META, 512, 1)
    bm_p = 1024
    n_up = 8
    side_up = (w_up[0], (d // (rows_p // bm_p), dff // n_up), lambda i, j: (i, jnp.minimum(j, n_up - 1)))
    xa_p, act_p, wu_b = _inproj(x_p2, g_mix, w_in_b, cos_p, sin_p, bm_p, 1024, t_p // bm_p, side=(side_up,))

    zc = jnp.zeros((1, CONV_W - 1, w), F32)
    zh = jnp.zeros((1, 1, w), F32)
    zs = jnp.zeros((1, RET_HEADS, HEAD_DIM, HEAD_DIM), F32)
    conv_m, h_m = _lru_seq(xa_s, act_s, rows_s, 1, N_META, N_META, zc, zh, lw, emit_y=False, first_pos_zero=True)
    (s_m,) = _retention(act_s, rows_s, 1, N_META, 1, N_META, zs, ret_g, emit_y=False)

    ya_p, conv_p, h_p = _lru_seq(xa_p, act_p, 0, nb_p, t_p, 256, conv_m, h_m, lw, emit_y=True, first_pos_zero=False)
    yb_p, s_p = _retention(act_p, 0, nb_p, t_p, 1, CHUNK, s_m, ret_g, emit_y=True)

    conv0_t = jnp.transpose(state_conv[0], (1, 0, 2))
    ya_s, conv_s_t, h_s = _lru_step(xa_s, act_s, nb_s, t_s, conv0_t, state_lru[0], lw)
    yb_s, s_s = _retention(act_s, 0, nb_s, t_s, 8, t_s, state_ret[0], ret_g, emit_y=True)

    bm_o = 256
    side_down = (w_down[0], (dff // (rows_p // bm_o), d), lambda i: (i, 0))
    x1_p, wd_b = _outproj(x_p2, ya_p, yb_p, act_p, pa_b, pb_b, wo_b, bm_o, side=(side_down,))
    (x1_s,) = _outproj(x_s2, ya_s, yb_s, act_s, pa_b, pb_b, wo_b, bm_o)
    y_p = _ffn(x1_p, g_ffn, wu_b, wd_b, g_f, 1024, 1024)
    y_s = _ffn(x1_s, g_ffn, wu_b, wd_b, g_f, 1024, 1024)

    return (y_p.reshape(nb_p, t_p, d), y_s.reshape(nb_s, t_s, d),
            conv_p[None], h_p.reshape(1, nb_p, w), s_p[None],
            jnp.transpose(conv_s_t, (1, 0, 2))[None], h_s[None], s_s[None])
```

```python
import functools
import math

import jax
import jax.numpy as jnp
from jax import lax
from jax.experimental import pallas as pl
from jax.experimental.pallas import tpu as pltpu

F32 = jnp.float32
BF16 = jnp.bfloat16

N_META = 16
PAST_LEN = 16384
LRU_BLOCKS = 16
CONV_W = 4
LRU_C = 8.0
RET_HEADS = 8
HEAD_DIM = 128
CHUNK = 128
ROPE_BASE = 10000.0
EPS = 1e-6

LANES = 128
VMEM_LIMIT = 56 << 20


def _cparams(n_axes):
    return pltpu.CompilerParams(dimension_semantics=("arbitrary",) * n_axes, vmem_limit_bytes=VMEM_LIMIT)


def _rmsnorm(x, g):
    return x * lax.rsqrt(jnp.mean(x * x, axis=-1, keepdims=True) + EPS) * g


def _sigmoid(x):
    return 1.0 / (1.0 + jnp.exp(-x))


def _row_chunks(rows, size):
    n = max(rows // size, 1)
    return [(c * size, size if c < n - 1 else rows - c * size) for c in range(n)]


_COL_GATE, _COL_Q, _COL_K, _COL_V, _COL_G, _COL_GATE_A = 1, 2, 3, 4, 5, 6
_ACT_GELU, _ACT_Q, _ACT_K, _ACT_V, _ACT_SGA, _ACT_SGB, _ACT_SILU = 0, 1, 2, 3, 4, 6, 8


def _act_block(kind):
    return jnp.where(kind <= _COL_V, kind - 1, jnp.where(kind == _COL_G, _ACT_SILU, kind - 2))


def _inproj_kernel(x_ref, g_ref, w_ref, cos_ref, sin_ref, *refs, nper, cast_w, n_side, has_tail):
    if has_tail:
        tail_ref, refs = refs[0], refs[1:]
    side_src, refs = refs[:n_side], refs[n_side:]
    xa_ref, act_ref = refs[0], refs[1]
    refs = refs[2:]
    if cast_w:
        wb_ref, refs = refs[0], refs[1:]
    side_dst, (xn_ref,) = refs[:n_side], refs[n_side:]
    j = pl.program_id(1)
    kind = j // nper
    bm, bn = xa_ref.shape
    chunks = _row_chunks(bm, 256)

    for src, dst in zip(side_src, side_dst):
        dst[...] = src[...].astype(BF16)
    if cast_w:
        wb_ref[...] = w_ref[...].astype(BF16)
        w_bf = wb_ref
    else:
        w_bf = w_ref

    def project(epilogue, normalize=False):
        for r0, nr in chunks:
            rows = slice(r0, r0 + nr)
            if normalize:
                own = slice(r0, min(r0 + nr, x_ref.shape[0]))
                xn_ref[own, :] = _rmsnorm(x_ref[own, :], g_ref[...]).astype(BF16)
                if own.stop < r0 + nr:
                    xn_ref[own.stop:r0 + nr, :] = _rmsnorm(tail_ref[...], g_ref[...]).astype(BF16)
            epilogue(jnp.dot(xn_ref[rows, :], w_bf[...], preferred_element_type=F32), rows)

    def to_xa(acc, rows):
        xa_ref[rows, :] = acc

    def rope(scale):
        def epilogue(acc, rows):
            cos, sin = cos_ref[rows, :], sin_ref[rows, :]
            for h in range(bn // HEAD_DIM):
                a = acc[:, h * HEAD_DIM:(h + 1) * HEAD_DIM]
                r = a * cos + pltpu.roll(a, HEAD_DIM // 2, axis=1) * sin
                if scale is not None:
                    r = r * scale
                act_ref[rows, h * HEAD_DIM:(h + 1) * HEAD_DIM] = r.astype(BF16)
        return epilogue

    def elementwise(fn):
        def epilogue(acc, rows):
            act_ref[rows, :] = fn(acc).astype(BF16)
        return epilogue

    pl.when(j == 0)(lambda: project(to_xa, normalize=True))
    if nper > 1:
        pl.when((j > 0) & (kind == 0))(lambda: project(to_xa))
    pl.when(kind == _COL_GATE)(lambda: project(elementwise(jax.nn.gelu)))
    pl.when(kind == _COL_Q)(lambda: project(rope(None)))
    pl.when(kind == _COL_K)(lambda: project(rope(HEAD_DIM ** -0.5)))
    pl.when(kind == _COL_V)(lambda: project(elementwise(lambda a: a)))
    pl.when(kind == _COL_G)(lambda: project(elementwise(lambda a: a * _sigmoid(a))))
    pl.when(kind >= _COL_GATE_A)(lambda: project(elementwise(_sigmoid)))


def _inproj(x, g, w, cos, sin, bm, bn, table_blocks, side=(), tail=None):
    xrows, d = x.shape
    m = xrows if tail is None else xrows + tail.shape[0]
    assert tail is None or m == bm
    n = w.shape[1]
    width = 1024
    nper = width // bn
    cast_w = w.dtype != BF16

    def act_map(i, j):
        kind = j // nper
        return i, jnp.where(kind == 0, 0, _act_block(kind) * nper + j % nper)

    in_specs = [
        pl.BlockSpec((min(bm, xrows), d), lambda i, j: (i, 0)),
        pl.BlockSpec((1, d), lambda i, j: (0, 0)),
        pl.BlockSpec((d, bn), lambda i, j: (0, j)),
        pl.BlockSpec((bm, LANES), lambda i, j: (i % table_blocks, 0)),
        pl.BlockSpec((bm, LANES), lambda i, j: (i % table_blocks, 0)),
    ]
    if tail is not None:
        in_specs.append(pl.BlockSpec(tail.shape, lambda i, j: (0, 0)))
    out_specs = [
        pl.BlockSpec((bm, bn), lambda i, j: (i, jnp.minimum(j, nper - 1))),
        pl.BlockSpec((bm, bn), act_map),
    ]
    out_shape = [jax.ShapeDtypeStruct((m, width), F32), jax.ShapeDtypeStruct((m, n - width), BF16)]
    if cast_w:
        out_specs.append(pl.BlockSpec((d, bn), lambda i, j: (0, j)))
        out_shape.append(jax.ShapeDtypeStruct(w.shape, BF16))
    for arr, block, index_map in side:
        in_specs.append(pl.BlockSpec(block, index_map))
        out_specs.append(pl.BlockSpec(block, index_map))
        out_shape.append(jax.ShapeDtypeStruct(arr.shape, BF16))
    return pl.pallas_call(
        functools.partial(_inproj_kernel, nper=nper, cast_w=cast_w, n_side=len(side), has_tail=tail is not None),
        name="inproj",
        grid=(m // bm, n // bn),
        in_specs=in_specs,
        out_specs=out_specs,
        out_shape=out_shape,
        scratch_shapes=[pltpu.VMEM((bm, d), BF16)],
        compiler_params=_cparams(2),
    )(x, g, w, cos, sin, *([] if tail is None else [tail]), *[arr for arr, _, _ in side])


def _lru_coeffs(xc, wbd_ref, bg_ref, cl_row, group):
    gates = jnp.dot(xc.astype(BF16), wbd_ref[group], preferred_element_type=F32) + bg_ref[group]
    rec = _sigmoid(gates[:, :LANES])
    ing = _sigmoid(gates[:, LANES:])
    log_a = rec * cl_row
    a = jnp.exp(log_a)
    mult = jnp.sqrt(jnp.tanh(-log_a) * (a * a + 1.0))
    return a, mult, ing


def _lru_seq_kernel(xa_ref, ga_ref, conv0_ref, h0_ref, cw_ref, cb_ref, wbd_ref, bg_ref, lam_ref,
                    *refs, tr, emit_y, first_pos_zero):
    if emit_y:
        ya_ref, convout_ref, hout_ref, halo_ref, h_ref, a_ref, b_ref = refs
    else:
        convout_ref, hout_ref, halo_ref, h_ref, a_ref, b_ref = refs
    t = pl.program_id(1)
    halo = 8

    @pl.when(t == 0)
    def _():
        halo_ref[halo - 3:halo, :] = conv0_ref[0]
        h_ref[...] = h0_ref[0]

    halo_ref[halo:halo + tr, :] = xa_ref[...]
    xc = cb_ref[...] + cw_ref[0:1, :] * halo_ref[halo - 3:halo - 3 + tr, :]
    for j in range(1, CONV_W):
        xc = xc + cw_ref[j:j + 1, :] * halo_ref[halo - 3 + j:halo - 3 + j + tr, :]
    halo_ref[halo - 3:halo, :] = halo_ref[halo + tr - 3:halo + tr, :]

    cl = LRU_C * jax.nn.log_sigmoid(lam_ref[...])
    for g in range(xc.shape[1] // LANES):
        sl = slice(g * LANES, (g + 1) * LANES)
        a, mult, ing = _lru_coeffs(xc[:, sl], wbd_ref, bg_ref, cl[:, sl], g)
        if first_pos_zero:
            row = lax.broadcasted_iota(jnp.int32, a.shape, 0) + t * tr
            mult = jnp.where(row == 0, 1.0, mult)
        a_ref[:, sl] = a
        b_ref[:, sl] = mult * ing * xc[:, sl]

    def step(r, h):
        h = a_ref[pl.ds(r, 1), :] * h + b_ref[pl.ds(r, 1), :]
        halo_ref[pl.ds(halo + r, 1), :] = h
        return h

    h_ref[...] = lax.fori_loop(0, tr, step, h_ref[...], unroll=8)
    if emit_y:
        ya_ref[...] = (halo_ref[halo:halo + tr, :] * ga_ref[...].astype(F32)).astype(BF16)

    @pl.when(t == pl.num_programs(1) - 1)
    def _():
        convout_ref[0] = halo_ref[halo - 3:halo, :]
        hout_ref[0] = h_ref[...]


def _lru_seq(xa, act, row0, nseq, seqlen, tr, conv0, h0, lw, emit_y, first_pos_zero):
    w = xa.shape[1]
    nt = seqlen // tr
    rb0 = row0 // tr
    ng = w // LANES
    state_map = (lambda b, t: (b, 0, 0)) if conv0.shape[0] == nseq else (lambda b, t: (0, 0, 0))
    const2 = lambda b, t: (0, 0)
    const3 = lambda b, t: (0, 0, 0)
    out_specs = [pl.BlockSpec((1, CONV_W - 1, w), lambda b, t: (b, 0, 0)),
                 pl.BlockSpec((1, 1, w), lambda b, t: (b, 0, 0))]
    out_shape = [jax.ShapeDtypeStruct((nseq, CONV_W - 1, w), F32), jax.ShapeDtypeStruct((nseq, 1, w), F32)]
    if emit_y:
        out_specs = [pl.BlockSpec((tr, w), lambda b, t: (b * nt + t, 0))] + out_specs
        out_shape = [jax.ShapeDtypeStruct((nseq * seqlen, w), BF16)] + out_shape
    return pl.pallas_call(
        functools.partial(_lru_seq_kernel, tr=tr, emit_y=emit_y, first_pos_zero=first_pos_zero),
        name="lru_seq",
        grid=(nseq, nt),
        in_specs=[
            pl.BlockSpec((tr, w), lambda b, t: (rb0 + b * nt + t, 0)),
            pl.BlockSpec((tr, w), lambda b, t: (rb0 + b * nt + t, _ACT_GELU)),
            pl.BlockSpec((1, CONV_W - 1, w), state_map),
            pl.BlockSpec((1, 1, w), state_map),
            pl.BlockSpec((CONV_W, w), const2),
            pl.BlockSpec((1, w), const2),
            pl.BlockSpec((ng, LANES, 2 * LANES), const3),
            pl.BlockSpec((ng, 1, 2 * LANES), const3),
            pl.BlockSpec((1, w), const2),
        ],
        out_specs=out_specs,
        out_shape=out_shape,
        scratch_shapes=[pltpu.VMEM((8 + tr, w), F32), pltpu.VMEM((1, w), F32),
                        pltpu.VMEM((tr, w), F32), pltpu.VMEM((tr, w), F32)],
        compiler_params=_cparams(2),
    )(xa, act, conv0, h0, lw["conv_w"], lw["conv_b"], lw["wbd"], lw["bg"], lw["lam"])


def _lru_step_kernel(xa_ref, ga_ref, conv0_ref, h0_ref, cw_ref, cb_ref, wbd_ref, bg_ref, lam_ref,
                     ya_ref, convout_ref, hout_ref, hs_ref, *, nseq, seqlen):
    cl = LRU_C * jax.nn.log_sigmoid(lam_ref[...])
    full = [conv0_ref[j] for j in range(CONV_W - 1)]
    full += [xa_ref[pl.ds(t, nseq, stride=seqlen), :] for t in range(seqlen)]
    h = h0_ref[...]
    for t in range(seqlen):
        xc = cb_ref[...] + cw_ref[0:1, :] * full[t]
        for j in range(1, CONV_W):
            xc = xc + cw_ref[j:j + 1, :] * full[t + j]
        a, mult, ing = _lru_coeffs(xc, wbd_ref, bg_ref, cl, 0)
        h = a * h + mult * ing * xc
        hs_ref[pl.ds(t, nseq, stride=seqlen), :] = h
    ya_ref[...] = (hs_ref[...] * ga_ref[...].astype(F32)).astype(BF16)
    for j in range(CONV_W - 1):
        convout_ref[j] = full[seqlen + j]
    hout_ref[...] = h


def _lru_step(xa, act, nseq, seqlen, conv0_t, h0, lw):
    w = xa.shape[1]
    rows = nseq * seqlen
    ng = w // LANES
    col = lambda g: (0, g)
    return pl.pallas_call(
        functools.partial(_lru_step_kernel, nseq=nseq, seqlen=seqlen),
        name="lru_step",
        grid=(ng,),
        in_specs=[
            pl.BlockSpec((rows, LANES), col),
            pl.BlockSpec((rows, LANES), col),
            pl.BlockSpec((CONV_W - 1, nseq, LANES), lambda g: (0, 0, g)),
            pl.BlockSpec((nseq, LANES), col),
            pl.BlockSpec((CONV_W, LANES), col),
            pl.BlockSpec((1, LANES), col),
            pl.BlockSpec((1, LANES, 2 * LANES), lambda g: (g, 0, 0)),
            pl.BlockSpec((1, 1, 2 * LANES), lambda g: (g, 0, 0)),
            pl.BlockSpec((1, LANES), col),
        ],
        out_specs=[
            pl.BlockSpec((rows, LANES), col),
            pl.BlockSpec((CONV_W - 1, nseq, LANES), lambda g: (0, 0, g)),
            pl.BlockSpec((nseq, LANES), col),
        ],
        out_shape=[jax.ShapeDtypeStruct((rows, w), BF16),
                   jax.ShapeDtypeStruct((CONV_W - 1, nseq, w), F32),
                   jax.ShapeDtypeStruct((nseq, w), F32)],
        scratch_shapes=[pltpu.VMEM((rows, LANES), F32)],
        compiler_params=_cparams(1),
    )(xa, act, conv0_t, h0, lw["conv_w"], lw["conv_b"], lw["wbd"], lw["bg"], lw["lam"])


def _log_gamma(h):
    return math.log1p(-(2.0 ** (-5.0 - h)))


def _ret_kernel(q_ref, k_ref, v_ref, sg_ref, s0_ref, ng_ref, *refs, nb, clen, emit_y):
    if emit_y:
        y_ref, sout_ref, s_ref, dec_ref, rdec_ref = refs
    else:
        sout_ref, s_ref, dec_ref, rdec_ref = refs
    c = pl.program_id(1)
    rows = nb * clen
    heads = range(RET_HEADS)

    @pl.when((pl.program_id(0) == 0) & (c == 0))
    def _():
        li = lax.broadcasted_iota(jnp.int32, (rows, rows), 0)
        mi = lax.broadcasted_iota(jnp.int32, (rows, rows), 1)
        keep = (li // clen == mi // clen) & (li >= mi)
        diff = jnp.where(keep, li - mi, 0).astype(F32)
        tpos = (lax.broadcasted_iota(jnp.int32, (rows, LANES), 0) % clen).astype(F32)
        for h in heads:
            lg = _log_gamma(h)
            dec_ref[h] = jnp.where(keep, jnp.exp(lg * diff), 0.0)
            rdec_ref[0, h] = jnp.exp(lg * (tpos + 1.0))
            rdec_ref[1, h] = jnp.exp(lg * (clen - 1.0 - tpos))

    @pl.when(c == 0)
    def _():
        for n in range(nb):
            s_ref[n] = s0_ref[n if s0_ref.shape[0] == nb else 0]

    if nb > 1:
        seq_of_row = lax.broadcasted_iota(jnp.int32, (rows, nb * HEAD_DIM), 0) // clen
        seq_of_col = lax.broadcasted_iota(jnp.int32, (rows, nb * HEAD_DIM), 1) // HEAD_DIM
        own = seq_of_row == seq_of_col
    tn = (((0,), (0,)), ((), ()))
    nt = (((1,), (1,)), ((), ()))
    hsl = [slice(h * HEAD_DIM, (h + 1) * HEAD_DIM) for h in heads]
    q = [q_ref[:, sl] for sl in hsl]
    k = [k_ref[:, sl] for sl in hsl]
    v = [v_ref[:, sl] for sl in hsl]
    s_old = [[s_ref[n, h] for n in range(nb)] for h in heads]

    if emit_y:
        scores = [lax.dot_general(q[h], k[h], nt, preferred_element_type=F32) for h in heads]
        inter = []
        for h in heads:
            s_cat = jnp.concatenate(s_old[h], axis=1) if nb > 1 else s_old[h][0]
            qs = jnp.dot(q[h], s_cat.astype(BF16), preferred_element_type=F32)
            if nb > 1:
                qs = jnp.where(own, qs, 0.0)
                qs = sum(qs[:, n * HEAD_DIM:(n + 1) * HEAD_DIM] for n in range(nb))
            inter.append(qs)
    for h in heads:
        kd = (k[h].astype(F32) * rdec_ref[1, h]).astype(BF16)
        if nb > 1:
            v_bd = jnp.where(own, jnp.concatenate([v[h].astype(F32)] * nb, axis=1), 0.0).astype(BF16)
        else:
            v_bd = v[h]
        upd = lax.dot_general(kd, v_bd, tn, preferred_element_type=F32)
        g_chunk = math.exp(clen * _log_gamma(h))
        for n in range(nb):
            s_ref[n, h] = g_chunk * s_old[h][n] + upd[:, n * HEAD_DIM:(n + 1) * HEAD_DIM]

    if emit_y:
        for h in heads:
            p = (scores[h] * dec_ref[h]).astype(BF16)
            o = jnp.dot(p, v[h], preferred_element_type=F32) + inter[h] * rdec_ref[0, h]
            mu = jnp.mean(o, axis=-1, keepdims=True)
            dev = o - mu
            var = jnp.mean(dev * dev, axis=-1, keepdims=True)
            normed = dev * lax.rsqrt(var + EPS) * ng_ref[:, hsl[h]]
            y_ref[:, hsl[h]] = (sg_ref[:, hsl[h]].astype(F32) * normed).astype(BF16)

    @pl.when(c == pl.num_programs(1) - 1)
    def _():
        sout_ref[...] = s_ref[...]


def _retention(act, row0, nseq, seqlen, nb, clen, s0, norm_g, emit_y):
    w = RET_HEADS * HEAD_DIM
    rows = nb * clen
    nc = seqlen // clen
    assert nb == 1 or nc == 1
    rb0 = row0 // rows
    rmap = lambda col: (lambda b, c: (rb0 + b * nc + c, col))
    shared = s0.shape[0] != nseq
    s_map = (lambda b, c: (0, 0, 0, 0)) if shared else (lambda b, c: (b, 0, 0, 0))
    s_block = (1 if shared else nb, RET_HEADS, HEAD_DIM, HEAD_DIM)
    out_specs = [pl.BlockSpec((nb, RET_HEADS, HEAD_DIM, HEAD_DIM), lambda b, c: (b, 0, 0, 0))]
    out_shape = [jax.ShapeDtypeStruct((nseq, RET_HEADS, HEAD_DIM, HEAD_DIM), F32)]
    if emit_y:
        out_specs = [pl.BlockSpec((rows, w), lambda b, c: (b * nc + c, 0))] + out_specs
        out_shape = [jax.ShapeDtypeStruct((nseq * seqlen, w), BF16)] + out_shape
    return pl.pallas_call(
        functools.partial(_ret_kernel, nb=nb, clen=clen, emit_y=emit_y),
        name="retention",
        grid=(nseq // nb, nc),
        in_specs=[
            pl.BlockSpec((rows, w), rmap(_ACT_Q)),
            pl.BlockSpec((rows, w), rmap(_ACT_K)),
            pl.BlockSpec((rows, w), rmap(_ACT_V)),
            pl.BlockSpec((rows, w), rmap(_ACT_SILU)),
            pl.BlockSpec(s_block, s_map),
            pl.BlockSpec((1, w), lambda b, c: (0, 0)),
        ],
        out_specs=out_specs,
        out_shape=out_shape,
        scratch_shapes=[pltpu.VMEM((nb, RET_HEADS, HEAD_DIM, HEAD_DIM), F32),
                        pltpu.VMEM((RET_HEADS, rows, rows), F32),
                        pltpu.VMEM((2, RET_HEADS, rows, LANES), F32)],
        compiler_params=_cparams(2),
    )(act, act, act, act, s0, norm_g)


def _outproj_kernel(x_ref, ya_ref, yb_ref, sga_ref, sgb_ref, pa_ref, pb_ref, wo_ref, *refs, n_side):
    side_src, o_ref, side_dst = refs[:n_side], refs[n_side], refs[n_side + 1:]
    for src, dst in zip(side_src, side_dst):
        dst[...] = src[...].astype(BF16)
    ma = jnp.dot(ya_ref[...], pa_ref[...], preferred_element_type=F32)
    mb = jnp.dot(yb_ref[...], pb_ref[...], preferred_element_type=F32)
    merged = sga_ref[...].astype(F32) * ma + sgb_ref[...].astype(F32) * mb
    o_ref[...] = x_ref[...] + jnp.dot(merged.astype(BF16), wo_ref[...], preferred_element_type=F32)


def _outproj(x, ya, yb, act, pa, pb, wo, bm, side=()):
    m, d = x.shape
    w = ya.shape[1]
    nd = d // 1024
    row = lambda i: (i, 0)
    const = lambda i: (0, 0)
    single = pl.Buffered(1)
    in_specs = [
        pl.BlockSpec((bm, d), row),
        pl.BlockSpec((bm, w), row),
        pl.BlockSpec((bm, w), row),
        pl.BlockSpec((bm, d), lambda i: (i, _ACT_SGA // nd)),
        pl.BlockSpec((bm, d), lambda i: (i, _ACT_SGB // nd)),
        pl.BlockSpec((w, d), const, pipeline_mode=single),
        pl.BlockSpec((w, d), const, pipeline_mode=single),
        pl.BlockSpec((d, d), const, pipeline_mode=single),
    ]
    out_specs = [pl.BlockSpec((bm, d), row)]
    out_shape = [jax.ShapeDtypeStruct((m, d), F32)]
    for arr, block, index_map in side:
        in_specs.append(pl.BlockSpec(block, index_map))
        out_specs.append(pl.BlockSpec(block, index_map))
        out_shape.append(jax.ShapeDtypeStruct(arr.shape, BF16))
    return pl.pallas_call(
        functools.partial(_outproj_kernel, n_side=len(side)),
        name="outproj",
        grid=(m // bm,),
        in_specs=in_specs,
        out_specs=out_specs,
        out_shape=out_shape,
        compiler_params=_cparams(1),
    )(x, ya, yb, act, act, pa, pb, wo, *[arr for arr, _, _ in side])


def _ffn_kernel(x_ref, g_ref, wu_ref, wd_ref, gf_ref, o_ref, hn_ref):
    j = pl.program_id(1)
    last = pl.num_programs(1) - 1

    def block(first, final):
        for r0, nr in _row_chunks(o_ref.shape[0], 256):
            rows = slice(r0, r0 + nr)
            if first:
                base = x_ref[rows, :]
                hn = _rmsnorm(base, g_ref[...]).astype(BF16)
                hn_ref[rows, :] = hn
            else:
                base = o_ref[rows, :]
                hn = hn_ref[rows, :]
            u = jnp.dot(hn, wu_ref[...], preferred_element_type=F32)
            r = jnp.square(jnp.maximum(u, 0.0)).astype(BF16)
            acc = base + jnp.dot(r, wd_ref[...], preferred_element_type=F32)
            o_ref[rows, :] = _rmsnorm(acc, gf_ref[...]) if final else acc

    pl.when(j == 0)(lambda: block(True, False))
    pl.when((j > 0) & (j < last))(lambda: block(False, False))
    pl.when(j == last)(lambda: block(False, True))


def _ffn(x, g, wu, wd, gf, bm, bf):
    m, d = x.shape
    dff = wu.shape[1]
    assert dff // bf >= 2
    return pl.pallas_call(
        _ffn_kernel,
        name="ffn",
        grid=(m // bm, dff // bf),
        in_specs=[
            pl.BlockSpec((bm, d), lambda i, j: (i, 0)),
            pl.BlockSpec((1, d), lambda i, j: (0, 0)),
            pl.BlockSpec((d, bf), lambda i, j: (0, j)),
            pl.BlockSpec((bf, d), lambda i, j: (j, 0)),
            pl.BlockSpec((1, d), lambda i, j: (0, 0)),
        ],
        out_specs=pl.BlockSpec((bm, d), lambda i, j: (i, 0)),
        out_shape=jax.ShapeDtypeStruct((m, d), F32),
        scratch_shapes=[pltpu.VMEM((bm, d), BF16)],
        compiler_params=_cparams(2),
    )(x, g, wu, wd, gf)


def _rope_tables(pos):
    inv = ROPE_BASE ** (-jnp.arange(0, HEAD_DIM, 2, dtype=F32) / HEAD_DIM)
    ang = pos.astype(F32)[:, None] * inv[None, :]
    cos, sin = jnp.cos(ang), jnp.sin(ang)
    return jnp.concatenate([cos, cos], axis=1), jnp.concatenate([-sin, sin], axis=1)


def _gate_weights(wa, wx, ba, bx):
    nblk, blk, _ = wa.shape
    per = LANES // blk
    ng = nblk // per

    def bd(wt):
        wt = wt.reshape(ng, per, blk, blk)
        eye = jnp.eye(per, dtype=wt.dtype)
        return jnp.einsum("gpcd,pq->gpcqd", wt, eye).reshape(ng, LANES, LANES)

    wbd = jnp.concatenate([bd(wa), bd(wx)], axis=2).astype(BF16)
    bg = jnp.concatenate([ba.reshape(ng, 1, LANES), bx.reshape(ng, 1, LANES)], axis=2)
    return wbd, bg


def kernel(x_prompt, x_sample, state_conv, state_lru, state_ret, meta_tokens, norm_mix_g, w_in, conv_w,
           conv_b, lru_wa, lru_ba, lru_wx, lru_bx, lru_lam, ret_norm_g, p_a, p_b, w_out, norm_ffn_g,
           w_up, w_down, norm_f_g):
    assert w_in.shape[0] == 1
    nb_p, t_p, d = x_prompt.shape
    nb_s, t_s, _ = x_sample.shape
    w = conv_w.shape[-1]
    dff = w_up.shape[-1]
    rows_p, rows_s = nb_p * t_p, nb_s * t_s

    g_mix, g_ffn, g_f = norm_mix_g[0][None], norm_ffn_g[0][None], norm_f_g[None]
    wbd, bg = _gate_weights(lru_wa[0], lru_wx[0], lru_ba[0], lru_bx[0])
    lw = dict(conv_w=conv_w[0], conv_b=conv_b[0][None], wbd=wbd, bg=bg, lam=lru_lam[0][None])
    ret_g = ret_norm_g[0][None]

    cos_p, sin_p = _rope_tables(N_META + jnp.arange(t_p, dtype=jnp.int32))
    pos_sm = jnp.concatenate([jnp.tile(PAST_LEN + jnp.arange(t_s, dtype=jnp.int32), nb_s),
                              jnp.arange(N_META, dtype=jnp.int32)])
    cos_s, sin_s = _rope_tables(pos_sm)
    x_p2 = x_prompt.reshape(rows_p, d)
    x_s2 = x_sample.reshape(rows_s, d)
    xa_s, act_s, w_in_b = _inproj(x_s2, g_mix, w_in[0], cos_s, sin_s, rows_s + N_META, 512, 1,
                                  tail=meta_tokens.astype(x_sample.dtype))
    bm_p = 1024
    n_i, n_j = rows_p // bm_p, 8
    side_map = lambda i, j: (i, jnp.minimum(j, n_j - 1))
    sides = tuple((wt[0], (wt.shape[1] // n_i, wt.shape[2] // n_j), side_map) for wt in (w_up, p_a, p_b, w_out))
    xa_p, act_p, wu_b, pa_b, pb_b, wo_b = _inproj(x_p2, g_mix, w_in_b, cos_p, sin_p, bm_p, 1024, t_p // bm_p,
                                                  side=sides)

    zc = jnp.zeros((1, CONV_W - 1, w), F32)
    zh = jnp.zeros((1, 1, w), F32)
    zs = jnp.zeros((1, RET_HEADS, HEAD_DIM, HEAD_DIM), F32)
    conv_m, h_m = _lru_seq(xa_s, act_s, rows_s, 1, N_META, N_META, zc, zh, lw, emit_y=False, first_pos_zero=True)
    (s_m,) = _retention(act_s, rows_s, 1, N_META, 1, N_META, zs, ret_g, emit_y=False)

    ya_p, conv_p, h_p = _lru_seq(xa_p, act_p, 0, nb_p, t_p, 256, conv_m, h_m, lw, emit_y=True, first_pos_zero=False)
    yb_p, s_p = _retention(act_p, 0, nb_p, t_p, 1, CHUNK, s_m, ret_g, emit_y=True)

    conv0_t = jnp.transpose(state_conv[0], (1, 0, 2))
    ya_s, conv_s_t, h_s = _lru_step(xa_s, act_s, nb_s, t_s, conv0_t, state_lru[0], lw)
    yb_s, s_s = _retention(act_s, 0, nb_s, t_s, 8, t_s, state_ret[0], ret_g, emit_y=True)

    bm_o = 256
    side_down = (w_down[0], (dff // (rows_p // bm_o), d), lambda i: (i, 0))
    x1_p, wd_b = _outproj(x_p2, ya_p, yb_p, act_p, pa_b, pb_b, wo_b, bm_o, side=(side_down,))
    (x1_s,) = _outproj(x_s2, ya_s, yb_s, act_s, pa_b, pb_b, wo_b, bm_o)
    y_p = _ffn(x1_p, g_ffn, wu_b, wd_b, g_f, 1024, 1024)
    y_s = _ffn(x1_s, g_ffn, wu_b, wd_b, g_f, 1024, 1024)

    return (y_p.reshape(nb_p, t_p, d), y_s.reshape(nb_s, t_s, d),
            conv_p[None], h_p.reshape(1, nb_p, w), s_p[None],
            jnp.transpose(conv_s_t, (1, 0, 2))[None], h_s[None], s_s[None])
```

```python
import functools
import math

import jax
import jax.numpy as jnp
from jax import lax
from jax.experimental import pallas as pl
from jax.experimental.pallas import tpu as pltpu

F32 = jnp.float32
BF16 = jnp.bfloat16

N_META = 16
PAST_LEN = 16384
LRU_BLOCKS = 16
CONV_W = 4
LRU_C = 8.0
RET_HEADS = 8
HEAD_DIM = 128
CHUNK = 128
ROPE_BASE = 10000.0
EPS = 1e-6

LANES = 128
VMEM_LIMIT = 56 << 20


def _cparams(n_axes):
    return pltpu.CompilerParams(dimension_semantics=("arbitrary",) * n_axes, vmem_limit_bytes=VMEM_LIMIT)


def _rmsnorm(x, g):
    return x * lax.rsqrt(jnp.mean(x * x, axis=-1, keepdims=True) + EPS) * g


def _sigmoid(x):
    return 1.0 / (1.0 + jnp.exp(-x))


def _row_chunks(rows, size):
    n = max(rows // size, 1)
    return [(c * size, size if c < n - 1 else rows - c * size) for c in range(n)]


_COL_GATE, _COL_Q, _COL_K, _COL_V, _COL_G, _COL_GATE_A = 1, 2, 3, 4, 5, 6
_ACT_GELU, _ACT_Q, _ACT_K, _ACT_V, _ACT_SGA, _ACT_SGB, _ACT_SILU = 0, 1, 2, 3, 4, 6, 8


def _act_block(kind):
    return jnp.where(kind <= _COL_V, kind - 1, jnp.where(kind == _COL_G, _ACT_SILU, kind - 2))


def _inproj_kernel(x_ref, g_ref, w_ref, cos_ref, sin_ref, *refs, nper, cast_w, n_side, has_tail):
    if has_tail:
        tail_ref, refs = refs[0], refs[1:]
    side_src, refs = refs[:n_side], refs[n_side:]
    xa_ref, act_ref = refs[0], refs[1]
    refs = refs[2:]
    if cast_w:
        wb_ref, refs = refs[0], refs[1:]
    side_dst, (xn_ref,) = refs[:n_side], refs[n_side:]
    j = pl.program_id(1)
    kind = j // nper
    bm, bn = xa_ref.shape
    chunks = _row_chunks(bm, 256)

    for src, dst in zip(side_src, side_dst):
        dst[...] = src[...].astype(BF16)
    if cast_w:
        wb_ref[...] = w_ref[...].astype(BF16)
        w_bf = wb_ref
    else:
        w_bf = w_ref

    def project(epilogue, normalize=False):
        for r0, nr in chunks:
            rows = slice(r0, r0 + nr)
            if normalize:
                own = slice(r0, min(r0 + nr, x_ref.shape[0]))
                xn_ref[own, :] = _rmsnorm(x_ref[own, :], g_ref[...]).astype(BF16)
                if own.stop < r0 + nr:
                    xn_ref[own.stop:r0 + nr, :] = _rmsnorm(tail_ref[...], g_ref[...]).astype(BF16)
            epilogue(jnp.dot(xn_ref[rows, :], w_bf[...], preferred_element_type=F32), rows)

    def to_xa(acc, rows):
        xa_ref[rows, :] = acc

    def rope(acc, rows):
        scale = jnp.where(kind == _COL_K, HEAD_DIM ** -0.5, 1.0).astype(F32)
        cos, sin = cos_ref[rows, :], sin_ref[rows, :]
        for h in range(bn // HEAD_DIM):
            a = acc[:, h * HEAD_DIM:(h + 1) * HEAD_DIM]
            r = (a * cos + pltpu.roll(a, HEAD_DIM // 2, axis=1) * sin) * scale
            act_ref[rows, h * HEAD_DIM:(h + 1) * HEAD_DIM] = r.astype(BF16)

    def gelu(acc, rows):
        act_ref[rows, :] = jax.nn.gelu(acc).astype(BF16)

    def gated(acc, rows):
        sg = _sigmoid(acc)
        out = jnp.where(kind == _COL_V, acc, jnp.where(kind == _COL_G, acc * sg, sg))
        act_ref[rows, :] = out.astype(BF16)

    pl.when(j == 0)(lambda: project(to_xa, normalize=True))
    if nper > 1:
        pl.when((j > 0) & (kind == 0))(lambda: project(to_xa))
    pl.when(kind == _COL_GATE)(lambda: project(gelu))
    pl.when((kind == _COL_Q) | (kind == _COL_K))(lambda: project(rope))
    pl.when(kind >= _COL_V)(lambda: project(gated))


def _inproj(x, g, w, cos, sin, bm, bn, table_blocks, side=(), tail=None):
    xrows, d = x.shape
    m = xrows if tail is None else xrows + tail.shape[0]
    assert tail is None or m == bm
    n = w.shape[1]
    width = 1024
    nper = width // bn
    cast_w = w.dtype != BF16

    def act_map(i, j):
        kind = j // nper
        return i, jnp.where(kind == 0, 0, _act_block(kind) * nper + j % nper)

    in_specs = [
        pl.BlockSpec((min(bm, xrows), d), lambda i, j: (i, 0)),
        pl.BlockSpec((1, d), lambda i, j: (0, 0)),
        pl.BlockSpec((d, bn), lambda i, j: (0, j)),
        pl.BlockSpec((bm, LANES), lambda i, j: (i % table_blocks, 0)),
        pl.BlockSpec((bm, LANES), lambda i, j: (i % table_blocks, 0)),
    ]
    if tail is not None:
        in_specs.append(pl.BlockSpec(tail.shape, lambda i, j: (0, 0)))
    out_specs = [
        pl.BlockSpec((bm, bn), lambda i, j: (i, jnp.minimum(j, nper - 1))),
        pl.BlockSpec((bm, bn), act_map),
    ]
    out_shape = [jax.ShapeDtypeStruct((m, width), F32), jax.ShapeDtypeStruct((m, n - width), BF16)]
    if cast_w:
        out_specs.append(pl.BlockSpec((d, bn), lambda i, j: (0, j)))
        out_shape.append(jax.ShapeDtypeStruct(w.shape, BF16))
    for arr, block, index_map in side:
        in_specs.append(pl.BlockSpec(block, index_map))
        out_specs.append(pl.BlockSpec(block, index_map))
        out_shape.append(jax.ShapeDtypeStruct(arr.shape, BF16))
    return pl.pallas_call(
        functools.partial(_inproj_kernel, nper=nper, cast_w=cast_w, n_side=len(side), has_tail=tail is not None),
        name="inproj",
        grid=(m // bm, n // bn),
        in_specs=in_specs,
        out_specs=out_specs,
        out_shape=out_shape,
        scratch_shapes=[pltpu.VMEM((bm, d), BF16)],
        compiler_params=_cparams(2),
    )(x, g, w, cos, sin, *([] if tail is None else [tail]), *[arr for arr, _, _ in side])


def _lru_coeffs(xc, wbd_ref, bg_ref, cl_row, group):
    gates = jnp.dot(xc.astype(BF16), wbd_ref[group], preferred_element_type=F32) + bg_ref[group]
    rec = _sigmoid(gates[:, :LANES])
    ing = _sigmoid(gates[:, LANES:])
    log_a = rec * cl_row
    a = jnp.exp(log_a)
    mult = jnp.sqrt(jnp.tanh(-log_a) * (a * a + 1.0))
    return a, mult, ing


def _lru_seq_kernel(xa_ref, ga_ref, conv0_ref, h0_ref, cw_ref, cb_ref, wbd_ref, bg_ref, lam_ref,
                    *refs, tr, emit_y, first_pos_zero):
    if emit_y:
        ya_ref, convout_ref, hout_ref, halo_ref, h_ref, a_ref, b_ref = refs
    else:
        convout_ref, hout_ref, halo_ref, h_ref, a_ref, b_ref = refs
    t = pl.program_id(1)
    halo = 8

    @pl.when(t == 0)
    def _():
        halo_ref[halo - 3:halo, :] = conv0_ref[0]
        h_ref[...] = h0_ref[0]

    halo_ref[halo:halo + tr, :] = xa_ref[...]
    xc = cb_ref[...] + cw_ref[0:1, :] * halo_ref[halo - 3:halo - 3 + tr, :]
    for j in range(1, CONV_W):
        xc = xc + cw_ref[j:j + 1, :] * halo_ref[halo - 3 + j:halo - 3 + j + tr, :]
    halo_ref[halo - 3:halo, :] = halo_ref[halo + tr - 3:halo + tr, :]

    cl = LRU_C * jax.nn.log_sigmoid(lam_ref[...])
    for g in range(xc.shape[1] // LANES):
        sl = slice(g * LANES, (g + 1) * LANES)
        a, mult, ing = _lru_coeffs(xc[:, sl], wbd_ref, bg_ref, cl[:, sl], g)
        if first_pos_zero:
            row = lax.broadcasted_iota(jnp.int32, a.shape, 0) + t * tr
            mult = jnp.where(row == 0, 1.0, mult)
        a_ref[:, sl] = a
        b_ref[:, sl] = mult * ing * xc[:, sl]

    def step(r, h):
        h = a_ref[pl.ds(r, 1), :] * h + b_ref[pl.ds(r, 1), :]
        halo_ref[pl.ds(halo + r, 1), :] = h
        return h

    h_ref[...] = lax.fori_loop(0, tr, step, h_ref[...], unroll=8)
    if emit_y:
        ya_ref[...] = (halo_ref[halo:halo + tr, :] * ga_ref[...].astype(F32)).astype(BF16)

    @pl.when(t == pl.num_programs(1) - 1)
    def _():
        convout_ref[0] = halo_ref[halo - 3:halo, :]
        hout_ref[0] = h_ref[...]


def _lru_seq(xa, act, row0, nseq, seqlen, tr, conv0, h0, lw, emit_y, first_pos_zero):
    w = xa.shape[1]
    nt = seqlen // tr
    rb0 = row0 // tr
    ng = w // LANES
    state_map = (lambda b, t: (b, 0, 0)) if conv0.shape[0] == nseq else (lambda b, t: (0, 0, 0))
    const2 = lambda b, t: (0, 0)
    const3 = lambda b, t: (0, 0, 0)
    out_specs = [pl.BlockSpec((1, CONV_W - 1, w), lambda b, t: (b, 0, 0)),
                 pl.BlockSpec((1, 1, w), lambda b, t: (b, 0, 0))]
    out_shape = [jax.ShapeDtypeStruct((nseq, CONV_W - 1, w), F32), jax.ShapeDtypeStruct((nseq, 1, w), F32)]
    if emit_y:
        out_specs = [pl.BlockSpec((tr, w), lambda b, t: (b * nt + t, 0))] + out_specs
        out_shape = [jax.ShapeDtypeStruct((nseq * seqlen, w), BF16)] + out_shape
    return pl.pallas_call(
        functools.partial(_lru_seq_kernel, tr=tr, emit_y=emit_y, first_pos_zero=first_pos_zero),
        name="lru_seq",
        grid=(nseq, nt),
        in_specs=[
            pl.BlockSpec((tr, w), lambda b, t: (rb0 + b * nt + t, 0)),
            pl.BlockSpec((tr, w), lambda b, t: (rb0 + b * nt + t, _ACT_GELU)),
            pl.BlockSpec((1, CONV_W - 1, w), state_map),
            pl.BlockSpec((1, 1, w), state_map),
            pl.BlockSpec((CONV_W, w), const2),
            pl.BlockSpec((1, w), const2),
            pl.BlockSpec((ng, LANES, 2 * LANES), const3),
            pl.BlockSpec((ng, 1, 2 * LANES), const3),
            pl.BlockSpec((1, w), const2),
        ],
        out_specs=out_specs,
        out_shape=out_shape,
        scratch_shapes=[pltpu.VMEM((8 + tr, w), F32), pltpu.VMEM((1, w), F32),
                        pltpu.VMEM((tr, w), F32), pltpu.VMEM((tr, w), F32)],
        compiler_params=_cparams(2),
    )(xa, act, conv0, h0, lw["conv_w"], lw["conv_b"], lw["wbd"], lw["bg"], lw["lam"])


def _lru_step_kernel(xa_ref, ga_ref, conv0_ref, h0_ref, cw_ref, cb_ref, wbd_ref, bg_ref, lam_ref,
                     ya_ref, convout_ref, hout_ref, hs_ref, *, nseq, seqlen):
    cl = LRU_C * jax.nn.log_sigmoid(lam_ref[...])
    full = [conv0_ref[j] for j in range(CONV_W - 1)]
    full += [xa_ref[pl.ds(t, nseq, stride=seqlen), :] for t in range(seqlen)]
    h = h0_ref[...]
    for t in range(seqlen):
        xc = cb_ref[...] + cw_ref[0:1, :] * full[t]
        for j in range(1, CONV_W):
            xc = xc + cw_ref[j:j + 1, :] * full[t + j]
        a, mult, ing = _lru_coeffs(xc, wbd_ref, bg_ref, cl, 0)
        h = a * h + mult * ing * xc
        hs_ref[pl.ds(t, nseq, stride=seqlen), :] = h
    ya_ref[...] = (hs_ref[...] * ga_ref[...].astype(F32)).astype(BF16)
    for j in range(CONV_W - 1):
        convout_ref[j] = full[seqlen + j]
    hout_ref[...] = h


def _lru_step(xa, act, nseq, seqlen, conv0_t, h0, lw):
    w = xa.shape[1]
    rows = nseq * seqlen
    ng = w // LANES
    col = lambda g: (0, g)
    return pl.pallas_call(
        functools.partial(_lru_step_kernel, nseq=nseq, seqlen=seqlen),
        name="lru_step",
        grid=(ng,),
        in_specs=[
            pl.BlockSpec((rows, LANES), col),
            pl.BlockSpec((rows, LANES), col),
            pl.BlockSpec((CONV_W - 1, nseq, LANES), lambda g: (0, 0, g)),
            pl.BlockSpec((nseq, LANES), col),
            pl.BlockSpec((CONV_W, LANES), col),
            pl.BlockSpec((1, LANES), col),
            pl.BlockSpec((1, LANES, 2 * LANES), lambda g: (g, 0, 0)),
            pl.BlockSpec((1, 1, 2 * LANES), lambda g: (g, 0, 0)),
            pl.BlockSpec((1, LANES), col),
        ],
        out_specs=[
            pl.BlockSpec((rows, LANES), col),
            pl.BlockSpec((CONV_W - 1, nseq, LANES), lambda g: (0, 0, g)),
            pl.BlockSpec((nseq, LANES), col),
        ],
        out_shape=[jax.ShapeDtypeStruct((rows, w), BF16),
                   jax.ShapeDtypeStruct((CONV_W - 1, nseq, w), F32),
                   jax.ShapeDtypeStruct((nseq, w), F32)],
        scratch_shapes=[pltpu.VMEM((rows, LANES), F32)],
        compiler_params=_cparams(1),
    )(xa, act, conv0_t, h0, lw["conv_w"], lw["conv_b"], lw["wbd"], lw["bg"], lw["lam"])


def _log_gamma(h):
    return math.log1p(-(2.0 ** (-5.0 - h)))


def _ret_kernel(q_ref, k_ref, v_ref, sg_ref, s0_ref, ng_ref, *refs, nb, clen, emit_y):
    if emit_y:
        y_ref, sout_ref, s_ref, dec_ref, rdec_ref = refs
    else:
        sout_ref, s_ref, dec_ref, rdec_ref = refs
    c = pl.program_id(1)
    rows = nb * clen
    heads = range(RET_HEADS)

    @pl.when((pl.program_id(0) == 0) & (c == 0))
    def _():
        li = lax.broadcasted_iota(jnp.int32, (rows, rows), 0)
        mi = lax.broadcasted_iota(jnp.int32, (rows, rows), 1)
        keep = (li // clen == mi // clen) & (li >= mi)
        diff = jnp.where(keep, li - mi, 0).astype(F32)
        tpos = (lax.broadcasted_iota(jnp.int32, (rows, LANES), 0) % clen).astype(F32)
        for h in heads:
            lg = _log_gamma(h)
            dec_ref[h] = jnp.where(keep, jnp.exp(lg * diff), 0.0)
            rdec_ref[0, h] = jnp.exp(lg * (tpos + 1.0))
            rdec_ref[1, h] = jnp.exp(lg * (clen - 1.0 - tpos))

    @pl.when(c == 0)
    def _():
        for n in range(nb):
            s_ref[n] = s0_ref[n if s0_ref.shape[0] == nb else 0]

    if nb > 1:
        seq_of_row = lax.broadcasted_iota(jnp.int32, (rows, nb * HEAD_DIM), 0) // clen
        seq_of_col = lax.broadcasted_iota(jnp.int32, (rows, nb * HEAD_DIM), 1) // HEAD_DIM
        own = seq_of_row == seq_of_col
    tn = (((0,), (0,)), ((), ()))
    nt = (((1,), (1,)), ((), ()))
    hsl = [slice(h * HEAD_DIM, (h + 1) * HEAD_DIM) for h in heads]
    q = [q_ref[:, sl] for sl in hsl]
    k = [k_ref[:, sl] for sl in hsl]
    v = [v_ref[:, sl] for sl in hsl]
    s_old = [[s_ref[n, h] for n in range(nb)] for h in heads]

    if emit_y:
        scores = [lax.dot_general(q[h], k[h], nt, preferred_element_type=F32) for h in heads]
        inter = []
        for h in heads:
            s_cat = jnp.concatenate(s_old[h], axis=1) if nb > 1 else s_old[h][0]
            qs = jnp.dot(q[h], s_cat.astype(BF16), preferred_element_type=F32)
            if nb > 1:
                qs = jnp.where(own, qs, 0.0)
                qs = sum(qs[:, n * HEAD_DIM:(n + 1) * HEAD_DIM] for n in range(nb))
            inter.append(qs)
    for h in heads:
        kd = (k[h].astype(F32) * rdec_ref[1, h]).astype(BF16)
        if nb > 1:
            v_bd = jnp.where(own, jnp.concatenate([v[h].astype(F32)] * nb, axis=1), 0.0).astype(BF16)
        else:
            v_bd = v[h]
        upd = lax.dot_general(kd, v_bd, tn, preferred_element_type=F32)
        g_chunk = math.exp(clen * _log_gamma(h))
        for n in range(nb):
            s_ref[n, h] = g_chunk * s_old[h][n] + upd[:, n * HEAD_DIM:(n + 1) * HEAD_DIM]

    if emit_y:
        for h in heads:
            p = (scores[h] * dec_ref[h]).astype(BF16)
            o = jnp.dot(p, v[h], preferred_element_type=F32) + inter[h] * rdec_ref[0, h]
            mu = jnp.mean(o, axis=-1, keepdims=True)
            dev = o - mu
            var = jnp.mean(dev * dev, axis=-1, keepdims=True)
            normed = dev * lax.rsqrt(var + EPS) * ng_ref[:, hsl[h]]
            y_ref[:, hsl[h]] = (sg_ref[:, hsl[h]].astype(F32) * normed).astype(BF16)

    @pl.when(c == pl.num_programs(1) - 1)
    def _():
        sout_ref[...] = s_ref[...]


def _retention(act, row0, nseq, seqlen, nb, clen, s0, norm_g, emit_y):
    w = RET_HEADS * HEAD_DIM
    rows = nb * clen
    nc = seqlen // clen
    assert nb == 1 or nc == 1
    rb0 = row0 // rows
    rmap = lambda col: (lambda b, c: (rb0 + b * nc + c, col))
    shared = s0.shape[0] != nseq
    s_map = (lambda b, c: (0, 0, 0, 0)) if shared else (lambda b, c: (b, 0, 0, 0))
    s_block = (1 if shared else nb, RET_HEADS, HEAD_DIM, HEAD_DIM)
    out_specs = [pl.BlockSpec((nb, RET_HEADS, HEAD_DIM, HEAD_DIM), lambda b, c: (b, 0, 0, 0))]
    out_shape = [jax.ShapeDtypeStruct((nseq, RET_HEADS, HEAD_DIM, HEAD_DIM), F32)]
    if emit_y:
        out_specs = [pl.BlockSpec((rows, w), lambda b, c: (b * nc + c, 0))] + out_specs
        out_shape = [jax.ShapeDtypeStruct((nseq * seqlen, w), BF16)] + out_shape
    return pl.pallas_call(
        functools.partial(_ret_kernel, nb=nb, clen=clen, emit_y=emit_y),
        name="retention",
        grid=(nseq // nb, nc),
        in_specs=[
            pl.BlockSpec((rows, w), rmap(_ACT_Q)),
            pl.BlockSpec((rows, w), rmap(_ACT_K)),
            pl.BlockSpec((rows, w), rmap(_ACT_V)),
            pl.BlockSpec((rows, w), rmap(_ACT_SILU)),
            pl.BlockSpec(s_block, s_map),
            pl.BlockSpec((1, w), lambda b, c: (0, 0)),
        ],
        out_specs=out_specs,
        out_shape=out_shape,
        scratch_shapes=[pltpu.VMEM((nb, RET_HEADS, HEAD_DIM, HEAD_DIM), F32),
                        pltpu.VMEM((RET_HEADS, rows, rows), F32),
                        pltpu.VMEM((2, RET_HEADS, rows, LANES), F32)],
        compiler_params=_cparams(2),
    )(act, act, act, act, s0, norm_g)


def _outproj_kernel(x_ref, ya_ref, yb_ref, sga_ref, sgb_ref, pa_ref, pb_ref, wo_ref, *refs, n_side):
    side_src, o_ref, side_dst = refs[:n_side], refs[n_side], refs[n_side + 1:]
    for src, dst in zip(side_src, side_dst):
        dst[...] = src[...].astype(BF16)
    ma = jnp.dot(ya_ref[...], pa_ref[...], preferred_element_type=F32)
    mb = jnp.dot(yb_ref[...], pb_ref[...], preferred_element_type=F32)
    merged = sga_ref[...].astype(F32) * ma + sgb_ref[...].astype(F32) * mb
    o_ref[...] = x_ref[...] + jnp.dot(merged.astype(BF16), wo_ref[...], preferred_element_type=F32)


def _outproj(x, ya, yb, act, pa, pb, wo, bm, side=()):
    m, d = x.shape
    w = ya.shape[1]
    nd = d // 1024
    row = lambda i: (i, 0)
    const = lambda i: (0, 0)
    single = pl.Buffered(1)
    in_specs = [
        pl.BlockSpec((bm, d), row),
        pl.BlockSpec((bm, w), row),
        pl.BlockSpec((bm, w), row),
        pl.BlockSpec((bm, d), lambda i: (i, _ACT_SGA // nd)),
        pl.BlockSpec((bm, d), lambda i: (i, _ACT_SGB // nd)),
        pl.BlockSpec((w, d), const, pipeline_mode=single),
        pl.BlockSpec((w, d), const, pipeline_mode=single),
        pl.BlockSpec((d, d), const, pipeline_mode=single),
    ]
    out_specs = [pl.BlockSpec((bm, d), row)]
    out_shape = [jax.ShapeDtypeStruct((m, d), F32)]
    for arr, block, index_map in side:
        in_specs.append(pl.BlockSpec(block, index_map))
        out_specs.append(pl.BlockSpec(block, index_map))
        out_shape.append(jax.ShapeDtypeStruct(arr.shape, BF16))
    return pl.pallas_call(
        functools.partial(_outproj_kernel, n_side=len(side)),
        name="outproj",
        grid=(m // bm,),
        in_specs=in_specs,
        out_specs=out_specs,
        out_shape=out_shape,
        compiler_params=_cparams(1),
    )(x, ya, yb, act, act, pa, pb, wo, *[arr for arr, _, _ in side])


def _ffn_kernel(x_ref, g_ref, wu_ref, wd_ref, gf_ref, o_ref, hn_ref):
    j = pl.program_id(1)
    last = pl.num_programs(1) - 1

    def block(first, final):
        for r0, nr in _row_chunks(o_ref.shape[0], 256):
            rows = slice(r0, r0 + nr)
            if first:
                base = x_ref[rows, :]
                hn = _rmsnorm(base, g_ref[...]).astype(BF16)
                hn_ref[rows, :] = hn
            else:
                base = o_ref[rows, :]
                hn = hn_ref[rows, :]
            u = jnp.dot(hn, wu_ref[...], preferred_element_type=F32)
            r = jnp.square(jnp.maximum(u, 0.0)).astype(BF16)
            acc = base + jnp.dot(r, wd_ref[...], preferred_element_type=F32)
            o_ref[rows, :] = _rmsnorm(acc, gf_ref[...]) if final else acc

    pl.when(j == 0)(lambda: block(True, False))
    pl.when((j > 0) & (j < last))(lambda: block(False, False))
    pl.when(j == last)(lambda: block(False, True))


def _ffn(x, g, wu, wd, gf, bm, bf):
    m, d = x.shape
    dff = wu.shape[1]
    assert dff // bf >= 2
    return pl.pallas_call(
        _ffn_kernel,
        name="ffn",
        grid=(m // bm, dff // bf),
        in_specs=[
            pl.BlockSpec((bm, d), lambda i, j: (i, 0)),
            pl.BlockSpec((1, d), lambda i, j: (0, 0)),
            pl.BlockSpec((d, bf), lambda i, j: (0, j)),
            pl.BlockSpec((bf, d), lambda i, j: (j, 0)),
            pl.BlockSpec((1, d), lambda i, j: (0, 0)),
        ],
        out_specs=pl.BlockSpec((bm, d), lambda i, j: (i, 0)),
        out_shape=jax.ShapeDtypeStruct((m, d), F32),
        scratch_shapes=[pltpu.VMEM((bm, d), BF16)],
        compiler_params=_cparams(2),
    )(x, g, wu, wd, gf)


def _rope_tables(pos):
    inv = ROPE_BASE ** (-jnp.arange(0, HEAD_DIM, 2, dtype=F32) / HEAD_DIM)
    ang = pos.astype(F32)[:, None] * inv[None, :]
    cos, sin = jnp.cos(ang), jnp.sin(ang)
    return jnp.concatenate([cos, cos], axis=1), jnp.concatenate([-sin, sin], axis=1)


def _gate_weights(wa, wx, ba, bx):
    nblk, blk, _ = wa.shape
    per = LANES // blk
    ng = nblk // per

    def bd(wt):
        wt = wt.reshape(ng, per, blk, blk)
        eye = jnp.eye(per, dtype=wt.dtype)
        return jnp.einsum("gpcd,pq->gpcqd", wt, eye).reshape(ng, LANES, LANES)

    wbd = jnp.concatenate([bd(wa), bd(wx)], axis=2).astype(BF16)
    bg = jnp.concatenate([ba.reshape(ng, 1, LANES), bx.reshape(ng, 1, LANES)], axis=2)
    return wbd, bg


def kernel(x_prompt, x_sample, state_conv, state_lru, state_ret, meta_tokens, norm_mix_g, w_in, conv_w,
           conv_b, lru_wa, lru_ba, lru_wx, lru_bx, lru_lam, ret_norm_g, p_a, p_b, w_out, norm_ffn_g,
           w_up, w_down, norm_f_g):
    assert w_in.shape[0] == 1
    nb_p, t_p, d = x_prompt.shape
    nb_s, t_s, _ = x_sample.shape
    w = conv_w.shape[-1]
    dff = w_up.shape[-1]
    rows_p, rows_s = nb_p * t_p, nb_s * t_s

    g_mix, g_ffn, g_f = norm_mix_g[0][None], norm_ffn_g[0][None], norm_f_g[None]
    wbd, bg = _gate_weights(lru_wa[0], lru_wx[0], lru_ba[0], lru_bx[0])
    lw = dict(conv_w=conv_w[0], conv_b=conv_b[0][None], wbd=wbd, bg=bg, lam=lru_lam[0][None])
    ret_g = ret_norm_g[0][None]

    cos_p, sin_p = _rope_tables(N_META + jnp.arange(t_p, dtype=jnp.int32))
    pos_sm = jnp.concatenate([jnp.tile(PAST_LEN + jnp.arange(t_s, dtype=jnp.int32), nb_s),
                              jnp.arange(N_META, dtype=jnp.int32)])
    cos_s, sin_s = _rope_tables(pos_sm)
    x_p2 = x_prompt.reshape(rows_p, d)
    x_s2 = x_sample.reshape(rows_s, d)
    xa_s, act_s, w_in_b = _inproj(x_s2, g_mix, w_in[0], cos_s, sin_s, rows_s + N_META, 512, 1,
                                  tail=meta_tokens.astype(x_sample.dtype))
    bm_p = 1024
    n_i, n_j = rows_p // bm_p, 8
    side_map = lambda i, j: (i, jnp.minimum(j, n_j - 1))
    sides = tuple((wt[0], (wt.shape[1] // n_i, wt.shape[2] // n_j), side_map) for wt in (w_up, p_a, p_b, w_out))
    xa_p, act_p, wu_b, pa_b, pb_b, wo_b = _inproj(x_p2, g_mix, w_in_b, cos_p, sin_p, bm_p, 1024, t_p // bm_p,
                                                  side=sides)

    zc = jnp.zeros((1, CONV_W - 1, w), F32)
    zh = jnp.zeros((1, 1, w), F32)
    zs = jnp.zeros((1, RET_HEADS, HEAD_DIM, HEAD_DIM), F32)
    conv_m, h_m = _lru_seq(xa_s, act_s, rows_s, 1, N_META, N_META, zc, zh, lw, emit_y=False, first_pos_zero=True)
    (s_m,) = _retention(act_s, rows_s, 1, N_META, 1, N_META, zs, ret_g, emit_y=False)

    ya_p, conv_p, h_p = _lru_seq(xa_p, act_p, 0, nb_p, t_p, 256, conv_m, h_m, lw, emit_y=True, first_pos_zero=False)
    yb_p, s_p = _retention(act_p, 0, nb_p, t_p, 1, CHUNK, s_m, ret_g, emit_y=True)

    conv0_t = jnp.transpose(state_conv[0], (1, 0, 2))
    ya_s, conv_s_t, h_s = _lru_step(xa_s, act_s, nb_s, t_s, conv0_t, state_lru[0], lw)
    yb_s, s_s = _retention(act_s, 0, nb_s, t_s, 8, t_s, state_ret[0], ret_g, emit_y=True)

    bm_o = 256
    side_down = (w_down[0], (dff // (rows_p // bm_o), d), lambda i: (i, 0))
    x1_p, wd_b = _outproj(x_p2, ya_p, yb_p, act_p, pa_b, pb_b, wo_b, bm_o, side=(side_down,))
    (x1_s,) = _outproj(x_s2, ya_s, yb_s, act_s, pa_b, pb_b, wo_b, bm_o)
    y_p = _ffn(x1_p, g_ffn, wu_b, wd_b, g_f, 1024, 1024)
    y_s = _ffn(x1_s, g_ffn, wu_b, wd_b, g_f, 1024, 1024)

    return (y_p.reshape(nb_p, t_p, d), y_s.reshape(nb_s, t_s, d),
            conv_p[None], h_p.reshape(1, nb_p, w), s_p[None],
            jnp.transpose(conv_s_t, (1, 0, 2))[None], h_s[None], s_s[None])
```

```python
import functools
import math

import jax
import jax.numpy as jnp
from jax import lax
from jax.experimental import pallas as pl
from jax.experimental.pallas import tpu as pltpu

F32 = jnp.float32
BF16 = jnp.bfloat16

N_META = 16
PAST_LEN = 16384
LRU_BLOCKS = 16
CONV_W = 4
LRU_C = 8.0
RET_HEADS = 8
HEAD_DIM = 128
CHUNK = 128
ROPE_BASE = 10000.0
EPS = 1e-6

LANES = 128
VMEM_LIMIT = 56 << 20


def _cparams(n_axes):
    return pltpu.CompilerParams(dimension_semantics=("arbitrary",) * n_axes, vmem_limit_bytes=VMEM_LIMIT)


def _rmsnorm(x, g):
    return x * lax.rsqrt(jnp.mean(x * x, axis=-1, keepdims=True) + EPS) * g


def _sigmoid(x):
    return 0.5 * jnp.tanh(0.5 * x) + 0.5


def _row_chunks(rows, size):
    n = max(rows // size, 1)
    return [(c * size, size if c < n - 1 else rows - c * size) for c in range(n)]


_COL_GATE, _COL_Q, _COL_K, _COL_V, _COL_G, _COL_GATE_A = 1, 2, 3, 4, 5, 6
_ACT_GELU, _ACT_Q, _ACT_K, _ACT_V, _ACT_SGA, _ACT_SGB, _ACT_SILU = 0, 1, 2, 3, 4, 6, 8


def _act_block(kind):
    return jnp.where(kind <= _COL_V, kind - 1, jnp.where(kind == _COL_G, _ACT_SILU, kind - 2))


def _inproj_kernel(x_ref, g_ref, w_ref, cos_ref, sin_ref, *refs, nper, cast_w, n_side, has_tail):
    if has_tail:
        tail_ref, refs = refs[0], refs[1:]
    side_src, refs = refs[:n_side], refs[n_side:]
    xa_ref, act_ref = refs[0], refs[1]
    refs = refs[2:]
    if cast_w:
        wb_ref, refs = refs[0], refs[1:]
    side_dst, (xn_ref,) = refs[:n_side], refs[n_side:]
    j = pl.program_id(1)
    kind = j // nper
    bn = xa_ref.shape[1]

    for src, dst in zip(side_src, side_dst):
        dst[...] = src[...].astype(BF16)
    if cast_w:
        wb_ref[...] = w_ref[...].astype(BF16)
        w_bf = wb_ref
    else:
        w_bf = w_ref

    def project(epilogue, normalize=False):
        if normalize:
            nx = x_ref.shape[0]
            xn_ref[0:nx, :] = _rmsnorm(x_ref[...], g_ref[...]).astype(BF16)
            if has_tail:
                xn_ref[nx:, :] = _rmsnorm(tail_ref[...], g_ref[...]).astype(BF16)
        epilogue(jnp.dot(xn_ref[...], w_bf[...], preferred_element_type=F32))

    def to_xa(acc):
        xa_ref[...] = acc

    def rope(scale):
        def epilogue(acc):
            cos, sin = cos_ref[...], sin_ref[...]
            for h in range(bn // HEAD_DIM):
                a = acc[:, h * HEAD_DIM:(h + 1) * HEAD_DIM]
                r = a * cos + pltpu.roll(a, HEAD_DIM // 2, axis=1) * sin
                if scale is not None:
                    r = r * scale
                act_ref[:, h * HEAD_DIM:(h + 1) * HEAD_DIM] = r.astype(BF16)
        return epilogue

    def elementwise(fn):
        def epilogue(acc):
            act_ref[...] = fn(acc).astype(BF16)
        return epilogue

    pl.when(j == 0)(lambda: project(to_xa, normalize=True))
    if nper > 1:
        pl.when((j > 0) & (kind == 0))(lambda: project(to_xa))
    pl.when(kind == _COL_GATE)(lambda: project(elementwise(jax.nn.gelu)))
    pl.when(kind == _COL_Q)(lambda: project(rope(None)))
    pl.when(kind == _COL_K)(lambda: project(rope(HEAD_DIM ** -0.5)))
    pl.when(kind == _COL_V)(lambda: project(elementwise(lambda a: a)))
    pl.when(kind == _COL_G)(lambda: project(elementwise(lambda a: a * _sigmoid(a))))
    pl.when(kind >= _COL_GATE_A)(lambda: project(elementwise(_sigmoid)))


def _inproj(x, g, w, cos, sin, bm, bn, table_blocks, side=(), tail=None):
    xrows, d = x.shape
    m = xrows if tail is None else xrows + tail.shape[0]
    assert tail is None or m == bm
    n = w.shape[1]
    width = 1024
    nper = width // bn
    cast_w = w.dtype != BF16

    def act_map(i, j):
        kind = j // nper
        return i, jnp.where(kind == 0, 0, _act_block(kind) * nper + j % nper)

    in_specs = [
        pl.BlockSpec((min(bm, xrows), d), lambda i, j: (i, 0)),
        pl.BlockSpec((1, d), lambda i, j: (0, 0)),
        pl.BlockSpec((d, bn), lambda i, j: (0, j)),
        pl.BlockSpec((bm, LANES), lambda i, j: (i % table_blocks, 0)),
        pl.BlockSpec((bm, LANES), lambda i, j: (i % table_blocks, 0)),
    ]
    if tail is not None:
        in_specs.append(pl.BlockSpec(tail.shape, lambda i, j: (0, 0)))
    out_specs = [
        pl.BlockSpec((bm, bn), lambda i, j: (i, jnp.minimum(j, nper - 1))),
        pl.BlockSpec((bm, bn), act_map),
    ]
    out_shape = [jax.ShapeDtypeStruct((m, width), F32), jax.ShapeDtypeStruct((m, n - width), BF16)]
    if cast_w:
        out_specs.append(pl.BlockSpec((d, bn), lambda i, j: (0, j)))
        out_shape.append(jax.ShapeDtypeStruct(w.shape, BF16))
    for arr, block, index_map in side:
        in_specs.append(pl.BlockSpec(block, index_map))
        out_specs.append(pl.BlockSpec(block, index_map))
        out_shape.append(jax.ShapeDtypeStruct(arr.shape, BF16))
    return pl.pallas_call(
        functools.partial(_inproj_kernel, nper=nper, cast_w=cast_w, n_side=len(side), has_tail=tail is not None),
        name="inproj",
        grid=(m // bm, n // bn),
        in_specs=in_specs,
        out_specs=out_specs,
        out_shape=out_shape,
        scratch_shapes=[pltpu.VMEM((bm, d), BF16)],
        compiler_params=_cparams(2),
    )(x, g, w, cos, sin, *([] if tail is None else [tail]), *[arr for arr, _, _ in side])


def _lru_coeffs(xc, wbd_ref, bg_ref, cl_row, group):
    gates = jnp.dot(xc.astype(BF16), wbd_ref[group], preferred_element_type=F32) + bg_ref[group]
    rec = _sigmoid(gates[:, :LANES])
    ing = _sigmoid(gates[:, LANES:])
    log_a = rec * cl_row
    a = jnp.exp(log_a)
    mult = jnp.sqrt(jnp.tanh(-log_a) * (a * a + 1.0))
    return a, mult, ing


def _lru_seq_kernel(xa_ref, ga_ref, conv0_ref, h0_ref, cw_ref, cb_ref, wbd_ref, bg_ref, lam_ref,
                    *refs, tr, emit_y, first_pos_zero):
    if emit_y:
        ya_ref, convout_ref, hout_ref, halo_ref, h_ref, a_ref, b_ref = refs
    else:
        convout_ref, hout_ref, halo_ref, h_ref, a_ref, b_ref = refs
    t = pl.program_id(1)
    halo = 8

    @pl.when(t == 0)
    def _():
        halo_ref[halo - 3:halo, :] = conv0_ref[0]
        h_ref[...] = h0_ref[0]

    halo_ref[halo:halo + tr, :] = xa_ref[...]
    xc = cb_ref[...] + cw_ref[0:1, :] * halo_ref[halo - 3:halo - 3 + tr, :]
    for j in range(1, CONV_W):
        xc = xc + cw_ref[j:j + 1, :] * halo_ref[halo - 3 + j:halo - 3 + j + tr, :]
    halo_ref[halo - 3:halo, :] = halo_ref[halo + tr - 3:halo + tr, :]

    cl = LRU_C * jax.nn.log_sigmoid(lam_ref[...])
    for g in range(xc.shape[1] // LANES):
        sl = slice(g * LANES, (g + 1) * LANES)
        a, mult, ing = _lru_coeffs(xc[:, sl], wbd_ref, bg_ref, cl[:, sl], g)
        if first_pos_zero:
            row = lax.broadcasted_iota(jnp.int32, a.shape, 0) + t * tr
            mult = jnp.where(row == 0, 1.0, mult)
        a_ref[:, sl] = a
        b_ref[:, sl] = mult * ing * xc[:, sl]

    def step(r, h):
        h = a_ref[pl.ds(r, 1), :] * h + b_ref[pl.ds(r, 1), :]
        halo_ref[pl.ds(halo + r, 1), :] = h
        return h

    h_ref[...] = lax.fori_loop(0, tr, step, h_ref[...], unroll=8)
    if emit_y:
        ya_ref[...] = (halo_ref[halo:halo + tr, :] * ga_ref[...].astype(F32)).astype(BF16)

    @pl.when(t == pl.num_programs(1) - 1)
    def _():
        convout_ref[0] = halo_ref[halo - 3:halo, :]
        hout_ref[0] = h_ref[...]


def _lru_seq(xa, act, row0, nseq, seqlen, tr, conv0, h0, lw, emit_y, first_pos_zero):
    w = xa.shape[1]
    nt = seqlen // tr
    rb0 = row0 // tr
    ng = w // LANES
    state_map = (lambda b, t: (b, 0, 0)) if conv0.shape[0] == nseq else (lambda b, t: (0, 0, 0))
    const2 = lambda b, t: (0, 0)
    const3 = lambda b, t: (0, 0, 0)
    out_specs = [pl.BlockSpec((1, CONV_W - 1, w), lambda b, t: (b, 0, 0)),
                 pl.BlockSpec((1, 1, w), lambda b, t: (b, 0, 0))]
    out_shape = [jax.ShapeDtypeStruct((nseq, CONV_W - 1, w), F32), jax.ShapeDtypeStruct((nseq, 1, w), F32)]
    if emit_y:
        out_specs = [pl.BlockSpec((tr, w), lambda b, t: (b * nt + t, 0))] + out_specs
        out_shape = [jax.ShapeDtypeStruct((nseq * seqlen, w), BF16)] + out_shape
    return pl.pallas_call(
        functools.partial(_lru_seq_kernel, tr=tr, emit_y=emit_y, first_pos_zero=first_pos_zero),
        name="lru_seq",
        grid=(nseq, nt),
        in_specs=[
            pl.BlockSpec((tr, w), lambda b, t: (rb0 + b * nt + t, 0)),
            pl.BlockSpec((tr, w), lambda b, t: (rb0 + b * nt + t, _ACT_GELU)),
            pl.BlockSpec((1, CONV_W - 1, w), state_map),
            pl.BlockSpec((1, 1, w), state_map),
            pl.BlockSpec((CONV_W, w), const2),
            pl.BlockSpec((1, w), const2),
            pl.BlockSpec((ng, LANES, 2 * LANES), const3),
            pl.BlockSpec((ng, 1, 2 * LANES), const3),
            pl.BlockSpec((1, w), const2),
        ],
        out_specs=out_specs,
        out_shape=out_shape,
        scratch_shapes=[pltpu.VMEM((8 + tr, w), F32), pltpu.VMEM((1, w), F32),
                        pltpu.VMEM((tr, w), F32), pltpu.VMEM((tr, w), F32)],
        compiler_params=_cparams(2),
    )(xa, act, conv0, h0, lw["conv_w"], lw["conv_b"], lw["wbd"], lw["bg"], lw["lam"])


def _lru_step_kernel(xa_ref, ga_ref, conv0_ref, h0_ref, cw_ref, cb_ref, wbd_ref, bg_ref, lam_ref,
                     ya_ref, convout_ref, hout_ref, hs_ref, *, nseq, seqlen):
    cl = LRU_C * jax.nn.log_sigmoid(lam_ref[...])
    full = [conv0_ref[j] for j in range(CONV_W - 1)]
    full += [xa_ref[pl.ds(t, nseq, stride=seqlen), :] for t in range(seqlen)]
    h = h0_ref[...]
    for t in range(seqlen):
        xc = cb_ref[...] + cw_ref[0:1, :] * full[t]
        for j in range(1, CONV_W):
            xc = xc + cw_ref[j:j + 1, :] * full[t + j]
        a, mult, ing = _lru_coeffs(xc, wbd_ref, bg_ref, cl, 0)
        h = a * h + mult * ing * xc
        hs_ref[pl.ds(t, nseq, stride=seqlen), :] = h
    ya_ref[...] = (hs_ref[...] * ga_ref[...].astype(F32)).astype(BF16)
    for j in range(CONV_W - 1):
        convout_ref[j] = full[seqlen + j]
    hout_ref[...] = h


def _lru_step(xa, act, nseq, seqlen, conv0_t, h0, lw):
    w = xa.shape[1]
    rows = nseq * seqlen
    ng = w // LANES
    col = lambda g: (0, g)
    return pl.pallas_call(
        functools.partial(_lru_step_kernel, nseq=nseq, seqlen=seqlen),
        name="lru_step",
        grid=(ng,),
        in_specs=[
            pl.BlockSpec((rows, LANES), col),
            pl.BlockSpec((rows, LANES), col),
            pl.BlockSpec((CONV_W - 1, nseq, LANES), lambda g: (0, 0, g)),
            pl.BlockSpec((nseq, LANES), col),
            pl.BlockSpec((CONV_W, LANES), col),
            pl.BlockSpec((1, LANES), col),
            pl.BlockSpec((1, LANES, 2 * LANES), lambda g: (g, 0, 0)),
            pl.BlockSpec((1, 1, 2 * LANES), lambda g: (g, 0, 0)),
            pl.BlockSpec((1, LANES), col),
        ],
        out_specs=[
            pl.BlockSpec((rows, LANES), col),
            pl.BlockSpec((CONV_W - 1, nseq, LANES), lambda g: (0, 0, g)),
            pl.BlockSpec((nseq, LANES), col),
        ],
        out_shape=[jax.ShapeDtypeStruct((rows, w), BF16),
                   jax.ShapeDtypeStruct((CONV_W - 1, nseq, w), F32),
                   jax.ShapeDtypeStruct((nseq, w), F32)],
        scratch_shapes=[pltpu.VMEM((rows, LANES), F32)],
        compiler_params=_cparams(1),
    )(xa, act, conv0_t, h0, lw["conv_w"], lw["conv_b"], lw["wbd"], lw["bg"], lw["lam"])


def _log_gamma(h):
    return math.log1p(-(2.0 ** (-5.0 - h)))


def _ret_kernel(q_ref, k_ref, v_ref, sg_ref, s0_ref, ng_ref, *refs, nb, clen, emit_y):
    if emit_y:
        y_ref, sout_ref, s_ref, dec_ref, rdec_ref = refs
    else:
        sout_ref, s_ref, dec_ref, rdec_ref = refs
    c = pl.program_id(1)
    rows = nb * clen
    heads = range(RET_HEADS)

    @pl.when((pl.program_id(0) == 0) & (c == 0))
    def _():
        li = lax.broadcasted_iota(jnp.int32, (rows, rows), 0)
        mi = lax.broadcasted_iota(jnp.int32, (rows, rows), 1)
        keep = (li // clen == mi // clen) & (li >= mi)
        diff = jnp.where(keep, li - mi, 0).astype(F32)
        tpos = (lax.broadcasted_iota(jnp.int32, (rows, LANES), 0) % clen).astype(F32)
        for h in heads:
            lg = _log_gamma(h)
            dec_ref[h] = jnp.where(keep, jnp.exp(lg * diff), 0.0)
            rdec_ref[0, h] = jnp.exp(lg * (tpos + 1.0))
            rdec_ref[1, h] = jnp.exp(lg * (clen - 1.0 - tpos))

    @pl.when(c == 0)
    def _():
        for n in range(nb):
            s_ref[n] = s0_ref[n if s0_ref.shape[0] == nb else 0]

    if nb > 1:
        seq_of_row = lax.broadcasted_iota(jnp.int32, (rows, nb * HEAD_DIM), 0) // clen
        seq_of_col = lax.broadcasted_iota(jnp.int32, (rows, nb * HEAD_DIM), 1) // HEAD_DIM
        own = seq_of_row == seq_of_col
    tn = (((0,), (0,)), ((), ()))
    nt = (((1,), (1,)), ((), ()))
    hsl = [slice(h * HEAD_DIM, (h + 1) * HEAD_DIM) for h in heads]
    q = [q_ref[:, sl] for sl in hsl]
    k = [k_ref[:, sl] for sl in hsl]
    v = [v_ref[:, sl] for sl in hsl]
    s_old = [[s_ref[n, h] for n in range(nb)] for h in heads]

    if emit_y:
        scores = [lax.dot_general(q[h], k[h], nt, preferred_element_type=F32) for h in heads]
        inter = []
        for h in heads:
            s_cat = jnp.concatenate(s_old[h], axis=1) if nb > 1 else s_old[h][0]
            qs = jnp.dot(q[h], s_cat.astype(BF16), preferred_element_type=F32)
            if nb > 1:
                qs = jnp.where(own, qs, 0.0)
                qs = sum(qs[:, n * HEAD_DIM:(n + 1) * HEAD_DIM] for n in range(nb))
            inter.append(qs)
    for h in heads:
        kd = (k[h].astype(F32) * rdec_ref[1, h]).astype(BF16)
        if nb > 1:
            v_bd = jnp.where(own, jnp.concatenate([v[h].astype(F32)] * nb, axis=1), 0.0).astype(BF16)
        else:
            v_bd = v[h]
        upd = lax.dot_general(kd, v_bd, tn, preferred_element_type=F32)
        g_chunk = math.exp(clen * _log_gamma(h))
        for n in range(nb):
            s_ref[n, h] = g_chunk * s_old[h][n] + upd[:, n * HEAD_DIM:(n + 1) * HEAD_DIM]

    if emit_y:
        for h in heads:
            p = (scores[h] * dec_ref[h]).astype(BF16)
            o = jnp.dot(p, v[h], preferred_element_type=F32) + inter[h] * rdec_ref[0, h]
            mu = jnp.mean(o, axis=-1, keepdims=True)
            dev = o - mu
            var = jnp.mean(dev * dev, axis=-1, keepdims=True)
            normed = dev * lax.rsqrt(var + EPS) * ng_ref[:, hsl[h]]
            y_ref[:, hsl[h]] = (sg_ref[:, hsl[h]].astype(F32) * normed).astype(BF16)

    @pl.when(c == pl.num_programs(1) - 1)
    def _():
        sout_ref[...] = s_ref[...]


def _retention(act, row0, nseq, seqlen, nb, clen, s0, norm_g, emit_y):
    w = RET_HEADS * HEAD_DIM
    rows = nb * clen
    nc = seqlen // clen
    assert nb == 1 or nc == 1
    rb0 = row0 // rows
    rmap = lambda col: (lambda b, c: (rb0 + b * nc + c, col))
    shared = s0.shape[0] != nseq
    s_map = (lambda b, c: (0, 0, 0, 0)) if shared else (lambda b, c: (b, 0, 0, 0))
    s_block = (1 if shared else nb, RET_HEADS, HEAD_DIM, HEAD_DIM)
    out_specs = [pl.BlockSpec((nb, RET_HEADS, HEAD_DIM, HEAD_DIM), lambda b, c: (b, 0, 0, 0))]
    out_shape = [jax.ShapeDtypeStruct((nseq, RET_HEADS, HEAD_DIM, HEAD_DIM), F32)]
    if emit_y:
        out_specs = [pl.BlockSpec((rows, w), lambda b, c: (b * nc + c, 0))] + out_specs
        out_shape = [jax.ShapeDtypeStruct((nseq * seqlen, w), BF16)] + out_shape
    return pl.pallas_call(
        functools.partial(_ret_kernel, nb=nb, clen=clen, emit_y=emit_y),
        name="retention",
        grid=(nseq // nb, nc),
        in_specs=[
            pl.BlockSpec((rows, w), rmap(_ACT_Q)),
            pl.BlockSpec((rows, w), rmap(_ACT_K)),
            pl.BlockSpec((rows, w), rmap(_ACT_V)),
            pl.BlockSpec((rows, w), rmap(_ACT_SILU)),
            pl.BlockSpec(s_block, s_map),
            pl.BlockSpec((1, w), lambda b, c: (0, 0)),
        ],
        out_specs=out_specs,
        out_shape=out_shape,
        scratch_shapes=[pltpu.VMEM((nb, RET_HEADS, HEAD_DIM, HEAD_DIM), F32),
                        pltpu.VMEM((RET_HEADS, rows, rows), F32),
                        pltpu.VMEM((2, RET_HEADS, rows, LANES), F32)],
        compiler_params=_cparams(2),
    )(act, act, act, act, s0, norm_g)


def _outproj_kernel(x_ref, ya_ref, yb_ref, sga_ref, sgb_ref, pa_ref, pb_ref, wo_ref, *refs, n_side):
    side_src, o_ref, side_dst = refs[:n_side], refs[n_side], refs[n_side + 1:]
    for src, dst in zip(side_src, side_dst):
        dst[...] = src[...].astype(BF16)
    ma = jnp.dot(ya_ref[...], pa_ref[...], preferred_element_type=F32)
    mb = jnp.dot(yb_ref[...], pb_ref[...], preferred_element_type=F32)
    merged = sga_ref[...].astype(F32) * ma + sgb_ref[...].astype(F32) * mb
    o_ref[...] = x_ref[...] + jnp.dot(merged.astype(BF16), wo_ref[...], preferred_element_type=F32)


def _outproj(x, ya, yb, act, pa, pb, wo, bm, side=()):
    m, d = x.shape
    w = ya.shape[1]
    nd = d // 1024
    row = lambda i: (i, 0)
    const = lambda i: (0, 0)
    single = pl.Buffered(1)
    in_specs = [
        pl.BlockSpec((bm, d), row),
        pl.BlockSpec((bm, w), row),
        pl.BlockSpec((bm, w), row),
        pl.BlockSpec((bm, d), lambda i: (i, _ACT_SGA // nd)),
        pl.BlockSpec((bm, d), lambda i: (i, _ACT_SGB // nd)),
        pl.BlockSpec((w, d), const, pipeline_mode=single),
        pl.BlockSpec((w, d), const, pipeline_mode=single),
        pl.BlockSpec((d, d), const, pipeline_mode=single),
    ]
    out_specs = [pl.BlockSpec((bm, d), row)]
    out_shape = [jax.ShapeDtypeStruct((m, d), F32)]
    for arr, block, index_map in side:
        in_specs.append(pl.BlockSpec(block, index_map))
        out_specs.append(pl.BlockSpec(block, index_map))
        out_shape.append(jax.ShapeDtypeStruct(arr.shape, BF16))
    return pl.pallas_call(
        functools.partial(_outproj_kernel, n_side=len(side)),
        name="outproj",
        grid=(m // bm,),
        in_specs=in_specs,
        out_specs=out_specs,
        out_shape=out_shape,
        compiler_params=_cparams(1),
    )(x, ya, yb, act, act, pa, pb, wo, *[arr for arr, _, _ in side])


def _ffn_kernel(x_ref, g_ref, wu_ref, wd_ref, gf_ref, o_ref, hn_ref):
    j = pl.program_id(1)
    last = pl.num_programs(1) - 1

    def block(first, final):
        for r0, nr in _row_chunks(o_ref.shape[0], 512):
            rows = slice(r0, r0 + nr)
            if first:
                base = x_ref[rows, :]
                hn = _rmsnorm(base, g_ref[...]).astype(BF16)
                hn_ref[rows, :] = hn
            else:
                base = o_ref[rows, :]
                hn = hn_ref[rows, :]
            u = jnp.dot(hn, wu_ref[...], preferred_element_type=F32)
            r = jnp.square(jnp.maximum(u, 0.0)).astype(BF16)
            acc = base + jnp.dot(r, wd_ref[...], preferred_element_type=F32)
            o_ref[rows, :] = _rmsnorm(acc, gf_ref[...]) if final else acc

    pl.when(j == 0)(lambda: block(True, False))
    pl.when((j > 0) & (j < last))(lambda: block(False, False))
    pl.when(j == last)(lambda: block(False, True))


def _ffn(x, g, wu, wd, gf, bm, bf):
    m, d = x.shape
    dff = wu.shape[1]
    assert dff // bf >= 2
    return pl.pallas_call(
        _ffn_kernel,
        name="ffn",
        grid=(m // bm, dff // bf),
        in_specs=[
            pl.BlockSpec((bm, d), lambda i, j: (i, 0)),
            pl.BlockSpec((1, d), lambda i, j: (0, 0)),
            pl.BlockSpec((d, bf), lambda i, j: (0, j)),
            pl.BlockSpec((bf, d), lambda i, j: (j, 0)),
            pl.BlockSpec((1, d), lambda i, j: (0, 0)),
        ],
        out_specs=pl.BlockSpec((bm, d), lambda i, j: (i, 0)),
        out_shape=jax.ShapeDtypeStruct((m, d), F32),
        scratch_shapes=[pltpu.VMEM((bm, d), BF16)],
        compiler_params=_cparams(2),
    )(x, g, wu, wd, gf)


def _rope_tables(pos):
    inv = ROPE_BASE ** (-jnp.arange(0, HEAD_DIM, 2, dtype=F32) / HEAD_DIM)
    ang = pos.astype(F32)[:, None] * inv[None, :]
    cos, sin = jnp.cos(ang), jnp.sin(ang)
    return jnp.concatenate([cos, cos], axis=1), jnp.concatenate([-sin, sin], axis=1)


def _gate_weights(wa, wx, ba, bx):
    nblk, blk, _ = wa.shape
    per = LANES // blk
    ng = nblk // per

    def bd(wt):
        wt = wt.reshape(ng, per, blk, blk)
        eye = jnp.eye(per, dtype=wt.dtype)
        return jnp.einsum("gpcd,pq->gpcqd", wt, eye).reshape(ng, LANES, LANES)

    wbd = jnp.concatenate([bd(wa), bd(wx)], axis=2).astype(BF16)
    bg = jnp.concatenate([ba.reshape(ng, 1, LANES), bx.reshape(ng, 1, LANES)], axis=2)
    return wbd, bg


def kernel(x_prompt, x_sample, state_conv, state_lru, state_ret, meta_tokens, norm_mix_g, w_in, conv_w,
           conv_b, lru_wa, lru_ba, lru_wx, lru_bx, lru_lam, ret_norm_g, p_a, p_b, w_out, norm_ffn_g,
           w_up, w_down, norm_f_g):
    assert w_in.shape[0] == 1
    nb_p, t_p, d = x_prompt.shape
    nb_s, t_s, _ = x_sample.shape
    w = conv_w.shape[-1]
    dff = w_up.shape[-1]
    rows_p, rows_s = nb_p * t_p, nb_s * t_s

    g_mix, g_ffn, g_f = norm_mix_g[0][None], norm_ffn_g[0][None], norm_f_g[None]
    wbd, bg = _gate_weights(lru_wa[0], lru_wx[0], lru_ba[0], lru_bx[0])
    lw = dict(conv_w=conv_w[0], conv_b=conv_b[0][None], wbd=wbd, bg=bg, lam=lru_lam[0][None])
    ret_g = ret_norm_g[0][None]

    cos_p, sin_p = _rope_tables(N_META + jnp.arange(t_p, dtype=jnp.int32))
    pos_sm = jnp.concatenate([jnp.tile(PAST_LEN + jnp.arange(t_s, dtype=jnp.int32), nb_s),
                              jnp.arange(N_META, dtype=jnp.int32)])
    cos_s, sin_s = _rope_tables(pos_sm)
    x_p2 = x_prompt.reshape(rows_p, d)
    x_s2 = x_sample.reshape(rows_s, d)
    xa_s, act_s, w_in_b = _inproj(x_s2, g_mix, w_in[0], cos_s, sin_s, rows_s + N_META, 512, 1,
                                  tail=meta_tokens.astype(x_sample.dtype))
    bm_p = 1024
    n_i, n_j = rows_p // bm_p, 8
    side_map = lambda i, j: (i, jnp.minimum(j, n_j - 1))
    sides = tuple((wt[0], (wt.shape[1] // n_i, wt.shape[2] // n_j), side_map) for wt in (w_up, p_a, p_b, w_out))
    xa_p, act_p, wu_b, pa_b, pb_b, wo_b = _inproj(x_p2, g_mix, w_in_b, cos_p, sin_p, bm_p, 1024, t_p // bm_p,
                                                  side=sides)


    zc = jnp.zeros((1, CONV_W - 1, w), F32)
    zh = jnp.zeros((1, 1, w), F32)
    zs = jnp.zeros((1, RET_HEADS, HEAD_DIM, HEAD_DIM), F32)
    conv_m, h_m = _lru_seq(xa_s, act_s, rows_s, 1, N_META, N_META, zc, zh, lw, emit_y=False, first_pos_zero=True)
    (s_m,) = _retention(act_s, rows_s, 1, N_META, 1, N_META, zs, ret_g, emit_y=False)

    ya_p, conv_p, h_p = _lru_seq(xa_p, act_p, 0, nb_p, t_p, 256, conv_m, h_m, lw, emit_y=True, first_pos_zero=False)
    yb_p, s_p = _retention(act_p, 0, nb_p, t_p, 1, CHUNK, s_m, ret_g, emit_y=True)

    conv0_t = jnp.transpose(state_conv[0], (1, 0, 2))
    ya_s, conv_s_t, h_s = _lru_step(xa_s, act_s, nb_s, t_s, conv0_t, state_lru[0], lw)
    yb_s, s_s = _retention(act_s, 0, nb_s, t_s, 8, t_s, state_ret[0], ret_g, emit_y=True)

    bm_o = 256
    side_down = (w_down[0], (dff // (rows_p // bm_o), d), lambda i: (i, 0))
    x1_p, wd_b = _outproj(x_p2, ya_p, yb_p, act_p, pa_b, pb_b, wo_b, bm_o, side=(side_down,))
    (x1_s,) = _outproj(x_s2, ya_s, yb_s, act_s, pa_b, pb_b, wo_b, bm_o)
    y_p = _ffn(x1_p, g_ffn, wu_b, wd_b, g_f, 1024, 1024)
    y_s = _ffn(x1_s, g_ffn, wu_b, wd_b, g_f, 1024, 1024)

    return (y_p.reshape(nb_p, t_p, d), y_s.reshape(nb_s, t_s, d),
            conv_p[None], h_p.reshape(1, nb_p, w), s_p[None],
            jnp.transpose(conv_s_t, (1, 0, 2))[None], h_s[None], s_s[None])
```

```python
import functools
import math

import jax
import jax.numpy as jnp
from jax import lax
from jax.experimental import pallas as pl
from jax.experimental.pallas import tpu as pltpu

F32 = jnp.float32
BF16 = jnp.bfloat16

N_META = 16
PAST_LEN = 16384
LRU_BLOCKS = 16
CONV_W = 4
LRU_C = 8.0
RET_HEADS = 8
HEAD_DIM = 128
CHUNK = 128
ROPE_BASE = 10000.0
EPS = 1e-6

LANES = 128
VMEM_LIMIT = 56 << 20


def _cparams(n_axes):
    return pltpu.CompilerParams(dimension_semantics=("arbitrary",) * n_axes, vmem_limit_bytes=VMEM_LIMIT)


def _rmsnorm(x, g):
    return x * lax.rsqrt(jnp.mean(x * x, axis=-1, keepdims=True) + EPS) * g


def _sigmoid(x):
    return 0.5 * jnp.tanh(0.5 * x) + 0.5


def _row_chunks(rows, size):
    n = max(rows // size, 1)
    return [(c * size, size if c < n - 1 else rows - c * size) for c in range(n)]


_COL_GATE, _COL_Q, _COL_K, _COL_V, _COL_G, _COL_GATE_A = 1, 2, 3, 4, 5, 6
_ACT_GELU, _ACT_Q, _ACT_K, _ACT_V, _ACT_SGA, _ACT_SGB, _ACT_SILU = 0, 1, 2, 3, 4, 6, 8


def _act_block(kind):
    return jnp.where(kind <= _COL_V, kind - 1, jnp.where(kind == _COL_G, _ACT_SILU, kind - 2))


def _inproj_kernel(x_ref, g_ref, w_ref, cos_ref, sin_ref, *refs, nper, cast_w, n_side, has_tail):
    if has_tail:
        tail_ref, refs = refs[0], refs[1:]
    side_src, refs = refs[:n_side], refs[n_side:]
    xa_ref, act_ref = refs[0], refs[1]
    refs = refs[2:]
    if cast_w:
        wb_ref, refs = refs[0], refs[1:]
    side_dst, (xn_ref,) = refs[:n_side], refs[n_side:]
    j = pl.program_id(1)
    kind = j // nper
    bn = xa_ref.shape[1]

    for src, dst in zip(side_src, side_dst):
        dst[...] = src[...].astype(BF16)
    if cast_w:
        wb_ref[...] = w_ref[...].astype(BF16)
        w_bf = wb_ref
    else:
        w_bf = w_ref

    def project(epilogue, normalize=False):
        if normalize:
            nx = x_ref.shape[0]
            xn_ref[0:nx, :] = _rmsnorm(x_ref[...], g_ref[...]).astype(BF16)
            if has_tail:
                xn_ref[nx:, :] = _rmsnorm(tail_ref[...], g_ref[...]).astype(BF16)
        epilogue(jnp.dot(xn_ref[...], w_bf[...], preferred_element_type=F32))

    def to_xa(acc):
        xa_ref[...] = acc

    def rope(scale):
        def epilogue(acc):
            cos, sin = cos_ref[...], sin_ref[...]
            for h in range(bn // HEAD_DIM):
                a = acc[:, h * HEAD_DIM:(h + 1) * HEAD_DIM]
                r = a * cos + pltpu.roll(a, HEAD_DIM // 2, axis=1) * sin
                if scale is not None:
                    r = r * scale
                act_ref[:, h * HEAD_DIM:(h + 1) * HEAD_DIM] = r.astype(BF16)
        return epilogue

    def elementwise(fn):
        def epilogue(acc):
            act_ref[...] = fn(acc).astype(BF16)
        return epilogue

    pl.when(j == 0)(lambda: project(to_xa, normalize=True))
    if nper > 1:
        pl.when((j > 0) & (kind == 0))(lambda: project(to_xa))
    pl.when(kind == _COL_GATE)(lambda: project(elementwise(jax.nn.gelu)))
    pl.when(kind == _COL_Q)(lambda: project(rope(None)))
    pl.when(kind == _COL_K)(lambda: project(rope(HEAD_DIM ** -0.5)))
    pl.when(kind == _COL_V)(lambda: project(elementwise(lambda a: a)))
    pl.when(kind == _COL_G)(lambda: project(elementwise(lambda a: a * _sigmoid(a))))
    pl.when(kind >= _COL_GATE_A)(lambda: project(elementwise(_sigmoid)))


def _inproj(x, g, w, cos, sin, bm, bn, table_blocks, side=(), tail=None):
    xrows, d = x.shape
    m = xrows if tail is None else xrows + tail.shape[0]
    assert tail is None or m == bm
    n = w.shape[1]
    width = 1024
    nper = width // bn
    cast_w = w.dtype != BF16

    def act_map(i, j):
        kind = j // nper
        return i, jnp.where(kind == 0, 0, _act_block(kind) * nper + j % nper)

    in_specs = [
        pl.BlockSpec((min(bm, xrows), d), lambda i, j: (i, 0)),
        pl.BlockSpec((1, d), lambda i, j: (0, 0)),
        pl.BlockSpec((d, bn), lambda i, j: (0, j)),
        pl.BlockSpec((bm, LANES), lambda i, j: (i % table_blocks, 0)),
        pl.BlockSpec((bm, LANES), lambda i, j: (i % table_blocks, 0)),
    ]
    if tail is not None:
        in_specs.append(pl.BlockSpec(tail.shape, lambda i, j: (0, 0)))
    out_specs = [
        pl.BlockSpec((bm, bn), lambda i, j: (i, jnp.minimum(j, nper - 1))),
        pl.BlockSpec((bm, bn), act_map),
    ]
    out_shape = [jax.ShapeDtypeStruct((m, width), F32), jax.ShapeDtypeStruct((m, n - width), BF16)]
    if cast_w:
        out_specs.append(pl.BlockSpec((d, bn), lambda i, j: (0, j)))
        out_shape.append(jax.ShapeDtypeStruct(w.shape, BF16))
    for arr, block, index_map in side:
        in_specs.append(pl.BlockSpec(block, index_map))
        out_specs.append(pl.BlockSpec(block, index_map))
        out_shape.append(jax.ShapeDtypeStruct(arr.shape, BF16))
    return pl.pallas_call(
        functools.partial(_inproj_kernel, nper=nper, cast_w=cast_w, n_side=len(side), has_tail=tail is not None),
        name="inproj",
        grid=(m // bm, n // bn),
        in_specs=in_specs,
        out_specs=out_specs,
        out_shape=out_shape,
        scratch_shapes=[pltpu.VMEM((bm, d), BF16)],
        compiler_params=_cparams(2),
    )(x, g, w, cos, sin, *([] if tail is None else [tail]), *[arr for arr, _, _ in side])


def _lru_coeffs(xc, wbd_ref, bg_ref, cl_row, group):
    gates = jnp.dot(xc.astype(BF16), wbd_ref[group], preferred_element_type=F32) + bg_ref[group]
    rec = _sigmoid(gates[:, :LANES])
    ing = _sigmoid(gates[:, LANES:])
    log_a = rec * cl_row
    a = jnp.exp(log_a)
    mult = jnp.sqrt(jnp.tanh(-log_a) * (a * a + 1.0))
    return a, mult, ing


def _lru_seq_kernel(xa_ref, ga_ref, conv0_ref, h0_ref, cw_ref, cb_ref, wbd_ref, bg_ref, lam_ref,
                    *refs, tr, emit_y, first_pos_zero):
    if emit_y:
        ya_ref, convout_ref, hout_ref, halo_ref, h_ref, a_ref, b_ref = refs
    else:
        convout_ref, hout_ref, halo_ref, h_ref, a_ref, b_ref = refs
    t = pl.program_id(1)
    halo = 8

    @pl.when(t == 0)
    def _():
        halo_ref[halo - 3:halo, :] = conv0_ref[0]
        h_ref[...] = h0_ref[0]

    halo_ref[halo:halo + tr, :] = xa_ref[...]
    xc = cb_ref[...] + cw_ref[0:1, :] * halo_ref[halo - 3:halo - 3 + tr, :]
    for j in range(1, CONV_W):
        xc = xc + cw_ref[j:j + 1, :] * halo_ref[halo - 3 + j:halo - 3 + j + tr, :]
    halo_ref[halo - 3:halo, :] = halo_ref[halo + tr - 3:halo + tr, :]

    cl = LRU_C * jax.nn.log_sigmoid(lam_ref[...])
    for g in range(xc.shape[1] // LANES):
        sl = slice(g * LANES, (g + 1) * LANES)
        a, mult, ing = _lru_coeffs(xc[:, sl], wbd_ref, bg_ref, cl[:, sl], g)
        if first_pos_zero:
            row = lax.broadcasted_iota(jnp.int32, a.shape, 0) + t * tr
            mult = jnp.where(row == 0, 1.0, mult)
        a_ref[:, sl] = a
        b_ref[:, sl] = mult * ing * xc[:, sl]

    h = h_ref[...]
    for r in range(tr):
        h = a_ref[r:r + 1, :] * h + b_ref[r:r + 1, :]
        halo_ref[halo + r:halo + r + 1, :] = h
    h_ref[...] = h
    if emit_y:
        ya_ref[...] = (halo_ref[halo:halo + tr, :] * ga_ref[...].astype(F32)).astype(BF16)

    @pl.when(t == pl.num_programs(1) - 1)
    def _():
        convout_ref[0] = halo_ref[halo - 3:halo, :]
        hout_ref[0] = h_ref[...]


def _lru_seq(xa, act, row0, nseq, seqlen, tr, conv0, h0, lw, emit_y, first_pos_zero):
    w = xa.shape[1]
    nt = seqlen // tr
    rb0 = row0 // tr
    ng = w // LANES
    state_map = (lambda b, t: (b, 0, 0)) if conv0.shape[0] == nseq else (lambda b, t: (0, 0, 0))
    const2 = lambda b, t: (0, 0)
    const3 = lambda b, t: (0, 0, 0)
    out_specs = [pl.BlockSpec((1, CONV_W - 1, w), lambda b, t: (b, 0, 0)),
                 pl.BlockSpec((1, 1, w), lambda b, t: (b, 0, 0))]
    out_shape = [jax.ShapeDtypeStruct((nseq, CONV_W - 1, w), F32), jax.ShapeDtypeStruct((nseq, 1, w), F32)]
    if emit_y:
        out_specs = [pl.BlockSpec((tr, w), lambda b, t: (b * nt + t, 0))] + out_specs
        out_shape = [jax.ShapeDtypeStruct((nseq * seqlen, w), BF16)] + out_shape
    return pl.pallas_call(
        functools.partial(_lru_seq_kernel, tr=tr, emit_y=emit_y, first_pos_zero=first_pos_zero),
        name="lru_seq",
        grid=(nseq, nt),
        in_specs=[
            pl.BlockSpec((tr, w), lambda b, t: (rb0 + b * nt + t, 0)),
            pl.BlockSpec((tr, w), lambda b, t: (rb0 + b * nt + t, _ACT_GELU)),
            pl.BlockSpec((1, CONV_W - 1, w), state_map),
            pl.BlockSpec((1, 1, w), state_map),
            pl.BlockSpec((CONV_W, w), const2),
            pl.BlockSpec((1, w), const2),
            pl.BlockSpec((ng, LANES, 2 * LANES), const3),
            pl.BlockSpec((ng, 1, 2 * LANES), const3),
            pl.BlockSpec((1, w), const2),
        ],
        out_specs=out_specs,
        out_shape=out_shape,
        scratch_shapes=[pltpu.VMEM((8 + tr, w), F32), pltpu.VMEM((1, w), F32),
                        pltpu.VMEM((tr, w), F32), pltpu.VMEM((tr, w), F32)],
        compiler_params=_cparams(2),
    )(xa, act, conv0, h0, lw["conv_w"], lw["conv_b"], lw["wbd"], lw["bg"], lw["lam"])


def _lru_step_kernel(xa_ref, ga_ref, conv0_ref, h0_ref, cw_ref, cb_ref, wbd_ref, bg_ref, lam_ref,
                     ya_ref, convout_ref, hout_ref, hs_ref, *, nseq, seqlen):
    cl = LRU_C * jax.nn.log_sigmoid(lam_ref[...])
    full = [conv0_ref[j] for j in range(CONV_W - 1)]
    full += [xa_ref[pl.ds(t, nseq, stride=seqlen), :] for t in range(seqlen)]
    h = h0_ref[...]
    for t in range(seqlen):
        xc = cb_ref[...] + cw_ref[0:1, :] * full[t]
        for j in range(1, CONV_W):
            xc = xc + cw_ref[j:j + 1, :] * full[t + j]
        a, mult, ing = _lru_coeffs(xc, wbd_ref, bg_ref, cl, 0)
        h = a * h + mult * ing * xc
        hs_ref[pl.ds(t, nseq, stride=seqlen), :] = h
    ya_ref[...] = (hs_ref[...] * ga_ref[...].astype(F32)).astype(BF16)
    for j in range(CONV_W - 1):
        convout_ref[j] = full[seqlen + j]
    hout_ref[...] = h


def _lru_step(xa, act, nseq, seqlen, conv0_t, h0, lw):
    w = xa.shape[1]
    rows = nseq * seqlen
    ng = w // LANES
    col = lambda g: (0, g)
    return pl.pallas_call(
        functools.partial(_lru_step_kernel, nseq=nseq, seqlen=seqlen),
        name="lru_step",
        grid=(ng,),
        in_specs=[
            pl.BlockSpec((rows, LANES), col),
            pl.BlockSpec((rows, LANES), col),
            pl.BlockSpec((CONV_W - 1, nseq, LANES), lambda g: (0, 0, g)),
            pl.BlockSpec((nseq, LANES), col),
            pl.BlockSpec((CONV_W, LANES), col),
            pl.BlockSpec((1, LANES), col),
            pl.BlockSpec((1, LANES, 2 * LANES), lambda g: (g, 0, 0)),
            pl.BlockSpec((1, 1, 2 * LANES), lambda g: (g, 0, 0)),
            pl.BlockSpec((1, LANES), col),
        ],
        out_specs=[
            pl.BlockSpec((rows, LANES), col),
            pl.BlockSpec((CONV_W - 1, nseq, LANES), lambda g: (0, 0, g)),
            pl.BlockSpec((nseq, LANES), col),
        ],
        out_shape=[jax.ShapeDtypeStruct((rows, w), BF16),
                   jax.ShapeDtypeStruct((CONV_W - 1, nseq, w), F32),
                   jax.ShapeDtypeStruct((nseq, w), F32)],
        scratch_shapes=[pltpu.VMEM((rows, LANES), F32)],
        compiler_params=_cparams(1),
    )(xa, act, conv0_t, h0, lw["conv_w"], lw["conv_b"], lw["wbd"], lw["bg"], lw["lam"])


def _log_gamma(h):
    return math.log1p(-(2.0 ** (-5.0 - h)))


def _ret_kernel(q_ref, k_ref, v_ref, sg_ref, s0_ref, ng_ref, *refs, nb, clen, cps, emit_y):
    if emit_y:
        y_ref, sout_ref, s_ref, dec_ref, rdec_ref = refs
    else:
        sout_ref, s_ref, dec_ref, rdec_ref = refs
    c = pl.program_id(1)
    rows = nb * clen
    heads = range(RET_HEADS)
    chunks = range(cps)

    @pl.when((pl.program_id(0) == 0) & (c == 0))
    def _():
        li = lax.broadcasted_iota(jnp.int32, (rows, rows), 0)
        mi = lax.broadcasted_iota(jnp.int32, (rows, rows), 1)
        keep = (li // clen == mi // clen) & (li >= mi)
        diff = jnp.where(keep, li - mi, 0).astype(F32)
        tpos = (lax.broadcasted_iota(jnp.int32, (rows, LANES), 0) % clen).astype(F32)
        for h in heads:
            lg = _log_gamma(h)
            dec_ref[h] = jnp.where(keep, jnp.exp(lg * diff), 0.0)
            rdec_ref[0, h] = jnp.exp(lg * (tpos + 1.0))
            rdec_ref[1, h] = jnp.exp(lg * (clen - 1.0 - tpos))

    @pl.when(c == 0)
    def _():
        for n in range(nb):
            s_ref[n] = s0_ref[n if s0_ref.shape[0] == nb else 0]

    if nb > 1:
        seq_of_row = lax.broadcasted_iota(jnp.int32, (rows, nb * HEAD_DIM), 0) // clen
        seq_of_col = lax.broadcasted_iota(jnp.int32, (rows, nb * HEAD_DIM), 1) // HEAD_DIM
        own = seq_of_row == seq_of_col
    tn = (((0,), (0,)), ((), ()))
    nt = (((1,), (1,)), ((), ()))
    hsl = [slice(h * HEAD_DIM, (h + 1) * HEAD_DIM) for h in heads]
    rsl = [slice(ci * rows, (ci + 1) * rows) for ci in chunks]
    q = [[q_ref[rsl[ci], sl] for sl in hsl] for ci in chunks]
    k = [[k_ref[rsl[ci], sl] for sl in hsl] for ci in chunks]
    v = [[v_ref[rsl[ci], sl] for sl in hsl] for ci in chunks]

    if emit_y:
        scores = [[lax.dot_general(q[ci][h], k[ci][h], nt, preferred_element_type=F32) for h in heads]
                  for ci in chunks]
    upd = []
    for ci in chunks:
        upd.append([])
        for h in heads:
            kd = (k[ci][h].astype(F32) * rdec_ref[1, h]).astype(BF16)
            if nb > 1:
                v_bd = jnp.where(own, jnp.concatenate([v[ci][h].astype(F32)] * nb, axis=1), 0.0).astype(BF16)
            else:
                v_bd = v[ci][h]
            upd[ci].append(lax.dot_general(kd, v_bd, tn, preferred_element_type=F32))

    state = [[[s_ref[n, h] for n in range(nb)] for h in heads]]
    for ci in chunks:
        state.append([[math.exp(clen * _log_gamma(h)) * state[ci][h][n]
                       + upd[ci][h][:, n * HEAD_DIM:(n + 1) * HEAD_DIM] for n in range(nb)] for h in heads])
    for h in heads:
        for n in range(nb):
            s_ref[n, h] = state[cps][h][n]

    if emit_y:
        inter = []
        for ci in chunks:
            inter.append([])
            for h in heads:
                s_cat = jnp.concatenate(state[ci][h], axis=1) if nb > 1 else state[ci][h][0]
                qs = jnp.dot(q[ci][h], s_cat.astype(BF16), preferred_element_type=F32)
                if nb > 1:
                    qs = jnp.where(own, qs, 0.0)
                    qs = sum(qs[:, n * HEAD_DIM:(n + 1) * HEAD_DIM] for n in range(nb))
                inter[ci].append(qs)
        for ci in chunks:
            for h in heads:
                p = (scores[ci][h] * dec_ref[h]).astype(BF16)
                o = jnp.dot(p, v[ci][h], preferred_element_type=F32) + inter[ci][h] * rdec_ref[0, h]
                mu = jnp.mean(o, axis=-1, keepdims=True)
                dev = o - mu
                var = jnp.mean(dev * dev, axis=-1, keepdims=True)
                normed = dev * lax.rsqrt(var + EPS) * ng_ref[:, hsl[h]]
                y_ref[rsl[ci], hsl[h]] = (sg_ref[rsl[ci], hsl[h]].astype(F32) * normed).astype(BF16)

    @pl.when(c == pl.num_programs(1) - 1)
    def _():
        sout_ref[...] = s_ref[...]


def _retention(act, row0, nseq, seqlen, nb, clen, cps, s0, norm_g, emit_y):
    w = RET_HEADS * HEAD_DIM
    rows = nb * clen * cps
    nc = seqlen // (clen * cps)
    assert nb == 1 or (nc == 1 and cps == 1)
    rb0 = row0 // rows
    rmap = lambda col: (lambda b, c: (rb0 + b * nc + c, col))
    shared = s0.shape[0] != nseq
    s_map = (lambda b, c: (0, 0, 0, 0)) if shared else (lambda b, c: (b, 0, 0, 0))
    s_block = (1 if shared else nb, RET_HEADS, HEAD_DIM, HEAD_DIM)
    out_specs = [pl.BlockSpec((nb, RET_HEADS, HEAD_DIM, HEAD_DIM), lambda b, c: (b, 0, 0, 0))]
    out_shape = [jax.ShapeDtypeStruct((nseq, RET_HEADS, HEAD_DIM, HEAD_DIM), F32)]
    if emit_y:
        out_specs = [pl.BlockSpec((rows, w), lambda b, c: (b * nc + c, 0))] + out_specs
        out_shape = [jax.ShapeDtypeStruct((nseq * seqlen, w), BF16)] + out_shape
    return pl.pallas_call(
        functools.partial(_ret_kernel, nb=nb, clen=clen, cps=cps, emit_y=emit_y),
        name="retention",
        grid=(nseq // nb, nc),
        in_specs=[
            pl.BlockSpec((rows, w), rmap(_ACT_Q)),
            pl.BlockSpec((rows, w), rmap(_ACT_K)),
            pl.BlockSpec((rows, w), rmap(_ACT_V)),
            pl.BlockSpec((rows, w), rmap(_ACT_SILU)),
            pl.BlockSpec(s_block, s_map),
            pl.BlockSpec((1, w), lambda b, c: (0, 0)),
        ],
        out_specs=out_specs,
        out_shape=out_shape,
        scratch_shapes=[pltpu.VMEM((nb, RET_HEADS, HEAD_DIM, HEAD_DIM), F32),
                        pltpu.VMEM((RET_HEADS, nb * clen, nb * clen), F32),
                        pltpu.VMEM((2, RET_HEADS, nb * clen, LANES), F32)],
        compiler_params=_cparams(2),
    )(act, act, act, act, s0, norm_g)


def _outproj_kernel(x_ref, ya_ref, yb_ref, sga_ref, sgb_ref, pa_ref, pb_ref, wo_ref, *refs, n_side):
    side_src, o_ref, side_dst = refs[:n_side], refs[n_side], refs[n_side + 1:]
    for src, dst in zip(side_src, side_dst):
        dst[...] = src[...].astype(BF16)
    ma = jnp.dot(ya_ref[...], pa_ref[...], preferred_element_type=F32)
    mb = jnp.dot(yb_ref[...], pb_ref[...], preferred_element_type=F32)
    merged = sga_ref[...].astype(F32) * ma + sgb_ref[...].astype(F32) * mb
    o_ref[...] = x_ref[...] + jnp.dot(merged.astype(BF16), wo_ref[...], preferred_element_type=F32)


def _outproj(x, ya, yb, act, pa, pb, wo, bm, side=()):
    m, d = x.shape
    w = ya.shape[1]
    nd = d // 1024
    row = lambda i: (i, 0)
    const = lambda i: (0, 0)
    single = pl.Buffered(1)
    in_specs = [
        pl.BlockSpec((bm, d), row),
        pl.BlockSpec((bm, w), row),
        pl.BlockSpec((bm, w), row),
        pl.BlockSpec((bm, d), lambda i: (i, _ACT_SGA // nd)),
        pl.BlockSpec((bm, d), lambda i: (i, _ACT_SGB // nd)),
        pl.BlockSpec((w, d), const, pipeline_mode=single),
        pl.BlockSpec((w, d), const, pipeline_mode=single),
        pl.BlockSpec((d, d), const, pipeline_mode=single),
    ]
    out_specs = [pl.BlockSpec((bm, d), row)]
    out_shape = [jax.ShapeDtypeStruct((m, d), F32)]
    for arr, block, index_map in side:
        in_specs.append(pl.BlockSpec(block, index_map))
        out_specs.append(pl.BlockSpec(block, index_map))
        out_shape.append(jax.ShapeDtypeStruct(arr.shape, BF16))
    return pl.pallas_call(
        functools.partial(_outproj_kernel, n_side=len(side)),
        name="outproj",
        grid=(m // bm,),
        in_specs=in_specs,
        out_specs=out_specs,
        out_shape=out_shape,
        compiler_params=_cparams(1),
    )(x, ya, yb, act, act, pa, pb, wo, *[arr for arr, _, _ in side])


def _ffn_kernel(x_ref, g_ref, wu_ref, wd_ref, gf_ref, o_ref, hn_ref):
    j = pl.program_id(1)
    last = pl.num_programs(1) - 1

    def block(first, final):
        for r0, nr in _row_chunks(o_ref.shape[0], 512):
            rows = slice(r0, r0 + nr)
            if first:
                base = x_ref[rows, :]
                hn = _rmsnorm(base, g_ref[...]).astype(BF16)
                hn_ref[rows, :] = hn
            else:
                base = o_ref[rows, :]
                hn = hn_ref[rows, :]
            u = jnp.dot(hn, wu_ref[...], preferred_element_type=F32)
            r = jnp.square(jnp.maximum(u, 0.0)).astype(BF16)
            acc = base + jnp.dot(r, wd_ref[...], preferred_element_type=F32)
            o_ref[rows, :] = _rmsnorm(acc, gf_ref[...]) if final else acc

    pl.when(j == 0)(lambda: block(True, False))
    pl.when((j > 0) & (j < last))(lambda: block(False, False))
    pl.when(j == last)(lambda: block(False, True))


def _ffn(x, g, wu, wd, gf, bm, bf):
    m, d = x.shape
    dff = wu.shape[1]
    assert dff // bf >= 2
    return pl.pallas_call(
        _ffn_kernel,
        name="ffn",
        grid=(m // bm, dff // bf),
        in_specs=[
            pl.BlockSpec((bm, d), lambda i, j: (i, 0)),
            pl.BlockSpec((1, d), lambda i, j: (0, 0)),
            pl.BlockSpec((d, bf), lambda i, j: (0, j)),
            pl.BlockSpec((bf, d), lambda i, j: (j, 0)),
            pl.BlockSpec((1, d), lambda i, j: (0, 0)),
        ],
        out_specs=pl.BlockSpec((bm, d), lambda i, j: (i, 0)),
        out_shape=jax.ShapeDtypeStruct((m, d), F32),
        scratch_shapes=[pltpu.VMEM((bm, d), BF16)],
        compiler_params=_cparams(2),
    )(x, g, wu, wd, gf)


def _rope_tables(pos):
    inv = ROPE_BASE ** (-jnp.arange(0, HEAD_DIM, 2, dtype=F32) / HEAD_DIM)
    ang = pos.astype(F32)[:, None] * inv[None, :]
    cos, sin = jnp.cos(ang), jnp.sin(ang)
    return jnp.concatenate([cos, cos], axis=1), jnp.concatenate([-sin, sin], axis=1)


def _gate_weights(wa, wx, ba, bx):
    nblk, blk, _ = wa.shape
    per = LANES // blk
    ng = nblk // per

    def bd(wt):
        wt = wt.reshape(ng, per, blk, blk)
        eye = jnp.eye(per, dtype=wt.dtype)
        return jnp.einsum("gpcd,pq->gpcqd", wt, eye).reshape(ng, LANES, LANES)

    wbd = jnp.concatenate([bd(wa), bd(wx)], axis=2).astype(BF16)
    bg = jnp.concatenate([ba.reshape(ng, 1, LANES), bx.reshape(ng, 1, LANES)], axis=2)
    return wbd, bg


def kernel(x_prompt, x_sample, state_conv, state_lru, state_ret, meta_tokens, norm_mix_g, w_in, conv_w,
           conv_b, lru_wa, lru_ba, lru_wx, lru_bx, lru_lam, ret_norm_g, p_a, p_b, w_out, norm_ffn_g,
           w_up, w_down, norm_f_g):
    assert w_in.shape[0] == 1
    nb_p, t_p, d = x_prompt.shape
    nb_s, t_s, _ = x_sample.shape
    w = conv_w.shape[-1]
    dff = w_up.shape[-1]
    rows_p, rows_s = nb_p * t_p, nb_s * t_s

    g_mix, g_ffn, g_f = norm_mix_g[0][None], norm_ffn_g[0][None], norm_f_g[None]
    wbd, bg = _gate_weights(lru_wa[0], lru_wx[0], lru_ba[0], lru_bx[0])
    lw = dict(conv_w=conv_w[0], conv_b=conv_b[0][None], wbd=wbd, bg=bg, lam=lru_lam[0][None])
    ret_g = ret_norm_g[0][None]

    cos_p, sin_p = _rope_tables(N_META + jnp.arange(t_p, dtype=jnp.int32))
    pos_sm = jnp.concatenate([jnp.tile(PAST_LEN + jnp.arange(t_s, dtype=jnp.int32), nb_s),
                              jnp.arange(N_META, dtype=jnp.int32)])
    cos_s, sin_s = _rope_tables(pos_sm)
    x_p2 = x_prompt.reshape(rows_p, d)
    x_s2 = x_sample.reshape(rows_s, d)
    xa_s, act_s, w_in_b = _inproj(x_s2, g_mix, w_in[0], cos_s, sin_s, rows_s + N_META, 512, 1,
                                  tail=meta_tokens.astype(x_sample.dtype))
    bm_p = 1024
    n_i, n_j = rows_p // bm_p, 8
    side_map = lambda i, j: (i, jnp.minimum(j, n_j - 1))
    sides = tuple((wt[0], (wt.shape[1] // n_i, wt.shape[2] // n_j), side_map) for wt in (w_up, p_a, p_b, w_out))
    xa_p, act_p, wu_b, pa_b, pb_b, wo_b = _inproj(x_p2, g_mix, w_in_b, cos_p, sin_p, bm_p, 1024, t_p // bm_p,
                                                  side=sides)


    zc = jnp.zeros((1, CONV_W - 1, w), F32)
    zh = jnp.zeros((1, 1, w), F32)
    zs = jnp.zeros((1, RET_HEADS, HEAD_DIM, HEAD_DIM), F32)
    conv_m, h_m = _lru_seq(xa_s, act_s, rows_s, 1, N_META, N_META, zc, zh, lw, emit_y=False, first_pos_zero=True)
    (s_m,) = _retention(act_s, rows_s, 1, N_META, 1, N_META, 1, zs, ret_g, emit_y=False)

    ya_p, conv_p, h_p = _lru_seq(xa_p, act_p, 0, nb_p, t_p, 512, conv_m, h_m, lw, emit_y=True, first_pos_zero=False)
    yb_p, s_p = _retention(act_p, 0, nb_p, t_p, 1, CHUNK, 4, s_m, ret_g, emit_y=True)

    conv0_t = jnp.transpose(state_conv[0], (1, 0, 2))
    ya_s, conv_s_t, h_s = _lru_step(xa_s, act_s, nb_s, t_s, conv0_t, state_lru[0], lw)
    yb_s, s_s = _retention(act_s, 0, nb_s, t_s, 8, t_s, 1, state_ret[0], ret_g, emit_y=True)

    bm_o = 256
    side_down = (w_down[0], (dff // (rows_p // bm_o), d), lambda i: (i, 0))
    x1_p, wd_b = _outproj(x_p2, ya_p, yb_p, act_p, pa_b, pb_b, wo_b, bm_o, side=(side_down,))
    (x1_s,) = _outproj(x_s2, ya_s, yb_s, act_s, pa_b, pb_b, wo_b, bm_o)
    y_p = _ffn(x1_p, g_ffn, wu_b, wd_b, g_f, 1024, 1024)
    y_s = _ffn(x1_s, g_ffn, wu_b, wd_b, g_f, 1024, 1024)

    return (y_p.reshape(nb_p, t_p, d), y_s.reshape(nb_s, t_s, d),
            conv_p[None], h_p.reshape(1, nb_p, w), s_p[None],
            jnp.transpose(conv_s_t, (1, 0, 2))[None], h_s[None], s_s[None])
```

```python
import functools
import math

import jax
import jax.numpy as jnp
from jax import lax
from jax.experimental import pallas as pl
from jax.experimental.pallas import tpu as pltpu

F32 = jnp.float32
BF16 = jnp.bfloat16

N_META = 16
PAST_LEN = 16384
LRU_BLOCKS = 16
CONV_W = 4
LRU_C = 8.0
RET_HEADS = 8
HEAD_DIM = 128
CHUNK = 128
ROPE_BASE = 10000.0
EPS = 1e-6

LANES = 128
VMEM_LIMIT = 56 << 20


def _cparams(n_axes):
    return pltpu.CompilerParams(dimension_semantics=("arbitrary",) * n_axes, vmem_limit_bytes=VMEM_LIMIT)


def _rmsnorm(x, g):
    return x * lax.rsqrt(jnp.mean(x * x, axis=-1, keepdims=True) + EPS) * g


def _sigmoid(x):
    return 0.5 * jnp.tanh(0.5 * x) + 0.5


def _row_chunks(rows, size):
    n = max(rows // size, 1)
    return [(c * size, size if c < n - 1 else rows - c * size) for c in range(n)]


_COL_GATE, _COL_Q, _COL_K, _COL_V, _COL_G, _COL_GATE_A = 1, 2, 3, 4, 5, 6
_ACT_GELU, _ACT_Q, _ACT_K, _ACT_V, _ACT_SGA, _ACT_SGB, _ACT_SILU = 0, 1, 2, 3, 4, 6, 8


def _act_block(kind):
    return jnp.where(kind <= _COL_V, kind - 1, jnp.where(kind == _COL_G, _ACT_SILU, kind - 2))


def _inproj_kernel(x_ref, g_ref, w_ref, cos_ref, sin_ref, *refs, nper, cast_w, n_side, has_tail):
    if has_tail:
        tail_ref, refs = refs[0], refs[1:]
    side_src, refs = refs[:n_side], refs[n_side:]
    xa_ref, act_ref = refs[0], refs[1]
    refs = refs[2:]
    if cast_w:
        wb_ref, refs = refs[0], refs[1:]
    side_dst, (xn_ref,) = refs[:n_side], refs[n_side:]
    j = pl.program_id(1)
    kind = j // nper
    bm, bn = xa_ref.shape

    for src, dst in zip(side_src, side_dst):
        dst[...] = src[...].astype(BF16)
    if cast_w:
        wb_ref[...] = w_ref[...].astype(BF16)
        w_bf = wb_ref
    else:
        w_bf = w_ref

    def project(epilogue, normalize=False):
        if normalize:
            nx = x_ref.shape[0]
            xn_ref[0:nx, :] = _rmsnorm(x_ref[...], g_ref[...]).astype(BF16)
            if has_tail:
                xn_ref[nx:, :] = _rmsnorm(tail_ref[...], g_ref[...]).astype(BF16)
        split = bn - 256
        pieces = [(slice(0, bm), slice(0, split))]
        pieces += [(slice(r0, r0 + nr), slice(split, bn)) for r0, nr in _row_chunks(bm, 256)]
        for rows, cols in pieces:
            epilogue(jnp.dot(xn_ref[rows, :], w_bf[:, cols], preferred_element_type=F32), rows, cols)

    def to_xa(acc, rows, cols):
        xa_ref[rows, cols] = acc

    def rope(scale):
        def epilogue(acc, rows, cols):
            cos, sin = cos_ref[rows, :], sin_ref[rows, :]
            for c0 in range(0, cols.stop - cols.start, HEAD_DIM):
                a = acc[:, c0:c0 + HEAD_DIM]
                r = a * cos + pltpu.roll(a, HEAD_DIM // 2, axis=1) * sin
                if scale is not None:
                    r = r * scale
                act_ref[rows, cols.start + c0:cols.start + c0 + HEAD_DIM] = r.astype(BF16)
        return epilogue

    def elementwise(fn):
        def epilogue(acc, rows, cols):
            act_ref[rows, cols] = fn(acc).astype(BF16)
        return epilogue

    pl.when(j == 0)(lambda: project(to_xa, normalize=True))
    if nper > 1:
        pl.when((j > 0) & (kind == 0))(lambda: project(to_xa))
    pl.when(kind == _COL_GATE)(lambda: project(elementwise(jax.nn.gelu)))
    pl.when(kind == _COL_Q)(lambda: project(rope(None)))
    pl.when(kind == _COL_K)(lambda: project(rope(HEAD_DIM ** -0.5)))
    pl.when(kind == _COL_V)(lambda: project(elementwise(lambda a: a)))
    pl.when(kind == _COL_G)(lambda: project(elementwise(lambda a: a * _sigmoid(a))))
    pl.when(kind >= _COL_GATE_A)(lambda: project(elementwise(_sigmoid)))


def _inproj(x, g, w, cos, sin, bm, bn, table_blocks, side=(), tail=None):
    xrows, d = x.shape
    m = xrows if tail is None else xrows + tail.shape[0]
    assert tail is None or m == bm
    n = w.shape[1]
    width = 1024
    nper = width // bn
    cast_w = w.dtype != BF16

    def act_map(i, j):
        kind = j // nper
        return i, jnp.where(kind == 0, 0, _act_block(kind) * nper + j % nper)

    in_specs = [
        pl.BlockSpec((min(bm, xrows), d), lambda i, j: (i, 0)),
        pl.BlockSpec((1, d), lambda i, j: (0, 0)),
        pl.BlockSpec((d, bn), lambda i, j: (0, j)),
        pl.BlockSpec((bm, LANES), lambda i, j: (i % table_blocks, 0)),
        pl.BlockSpec((bm, LANES), lambda i, j: (i % table_blocks, 0)),
    ]
    if tail is not None:
        in_specs.append(pl.BlockSpec(tail.shape, lambda i, j: (0, 0)))
    out_specs = [
        pl.BlockSpec((bm, bn), lambda i, j: (i, jnp.minimum(j, nper - 1))),
        pl.BlockSpec((bm, bn), act_map),
    ]
    out_shape = [jax.ShapeDtypeStruct((m, width), F32), jax.ShapeDtypeStruct((m, n - width), BF16)]
    if cast_w:
        out_specs.append(pl.BlockSpec((d, bn), lambda i, j: (0, j)))
        out_shape.append(jax.ShapeDtypeStruct(w.shape, BF16))
    for arr, block, index_map in side:
        in_specs.append(pl.BlockSpec(block, index_map))
        out_specs.append(pl.BlockSpec(block, index_map))
        out_shape.append(jax.ShapeDtypeStruct(arr.shape, BF16))
    return pl.pallas_call(
        functools.partial(_inproj_kernel, nper=nper, cast_w=cast_w, n_side=len(side), has_tail=tail is not None),
        name="inproj",
        grid=(m // bm, n // bn),
        in_specs=in_specs,
        out_specs=out_specs,
        out_shape=out_shape,
        scratch_shapes=[pltpu.VMEM((bm, d), BF16)],
        compiler_params=_cparams(2),
    )(x, g, w, cos, sin, *([] if tail is None else [tail]), *[arr for arr, _, _ in side])


def _lru_coeffs(xc, wbd_ref, bg_ref, cl_row, group):
    gates = jnp.dot(xc.astype(BF16), wbd_ref[group], preferred_element_type=F32) + bg_ref[group]
    rec = _sigmoid(gates[:, :LANES])
    ing = _sigmoid(gates[:, LANES:])
    log_a = rec * cl_row
    a = jnp.exp(log_a)
    mult = jnp.sqrt(jnp.tanh(-log_a) * (a * a + 1.0))
    return a, mult, ing


def _lru_seq_kernel(xa_ref, ga_ref, conv0_ref, h0_ref, cw_ref, cb_ref, wbd_ref, bg_ref, lam_ref,
                    *refs, tr, emit_y, first_pos_zero):
    if emit_y:
        ya_ref, convout_ref, hout_ref, halo_ref, h_ref, a_ref, b_ref = refs
    else:
        convout_ref, hout_ref, halo_ref, h_ref, a_ref, b_ref = refs
    t = pl.program_id(1)
    halo = 8

    @pl.when(t == 0)
    def _():
        halo_ref[halo - 3:halo, :] = conv0_ref[0]
        h_ref[...] = h0_ref[0]

    halo_ref[halo:halo + tr, :] = xa_ref[...]
    xc = cb_ref[...] + cw_ref[0:1, :] * halo_ref[halo - 3:halo - 3 + tr, :]
    for j in range(1, CONV_W):
        xc = xc + cw_ref[j:j + 1, :] * halo_ref[halo - 3 + j:halo - 3 + j + tr, :]
    halo_ref[halo - 3:halo, :] = halo_ref[halo + tr - 3:halo + tr, :]

    cl = LRU_C * jax.nn.log_sigmoid(lam_ref[...])
    for g in range(xc.shape[1] // LANES):
        sl = slice(g * LANES, (g + 1) * LANES)
        a, mult, ing = _lru_coeffs(xc[:, sl], wbd_ref, bg_ref, cl[:, sl], g)
        if first_pos_zero:
            row = lax.broadcasted_iota(jnp.int32, a.shape, 0) + t * tr
            mult = jnp.where(row == 0, 1.0, mult)
        a_ref[:, sl] = a
        b_ref[:, sl] = mult * ing * xc[:, sl]

    h = h_ref[...]
    for r in range(tr):
        h = a_ref[r:r + 1, :] * h + b_ref[r:r + 1, :]
        halo_ref[halo + r:halo + r + 1, :] = h
    h_ref[...] = h
    if emit_y:
        ya_ref[...] = (halo_ref[halo:halo + tr, :] * ga_ref[...].astype(F32)).astype(BF16)

    @pl.when(t == pl.num_programs(1) - 1)
    def _():
        convout_ref[0] = halo_ref[halo - 3:halo, :]
        hout_ref[0] = h_ref[...]


def _lru_seq(xa, act, row0, nseq, seqlen, tr, conv0, h0, lw, emit_y, first_pos_zero):
    w = xa.shape[1]
    nt = seqlen // tr
    rb0 = row0 // tr
    ng = w // LANES
    state_map = (lambda b, t: (b, 0, 0)) if conv0.shape[0] == nseq else (lambda b, t: (0, 0, 0))
    const2 = lambda b, t: (0, 0)
    const3 = lambda b, t: (0, 0, 0)
    out_specs = [pl.BlockSpec((1, CONV_W - 1, w), lambda b, t: (b, 0, 0)),
                 pl.BlockSpec((1, 1, w), lambda b, t: (b, 0, 0))]
    out_shape = [jax.ShapeDtypeStruct((nseq, CONV_W - 1, w), F32), jax.ShapeDtypeStruct((nseq, 1, w), F32)]
    if emit_y:
        out_specs = [pl.BlockSpec((tr, w), lambda b, t: (b * nt + t, 0))] + out_specs
        out_shape = [jax.ShapeDtypeStruct((nseq * seqlen, w), BF16)] + out_shape
    return pl.pallas_call(
        functools.partial(_lru_seq_kernel, tr=tr, emit_y=emit_y, first_pos_zero=first_pos_zero),
        name="lru_seq",
        grid=(nseq, nt),
        in_specs=[
            pl.BlockSpec((tr, w), lambda b, t: (rb0 + b * nt + t, 0)),
            pl.BlockSpec((tr, w), lambda b, t: (rb0 + b * nt + t, _ACT_GELU)),
            pl.BlockSpec((1, CONV_W - 1, w), state_map),
            pl.BlockSpec((1, 1, w), state_map),
            pl.BlockSpec((CONV_W, w), const2),
            pl.BlockSpec((1, w), const2),
            pl.BlockSpec((ng, LANES, 2 * LANES), const3),
            pl.BlockSpec((ng, 1, 2 * LANES), const3),
            pl.BlockSpec((1, w), const2),
        ],
        out_specs=out_specs,
        out_shape=out_shape,
        scratch_shapes=[pltpu.VMEM((8 + tr, w), F32), pltpu.VMEM((1, w), F32),
                        pltpu.VMEM((tr, w), F32), pltpu.VMEM((tr, w), F32)],
        compiler_params=_cparams(2),
    )(xa, act, conv0, h0, lw["conv_w"], lw["conv_b"], lw["wbd"], lw["bg"], lw["lam"])


def _lru_step_kernel(xa_ref, ga_ref, conv0_ref, h0_ref, cw_ref, cb_ref, wbd_ref, bg_ref, lam_ref,
                     ya_ref, convout_ref, hout_ref, hs_ref, *, nseq, seqlen):
    cl = LRU_C * jax.nn.log_sigmoid(lam_ref[...])
    full = [conv0_ref[j] for j in range(CONV_W - 1)]
    full += [xa_ref[pl.ds(t, nseq, stride=seqlen), :] for t in range(seqlen)]
    h = h0_ref[...]
    for t in range(seqlen):
        xc = cb_ref[...] + cw_ref[0:1, :] * full[t]
        for j in range(1, CONV_W):
            xc = xc + cw_ref[j:j + 1, :] * full[t + j]
        a, mult, ing = _lru_coeffs(xc, wbd_ref, bg_ref, cl, 0)
        h = a * h + mult * ing * xc
        hs_ref[pl.ds(t, nseq, stride=seqlen), :] = h
    ya_ref[...] = (hs_ref[...] * ga_ref[...].astype(F32)).astype(BF16)
    for j in range(CONV_W - 1):
        convout_ref[j] = full[seqlen + j]
    hout_ref[...] = h


def _lru_step(xa, act, nseq, seqlen, conv0_t, h0, lw):
    w = xa.shape[1]
    rows = nseq * seqlen
    ng = w // LANES
    col = lambda g: (0, g)
    return pl.pallas_call(
        functools.partial(_lru_step_kernel, nseq=nseq, seqlen=seqlen),
        name="lru_step",
        grid=(ng,),
        in_specs=[
            pl.BlockSpec((rows, LANES), col),
            pl.BlockSpec((rows, LANES), col),
            pl.BlockSpec((CONV_W - 1, nseq, LANES), lambda g: (0, 0, g)),
            pl.BlockSpec((nseq, LANES), col),
            pl.BlockSpec((CONV_W, LANES), col),
            pl.BlockSpec((1, LANES), col),
            pl.BlockSpec((1, LANES, 2 * LANES), lambda g: (g, 0, 0)),
            pl.BlockSpec((1, 1, 2 * LANES), lambda g: (g, 0, 0)),
            pl.BlockSpec((1, LANES), col),
        ],
        out_specs=[
            pl.BlockSpec((rows, LANES), col),
            pl.BlockSpec((CONV_W - 1, nseq, LANES), lambda g: (0, 0, g)),
            pl.BlockSpec((nseq, LANES), col),
        ],
        out_shape=[jax.ShapeDtypeStruct((rows, w), BF16),
                   jax.ShapeDtypeStruct((CONV_W - 1, nseq, w), F32),
                   jax.ShapeDtypeStruct((nseq, w), F32)],
        scratch_shapes=[pltpu.VMEM((rows, LANES), F32)],
        compiler_params=_cparams(1),
    )(xa, act, conv0_t, h0, lw["conv_w"], lw["conv_b"], lw["wbd"], lw["bg"], lw["lam"])


def _log_gamma(h):
    return math.log1p(-(2.0 ** (-5.0 - h)))


def _ret_kernel(q_ref, k_ref, v_ref, sg_ref, s0_ref, ng_ref, *refs, nb, clen, cps, emit_y):
    if emit_y:
        y_ref, sout_ref, s_ref, dec_ref, rdec_ref = refs
    else:
        sout_ref, s_ref, dec_ref, rdec_ref = refs
    c = pl.program_id(1)
    rows = nb * clen
    heads = range(RET_HEADS)
    chunks = range(cps)

    @pl.when((pl.program_id(0) == 0) & (c == 0))
    def _():
        li = lax.broadcasted_iota(jnp.int32, (rows, rows), 0)
        mi = lax.broadcasted_iota(jnp.int32, (rows, rows), 1)
        keep = (li // clen == mi // clen) & (li >= mi)
        diff = jnp.where(keep, li - mi, 0).astype(F32)
        tpos = (lax.broadcasted_iota(jnp.int32, (rows, LANES), 0) % clen).astype(F32)
        for h in heads:
            lg = _log_gamma(h)
            dec_ref[h] = jnp.where(keep, jnp.exp(lg * diff), 0.0)
            rdec_ref[0, h] = jnp.exp(lg * (tpos + 1.0))
            rdec_ref[1, h] = jnp.exp(lg * (clen - 1.0 - tpos))

    @pl.when(c == 0)
    def _():
        for n in range(nb):
            s_ref[n] = s0_ref[n if s0_ref.shape[0] == nb else 0]

    if nb > 1:
        seq_of_row = lax.broadcasted_iota(jnp.int32, (rows, nb * HEAD_DIM), 0) // clen
        seq_of_col = lax.broadcasted_iota(jnp.int32, (rows, nb * HEAD_DIM), 1) // HEAD_DIM
        own = seq_of_row == seq_of_col
    tn = (((0,), (0,)), ((), ()))
    nt = (((1,), (1,)), ((), ()))
    hsl = [slice(h * HEAD_DIM, (h + 1) * HEAD_DIM) for h in heads]
    rsl = [slice(ci * rows, (ci + 1) * rows) for ci in chunks]
    q = [[q_ref[rsl[ci], sl] for sl in hsl] for ci in chunks]
    k = [[k_ref[rsl[ci], sl] for sl in hsl] for ci in chunks]
    v = [[v_ref[rsl[ci], sl] for sl in hsl] for ci in chunks]

    if emit_y:
        scores = [[lax.dot_general(q[ci][h], k[ci][h], nt, preferred_element_type=F32) for h in heads]
                  for ci in chunks]
    upd = []
    for ci in chunks:
        upd.append([])
        for h in heads:
            kd = (k[ci][h].astype(F32) * rdec_ref[1, h]).astype(BF16)
            if nb > 1:
                v_bd = jnp.where(own, jnp.concatenate([v[ci][h].astype(F32)] * nb, axis=1), 0.0).astype(BF16)
            else:
                v_bd = v[ci][h]
            upd[ci].append(lax.dot_general(kd, v_bd, tn, preferred_element_type=F32))

    state = [[[s_ref[n, h] for n in range(nb)] for h in heads]]
    for ci in chunks:
        state.append([[math.exp(clen * _log_gamma(h)) * state[ci][h][n]
                       + upd[ci][h][:, n * HEAD_DIM:(n + 1) * HEAD_DIM] for n in range(nb)] for h in heads])
    for h in heads:
        for n in range(nb):
            s_ref[n, h] = state[cps][h][n]

    if emit_y:
        inter = []
        for ci in chunks:
            inter.append([])
            for h in heads:
                s_cat = jnp.concatenate(state[ci][h], axis=1) if nb > 1 else state[ci][h][0]
                qs = jnp.dot(q[ci][h], s_cat.astype(BF16), preferred_element_type=F32)
                if nb > 1:
                    qs = jnp.where(own, qs, 0.0)
                    qs = sum(qs[:, n * HEAD_DIM:(n + 1) * HEAD_DIM] for n in range(nb))
                inter[ci].append(qs)
        for ci in chunks:
            for h in heads:
                p = (scores[ci][h] * dec_ref[h]).astype(BF16)
                o = jnp.dot(p, v[ci][h], preferred_element_type=F32) + inter[ci][h] * rdec_ref[0, h]
                mu = jnp.mean(o, axis=-1, keepdims=True)
                dev = o - mu
                var = jnp.mean(dev * dev, axis=-1, keepdims=True)
                normed = dev * lax.rsqrt(var + EPS) * ng_ref[:, hsl[h]]
                y_ref[rsl[ci], hsl[h]] = (sg_ref[rsl[ci], hsl[h]].astype(F32) * normed).astype(BF16)

    @pl.when(c == pl.num_programs(1) - 1)
    def _():
        sout_ref[...] = s_ref[...]


def _retention(act, row0, nseq, seqlen, nb, clen, cps, s0, norm_g, emit_y):
    w = RET_HEADS * HEAD_DIM
    rows = nb * clen * cps
    nc = seqlen // (clen * cps)
    assert nb == 1 or (nc == 1 and cps == 1)
    rb0 = row0 // rows
    rmap = lambda col: (lambda b, c: (rb0 + b * nc + c, col))
    shared = s0.shape[0] != nseq
    s_map = (lambda b, c: (0, 0, 0, 0)) if shared else (lambda b, c: (b, 0, 0, 0))
    s_block = (1 if shared else nb, RET_HEADS, HEAD_DIM, HEAD_DIM)
    out_specs = [pl.BlockSpec((nb, RET_HEADS, HEAD_DIM, HEAD_DIM), lambda b, c: (b, 0, 0, 0))]
    out_shape = [jax.ShapeDtypeStruct((nseq, RET_HEADS, HEAD_DIM, HEAD_DIM), F32)]
    if emit_y:
        out_specs = [pl.BlockSpec((rows, w), lambda b, c: (b * nc + c, 0))] + out_specs
        out_shape = [jax.ShapeDtypeStruct((nseq * seqlen, w), BF16)] + out_shape
    return pl.pallas_call(
        functools.partial(_ret_kernel, nb=nb, clen=clen, cps=cps, emit_y=emit_y),
        name="retention",
        grid=(nseq // nb, nc),
        in_specs=[
            pl.BlockSpec((rows, w), rmap(_ACT_Q)),
            pl.BlockSpec((rows, w), rmap(_ACT_K)),
            pl.BlockSpec((rows, w), rmap(_ACT_V)),
            pl.BlockSpec((rows, w), rmap(_ACT_SILU)),
            pl.BlockSpec(s_block, s_map),
            pl.BlockSpec((1, w), lambda b, c: (0, 0)),
        ],
        out_specs=out_specs,
        out_shape=out_shape,
        scratch_shapes=[pltpu.VMEM((nb, RET_HEADS, HEAD_DIM, HEAD_DIM), F32),
                        pltpu.VMEM((RET_HEADS, nb * clen, nb * clen), F32),
                        pltpu.VMEM((2, RET_HEADS, nb * clen, LANES), F32)],
        compiler_params=_cparams(2),
    )(act, act, act, act, s0, norm_g)


def _outproj_kernel(x_ref, ya_ref, yb_ref, sga_ref, sgb_ref, pa_ref, pb_ref, wo_ref, o_ref):
    for r0, nr in _row_chunks(o_ref.shape[0], 256):
        rows = slice(r0, r0 + nr)
        ma = jnp.dot(ya_ref[rows, :], pa_ref[...], preferred_element_type=F32)
        mb = jnp.dot(yb_ref[rows, :], pb_ref[...], preferred_element_type=F32)
        merged = sga_ref[rows, :].astype(F32) * ma + sgb_ref[rows, :].astype(F32) * mb
        o_ref[rows, :] = x_ref[rows, :] + jnp.dot(merged.astype(BF16), wo_ref[...], preferred_element_type=F32)


def _outproj(x, ya, yb, act, pa, pb, wo, bm):
    m, d = x.shape
    w = ya.shape[1]
    nd = d // 1024
    row = lambda i: (i, 0)
    const = lambda i: (0, 0)
    single = pl.Buffered(1)
    return pl.pallas_call(
        _outproj_kernel,
        name="outproj",
        grid=(m // bm,),
        in_specs=[
            pl.BlockSpec((bm, d), row),
            pl.BlockSpec((bm, w), row),
            pl.BlockSpec((bm, w), row),
            pl.BlockSpec((bm, d), lambda i: (i, _ACT_SGA // nd)),
            pl.BlockSpec((bm, d), lambda i: (i, _ACT_SGB // nd)),
            pl.BlockSpec((w, d), const, pipeline_mode=single),
            pl.BlockSpec((w, d), const, pipeline_mode=single),
            pl.BlockSpec((d, d), const, pipeline_mode=single),
        ],
        out_specs=pl.BlockSpec((bm, d), row),
        out_shape=jax.ShapeDtypeStruct((m, d), F32),
        compiler_params=_cparams(1),
    )(x, ya, yb, act, act, pa, pb, wo)


def _ffn_kernel(x_ref, g_ref, wu_ref, wd_ref, gf_ref, o_ref, hn_ref):
    j = pl.program_id(1)
    last = pl.num_programs(1) - 1

    def block(first, final):
        for r0, nr in _row_chunks(o_ref.shape[0], 512):
            rows = slice(r0, r0 + nr)
            if first:
                base = x_ref[rows, :]
                hn = _rmsnorm(base, g_ref[...]).astype(BF16)
                hn_ref[rows, :] = hn
            else:
                base = o_ref[rows, :]
                hn = hn_ref[rows, :]
            u = jnp.dot(hn, wu_ref[...], preferred_element_type=F32)
            r = jnp.square(jnp.maximum(u, 0.0)).astype(BF16)
            acc = base + jnp.dot(r, wd_ref[...], preferred_element_type=F32)
            o_ref[rows, :] = _rmsnorm(acc, gf_ref[...]) if final else acc

    pl.when(j == 0)(lambda: block(True, False))
    pl.when((j > 0) & (j < last))(lambda: block(False, False))
    pl.when(j == last)(lambda: block(False, True))


def _ffn(x, g, wu, wd, gf, bm, bf):
    m, d = x.shape
    dff = wu.shape[1]
    assert dff // bf >= 2
    return pl.pallas_call(
        _ffn_kernel,
        name="ffn",
        grid=(m // bm, dff // bf),
        in_specs=[
            pl.BlockSpec((bm, d), lambda i, j: (i, 0)),
            pl.BlockSpec((1, d), lambda i, j: (0, 0)),
            pl.BlockSpec((d, bf), lambda i, j: (0, j)),
            pl.BlockSpec((bf, d), lambda i, j: (j, 0)),
            pl.BlockSpec((1, d), lambda i, j: (0, 0)),
        ],
        out_specs=pl.BlockSpec((bm, d), lambda i, j: (i, 0)),
        out_shape=jax.ShapeDtypeStruct((m, d), F32),
        scratch_shapes=[pltpu.VMEM((bm, d), BF16)],
        compiler_params=_cparams(2),
    )(x, g, wu, wd, gf)


def _rope_tables(pos):
    inv = ROPE_BASE ** (-jnp.arange(0, HEAD_DIM, 2, dtype=F32) / HEAD_DIM)
    ang = pos.astype(F32)[:, None] * inv[None, :]
    cos, sin = jnp.cos(ang), jnp.sin(ang)
    return jnp.concatenate([cos, cos], axis=1), jnp.concatenate([-sin, sin], axis=1)


def _gate_weights(wa, wx, ba, bx):
    nblk, blk, _ = wa.shape
    per = LANES // blk
    ng = nblk // per

    def bd(wt):
        wt = wt.reshape(ng, per, blk, blk)
        eye = jnp.eye(per, dtype=wt.dtype)
        return jnp.einsum("gpcd,pq->gpcqd", wt, eye).reshape(ng, LANES, LANES)

    wbd = jnp.concatenate([bd(wa), bd(wx)], axis=2).astype(BF16)
    bg = jnp.concatenate([ba.reshape(ng, 1, LANES), bx.reshape(ng, 1, LANES)], axis=2)
    return wbd, bg


def kernel(x_prompt, x_sample, state_conv, state_lru, state_ret, meta_tokens, norm_mix_g, w_in, conv_w,
           conv_b, lru_wa, lru_ba, lru_wx, lru_bx, lru_lam, ret_norm_g, p_a, p_b, w_out, norm_ffn_g,
           w_up, w_down, norm_f_g):
    assert w_in.shape[0] == 1
    nb_p, t_p, d = x_prompt.shape
    nb_s, t_s, _ = x_sample.shape
    w = conv_w.shape[-1]
    rows_p, rows_s = nb_p * t_p, nb_s * t_s

    g_mix, g_ffn, g_f = norm_mix_g[0][None], norm_ffn_g[0][None], norm_f_g[None]
    wbd, bg = _gate_weights(lru_wa[0], lru_wx[0], lru_ba[0], lru_bx[0])
    lw = dict(conv_w=conv_w[0], conv_b=conv_b[0][None], wbd=wbd, bg=bg, lam=lru_lam[0][None])
    ret_g = ret_norm_g[0][None]

    cos_p, sin_p = _rope_tables(N_META + jnp.arange(t_p, dtype=jnp.int32))
    pos_sm = jnp.concatenate([jnp.tile(PAST_LEN + jnp.arange(t_s, dtype=jnp.int32), nb_s),
                              jnp.arange(N_META, dtype=jnp.int32)])
    cos_s, sin_s = _rope_tables(pos_sm)
    x_p2 = x_prompt.reshape(rows_p, d)
    x_s2 = x_sample.reshape(rows_s, d)
    xa_s, act_s, w_in_b = _inproj(x_s2, g_mix, w_in[0], cos_s, sin_s, rows_s + N_META, 512, 1,
                                  tail=meta_tokens.astype(x_sample.dtype))
    bm_p = 1024
    n_i, n_j = rows_p // bm_p, 8
    side_map = lambda i, j: (i, jnp.minimum(j, n_j - 1))
    later = (w_up, w_down, p_a, p_b, w_out)
    sides = tuple((wt[0], (wt.shape[1] // n_i, wt.shape[2] // n_j), side_map) for wt in later)
    xa_p, act_p, wu_b, wd_b, pa_b, pb_b, wo_b = _inproj(x_p2, g_mix, w_in_b, cos_p, sin_p, bm_p, 1024, t_p // bm_p,
                                                  side=sides)


    zc = jnp.zeros((1, CONV_W - 1, w), F32)
    zh = jnp.zeros((1, 1, w), F32)
    zs = jnp.zeros((1, RET_HEADS, HEAD_DIM, HEAD_DIM), F32)
    conv_m, h_m = _lru_seq(xa_s, act_s, rows_s, 1, N_META, N_META, zc, zh, lw, emit_y=False, first_pos_zero=True)
    (s_m,) = _retention(act_s, rows_s, 1, N_META, 1, N_META, 1, zs, ret_g, emit_y=False)

    ya_p, conv_p, h_p = _lru_seq(xa_p, act_p, 0, nb_p, t_p, 512, conv_m, h_m, lw, emit_y=True, first_pos_zero=False)
    yb_p, s_p = _retention(act_p, 0, nb_p, t_p, 1, CHUNK, 4, s_m, ret_g, emit_y=True)

    conv0_t = jnp.transpose(state_conv[0], (1, 0, 2))
    ya_s, conv_s_t, h_s = _lru_step(xa_s, act_s, nb_s, t_s, conv0_t, state_lru[0], lw)
    yb_s, s_s = _retention(act_s, 0, nb_s, t_s, 8, t_s, 1, state_ret[0], ret_g, emit_y=True)

    x1_p = _outproj(x_p2, ya_p, yb_p, act_p, pa_b, pb_b, wo_b, 512)
    x1_s = _outproj(x_s2, ya_s, yb_s, act_s, pa_b, pb_b, wo_b, 512)
    y_p = _ffn(x1_p, g_ffn, wu_b, wd_b, g_f, 1024, 1024)
    y_s = _ffn(x1_s, g_ffn, wu_b, wd_b, g_f, 1024, 1024)

    return (y_p.reshape(nb_p, t_p, d), y_s.reshape(nb_s, t_s, d),
            conv_p[None], h_p.reshape(1, nb_p, w), s_p[None],
            jnp.transpose(conv_s_t, (1, 0, 2))[None], h_s[None], s_s[None])
```

```python
import functools
import math

import jax
import jax.numpy as jnp
from jax import lax
from jax.experimental import pallas as pl
from jax.experimental.pallas import tpu as pltpu

F32 = jnp.float32
BF16 = jnp.bfloat16

N_META = 16
PAST_LEN = 16384
LRU_BLOCKS = 16
CONV_W = 4
LRU_C = 8.0
RET_HEADS = 8
HEAD_DIM = 128
CHUNK = 128
ROPE_BASE = 10000.0
EPS = 1e-6

LANES = 128
VMEM_LIMIT = 56 << 20


def _cparams(n_axes):
    return pltpu.CompilerParams(dimension_semantics=("arbitrary",) * n_axes, vmem_limit_bytes=VMEM_LIMIT)


def _rmsnorm(x, g):
    return x * lax.rsqrt(jnp.mean(x * x, axis=-1, keepdims=True) + EPS) * g


def _sigmoid(x):
    return 0.5 * jnp.tanh(0.5 * x) + 0.5


def _row_chunks(rows, size):
    n = max(rows // size, 1)
    return [(c * size, size if c < n - 1 else rows - c * size) for c in range(n)]


_COL_GATE, _COL_Q, _COL_K, _COL_V, _COL_G, _COL_GATE_A = 1, 2, 3, 4, 5, 6
_ACT_GELU, _ACT_Q, _ACT_K, _ACT_V, _ACT_SGA, _ACT_SGB, _ACT_SILU = 0, 1, 2, 3, 4, 6, 8


def _act_block(kind):
    return jnp.where(kind <= _COL_V, kind - 1, jnp.where(kind == _COL_G, _ACT_SILU, kind - 2))


def _inproj_kernel(x_ref, g_ref, w_ref, cos_ref, sin_ref, *refs, nper, cast_w, n_side, has_tail):
    if has_tail:
        tail_ref, refs = refs[0], refs[1:]
    side_src, refs = refs[:n_side], refs[n_side:]
    xa_ref, act_ref = refs[0], refs[1]
    refs = refs[2:]
    if cast_w:
        wb_ref, refs = refs[0], refs[1:]
    side_dst, (xn_ref,) = refs[:n_side], refs[n_side:]
    j = pl.program_id(1)
    kind = j // nper
    bn = xa_ref.shape[1]

    for src, dst in zip(side_src, side_dst):
        dst[...] = src[...].astype(BF16)
    if cast_w:
        wb_ref[...] = w_ref[...].astype(BF16)
        w_bf = wb_ref
    else:
        w_bf = w_ref

    def project(epilogue, normalize=False):
        if normalize:
            nx = x_ref.shape[0]
            xn_ref[0:nx, :] = _rmsnorm(x_ref[...], g_ref[...]).astype(BF16)
            if has_tail:
                xn_ref[nx:, :] = _rmsnorm(tail_ref[...], g_ref[...]).astype(BF16)
        epilogue(jnp.dot(xn_ref[...], w_bf[...], preferred_element_type=F32))

    def to_xa(acc):
        xa_ref[...] = acc

    def rope(scale):
        def epilogue(acc):
            cos, sin = cos_ref[...], sin_ref[...]
            for h in range(bn // HEAD_DIM):
                a = acc[:, h * HEAD_DIM:(h + 1) * HEAD_DIM]
                r = a * cos + pltpu.roll(a, HEAD_DIM // 2, axis=1) * sin
                if scale is not None:
                    r = r * scale
                act_ref[:, h * HEAD_DIM:(h + 1) * HEAD_DIM] = r.astype(BF16)
        return epilogue

    def elementwise(fn):
        def epilogue(acc):
            act_ref[...] = fn(acc).astype(BF16)
        return epilogue

    pl.when(j == 0)(lambda: project(to_xa, normalize=True))
    if nper > 1:
        pl.when((j > 0) & (kind == 0))(lambda: project(to_xa))
    pl.when(kind == _COL_GATE)(lambda: project(elementwise(jax.nn.gelu)))
    pl.when(kind == _COL_Q)(lambda: project(rope(None)))
    pl.when(kind == _COL_K)(lambda: project(rope(HEAD_DIM ** -0.5)))
    pl.when(kind == _COL_V)(lambda: project(elementwise(lambda a: a)))
    pl.when(kind == _COL_G)(lambda: project(elementwise(lambda a: a * _sigmoid(a))))
    pl.when(kind >= _COL_GATE_A)(lambda: project(elementwise(_sigmoid)))


def _inproj(x, g, w, cos, sin, bm, bn, table_blocks, side=(), tail=None):
    xrows, d = x.shape
    m = xrows if tail is None else xrows + tail.shape[0]
    assert tail is None or m == bm
    n = w.shape[1]
    width = 1024
    nper = width // bn
    cast_w = w.dtype != BF16

    def act_map(i, j):
        kind = j // nper
        return i, jnp.where(kind == 0, 0, _act_block(kind) * nper + j % nper)

    in_specs = [
        pl.BlockSpec((min(bm, xrows), d), lambda i, j: (i, 0)),
        pl.BlockSpec((1, d), lambda i, j: (0, 0)),
        pl.BlockSpec((d, bn), lambda i, j: (0, j)),
        pl.BlockSpec((bm, LANES), lambda i, j: (i % table_blocks, 0)),
        pl.BlockSpec((bm, LANES), lambda i, j: (i % table_blocks, 0)),
    ]
    if tail is not None:
        in_specs.append(pl.BlockSpec(tail.shape, lambda i, j: (0, 0)))
    out_specs = [
        pl.BlockSpec((bm, bn), lambda i, j: (i, jnp.minimum(j, nper - 1))),
        pl.BlockSpec((bm, bn), act_map),
    ]
    out_shape = [jax.ShapeDtypeStruct((m, width), F32), jax.ShapeDtypeStruct((m, n - width), BF16)]
    if cast_w:
        out_specs.append(pl.BlockSpec((d, bn), lambda i, j: (0, j)))
        out_shape.append(jax.ShapeDtypeStruct(w.shape, BF16))
    for arr, block, index_map in side:
        in_specs.append(pl.BlockSpec(block, index_map))
        out_specs.append(pl.BlockSpec(block, index_map))
        out_shape.append(jax.ShapeDtypeStruct(arr.shape, BF16))
    return pl.pallas_call(
        functools.partial(_inproj_kernel, nper=nper, cast_w=cast_w, n_side=len(side), has_tail=tail is not None),
        name="inproj",
        grid=(m // bm, n // bn),
        in_specs=in_specs,
        out_specs=out_specs,
        out_shape=out_shape,
        scratch_shapes=[pltpu.VMEM((bm, d), BF16)],
        compiler_params=_cparams(2),
    )(x, g, w, cos, sin, *([] if tail is None else [tail]), *[arr for arr, _, _ in side])


def _lru_coeffs(xc, wbd_ref, bg_ref, cl_row, group):
    gates = jnp.dot(xc.astype(BF16), wbd_ref[group], preferred_element_type=F32) + bg_ref[group]
    rec = _sigmoid(gates[:, :LANES])
    ing = _sigmoid(gates[:, LANES:])
    log_a = rec * cl_row
    a = jnp.exp(log_a)
    mult = jnp.sqrt(jnp.tanh(-log_a) * (a * a + 1.0))
    return a, mult, ing


def _lru_seq_kernel(xa_ref, ga_ref, conv0_ref, h0_ref, cw_ref, cb_ref, wbd_ref, bg_ref, lam_ref,
                    *refs, tr, emit_y, first_pos_zero):
    if emit_y:
        ya_ref, convout_ref, hout_ref, halo_ref, h_ref, a_ref, b_ref = refs
    else:
        convout_ref, hout_ref, halo_ref, h_ref, a_ref, b_ref = refs
    t = pl.program_id(1)
    halo = 8

    @pl.when(t == 0)
    def _():
        halo_ref[0:halo - 3, :] = jnp.zeros((halo - 3, halo_ref.shape[1]), F32)
        halo_ref[halo - 3:halo, :] = conv0_ref[0]
        h_ref[...] = h0_ref[0]

    xa = xa_ref[...]
    cw = [cw_ref[CONV_W - 1 - k:CONV_W - k, :] for k in range(CONV_W)]
    xc = cb_ref[...] + cw[0] * xa + sum(cw[k] * pltpu.roll(xa, k, axis=0) for k in range(1, CONV_W))
    window = jnp.concatenate([halo_ref[0:halo, :], xa[0:halo, :]], axis=0)
    first = cb_ref[...] + sum(cw[k] * window[halo - k:2 * halo - k, :] for k in range(CONV_W))
    xc = jnp.concatenate([first, xc[halo:, :]], axis=0)
    halo_ref[0:halo, :] = xa[tr - halo:, :]

    cl = LRU_C * jax.nn.log_sigmoid(lam_ref[...])
    for g in range(xc.shape[1] // LANES):
        sl = slice(g * LANES, (g + 1) * LANES)
        a, mult, ing = _lru_coeffs(xc[:, sl], wbd_ref, bg_ref, cl[:, sl], g)
        if first_pos_zero:
            row = lax.broadcasted_iota(jnp.int32, a.shape, 0) + t * tr
            mult = jnp.where(row == 0, 1.0, mult)
        a_ref[:, sl] = a
        b_ref[:, sl] = mult * ing * xc[:, sl]

    h = h_ref[...]
    for r in range(tr):
        h = a_ref[r:r + 1, :] * h + b_ref[r:r + 1, :]
        halo_ref[halo + r:halo + r + 1, :] = h
    h_ref[...] = h
    if emit_y:
        ya_ref[...] = (halo_ref[halo:halo + tr, :] * ga_ref[...].astype(F32)).astype(BF16)

    @pl.when(t == pl.num_programs(1) - 1)
    def _():
        convout_ref[0] = halo_ref[halo - 3:halo, :]
        hout_ref[0] = h_ref[...]


def _lru_seq(xa, act, row0, nseq, seqlen, tr, conv0, h0, lw, emit_y, first_pos_zero):
    w = xa.shape[1]
    nt = seqlen // tr
    rb0 = row0 // tr
    ng = w // LANES
    state_map = (lambda b, t: (b, 0, 0)) if conv0.shape[0] == nseq else (lambda b, t: (0, 0, 0))
    const2 = lambda b, t: (0, 0)
    const3 = lambda b, t: (0, 0, 0)
    out_specs = [pl.BlockSpec((1, CONV_W - 1, w), lambda b, t: (b, 0, 0)),
                 pl.BlockSpec((1, 1, w), lambda b, t: (b, 0, 0))]
    out_shape = [jax.ShapeDtypeStruct((nseq, CONV_W - 1, w), F32), jax.ShapeDtypeStruct((nseq, 1, w), F32)]
    if emit_y:
        out_specs = [pl.BlockSpec((tr, w), lambda b, t: (b * nt + t, 0))] + out_specs
        out_shape = [jax.ShapeDtypeStruct((nseq * seqlen, w), BF16)] + out_shape
    return pl.pallas_call(
        functools.partial(_lru_seq_kernel, tr=tr, emit_y=emit_y, first_pos_zero=first_pos_zero),
        name="lru_seq",
        grid=(nseq, nt),
        in_specs=[
            pl.BlockSpec((tr, w), lambda b, t: (rb0 + b * nt + t, 0)),
            pl.BlockSpec((tr, w), lambda b, t: (rb0 + b * nt + t, _ACT_GELU)),
            pl.BlockSpec((1, CONV_W - 1, w), state_map),
            pl.BlockSpec((1, 1, w), state_map),
            pl.BlockSpec((CONV_W, w), const2),
            pl.BlockSpec((1, w), const2),
            pl.BlockSpec((ng, LANES, 2 * LANES), const3),
            pl.BlockSpec((ng, 1, 2 * LANES), const3),
            pl.BlockSpec((1, w), const2),
        ],
        out_specs=out_specs,
        out_shape=out_shape,
        scratch_shapes=[pltpu.VMEM((8 + tr, w), F32), pltpu.VMEM((1, w), F32),
                        pltpu.VMEM((tr, w), F32), pltpu.VMEM((tr, w), F32)],
        compiler_params=_cparams(2),
    )(xa, act, conv0, h0, lw["conv_w"], lw["conv_b"], lw["wbd"], lw["bg"], lw["lam"])


def _lru_step_kernel(xa_ref, ga_ref, conv0_ref, h0_ref, cw_ref, cb_ref, wbd_ref, bg_ref, lam_ref,
                     ya_ref, convout_ref, hout_ref, hs_ref, *, nseq, seqlen):
    cl = LRU_C * jax.nn.log_sigmoid(lam_ref[...])
    full = [conv0_ref[j] for j in range(CONV_W - 1)]
    full += [xa_ref[pl.ds(t, nseq, stride=seqlen), :] for t in range(seqlen)]
    h = h0_ref[...]
    for t in range(seqlen):
        xc = cb_ref[...] + cw_ref[0:1, :] * full[t]
        for j in range(1, CONV_W):
            xc = xc + cw_ref[j:j + 1, :] * full[t + j]
        a, mult, ing = _lru_coeffs(xc, wbd_ref, bg_ref, cl, 0)
        h = a * h + mult * ing * xc
        hs_ref[pl.ds(t, nseq, stride=seqlen), :] = h
    ya_ref[...] = (hs_ref[...] * ga_ref[...].astype(F32)).astype(BF16)
    for j in range(CONV_W - 1):
        convout_ref[j] = full[seqlen + j]
    hout_ref[...] = h


def _lru_step(xa, act, nseq, seqlen, conv0_t, h0, lw):
    w = xa.shape[1]
    rows = nseq * seqlen
    ng = w // LANES
    col = lambda g: (0, g)
    return pl.pallas_call(
        functools.partial(_lru_step_kernel, nseq=nseq, seqlen=seqlen),
        name="lru_step",
        grid=(ng,),
        in_specs=[
            pl.BlockSpec((rows, LANES), col),
            pl.BlockSpec((rows, LANES), col),
            pl.BlockSpec((CONV_W - 1, nseq, LANES), lambda g: (0, 0, g)),
            pl.BlockSpec((nseq, LANES), col),
            pl.BlockSpec((CONV_W, LANES), col),
            pl.BlockSpec((1, LANES), col),
            pl.BlockSpec((1, LANES, 2 * LANES), lambda g: (g, 0, 0)),
            pl.BlockSpec((1, 1, 2 * LANES), lambda g: (g, 0, 0)),
            pl.BlockSpec((1, LANES), col),
        ],
        out_specs=[
            pl.BlockSpec((rows, LANES), col),
            pl.BlockSpec((CONV_W - 1, nseq, LANES), lambda g: (0, 0, g)),
            pl.BlockSpec((nseq, LANES), col),
        ],
        out_shape=[jax.ShapeDtypeStruct((rows, w), BF16),
                   jax.ShapeDtypeStruct((CONV_W - 1, nseq, w), F32),
                   jax.ShapeDtypeStruct((nseq, w), F32)],
        scratch_shapes=[pltpu.VMEM((rows, LANES), F32)],
        compiler_params=_cparams(1),
    )(xa, act, conv0_t, h0, lw["conv_w"], lw["conv_b"], lw["wbd"], lw["bg"], lw["lam"])


def _log_gamma(h):
    return math.log1p(-(2.0 ** (-5.0 - h)))


def _ret_kernel(q_ref, k_ref, v_ref, sg_ref, s0_ref, ng_ref, *refs, nb, clen, cps, emit_y):
    if emit_y:
        y_ref, sout_ref, s_ref, dec_ref, rdec_ref = refs
    else:
        sout_ref, s_ref, dec_ref, rdec_ref = refs
    c = pl.program_id(1)
    rows = nb * clen
    heads = range(RET_HEADS)
    chunks = range(cps)

    @pl.when((pl.program_id(0) == 0) & (c == 0))
    def _():
        li = lax.broadcasted_iota(jnp.int32, (rows, rows), 0)
        mi = lax.broadcasted_iota(jnp.int32, (rows, rows), 1)
        keep = (li // clen == mi // clen) & (li >= mi)
        diff = jnp.where(keep, li - mi, 0).astype(F32)
        tpos = (lax.broadcasted_iota(jnp.int32, (rows, LANES), 0) % clen).astype(F32)
        for h in heads:
            lg = _log_gamma(h)
            dec_ref[h] = jnp.where(keep, jnp.exp(lg * diff), 0.0)
            rdec_ref[0, h] = jnp.exp(lg * (tpos + 1.0))
            rdec_ref[1, h] = jnp.exp(lg * (clen - 1.0 - tpos))

    @pl.when(c == 0)
    def _():
        for n in range(nb):
            s_ref[n] = s0_ref[n if s0_ref.shape[0] == nb else 0]

    if nb > 1:
        seq_of_row = lax.broadcasted_iota(jnp.int32, (rows, nb * HEAD_DIM), 0) // clen
        seq_of_col = lax.broadcasted_iota(jnp.int32, (rows, nb * HEAD_DIM), 1) // HEAD_DIM
        own = seq_of_row == seq_of_col
    tn = (((0,), (0,)), ((), ()))
    nt = (((1,), (1,)), ((), ()))
    hsl = [slice(h * HEAD_DIM, (h + 1) * HEAD_DIM) for h in heads]
    rsl = [slice(ci * rows, (ci + 1) * rows) for ci in chunks]
    q = [[q_ref[rsl[ci], sl] for sl in hsl] for ci in chunks]
    k = [[k_ref[rsl[ci], sl] for sl in hsl] for ci in chunks]
    v = [[v_ref[rsl[ci], sl] for sl in hsl] for ci in chunks]

    if emit_y:
        scores = [[lax.dot_general(q[ci][h], k[ci][h], nt, preferred_element_type=F32) for h in heads]
                  for ci in chunks]
    upd = []
    for ci in chunks:
        upd.append([])
        for h in heads:
            kd = (k[ci][h].astype(F32) * rdec_ref[1, h]).astype(BF16)
            if nb > 1:
                v_bd = jnp.where(own, jnp.concatenate([v[ci][h].astype(F32)] * nb, axis=1), 0.0).astype(BF16)
            else:
                v_bd = v[ci][h]
            upd[ci].append(lax.dot_general(kd, v_bd, tn, preferred_element_type=F32))

    state = [[[s_ref[n, h] for n in range(nb)] for h in heads]]
    for ci in chunks:
        state.append([[math.exp(clen * _log_gamma(h)) * state[ci][h][n]
                       + upd[ci][h][:, n * HEAD_DIM:(n + 1) * HEAD_DIM] for n in range(nb)] for h in heads])
    for h in heads:
        for n in range(nb):
            s_ref[n, h] = state[cps][h][n]

    if emit_y:
        inter = []
        for ci in chunks:
            inter.append([])
            for h in heads:
                s_cat = jnp.concatenate(state[ci][h], axis=1) if nb > 1 else state[ci][h][0]
                qs = jnp.dot(q[ci][h], s_cat.astype(BF16), preferred_element_type=F32)
                if nb > 1:
                    qs = jnp.where(own, qs, 0.0)
                    qs = sum(qs[:, n * HEAD_DIM:(n + 1) * HEAD_DIM] for n in range(nb))
                inter[ci].append(qs)
        for ci in chunks:
            for h in heads:
                p = (scores[ci][h] * dec_ref[h]).astype(BF16)
                o = jnp.dot(p, v[ci][h], preferred_element_type=F32) + inter[ci][h] * rdec_ref[0, h]
                mu = jnp.mean(o, axis=-1, keepdims=True)
                dev = o - mu
                var = jnp.mean(dev * dev, axis=-1, keepdims=True)
                normed = dev * lax.rsqrt(var + EPS) * ng_ref[:, hsl[h]]
                y_ref[rsl[ci], hsl[h]] = (sg_ref[rsl[ci], hsl[h]].astype(F32) * normed).astype(BF16)

    @pl.when(c == pl.num_programs(1) - 1)
    def _():
        sout_ref[...] = s_ref[...]


def _retention(act, row0, nseq, seqlen, nb, clen, cps, s0, norm_g, emit_y):
    w = RET_HEADS * HEAD_DIM
    rows = nb * clen * cps
    nc = seqlen // (clen * cps)
    assert nb == 1 or (nc == 1 and cps == 1)
    rb0 = row0 // rows
    rmap = lambda col: (lambda b, c: (rb0 + b * nc + c, col))
    shared = s0.shape[0] != nseq
    s_map = (lambda b, c: (0, 0, 0, 0)) if shared else (lambda b, c: (b, 0, 0, 0))
    s_block = (1 if shared else nb, RET_HEADS, HEAD_DIM, HEAD_DIM)
    out_specs = [pl.BlockSpec((nb, RET_HEADS, HEAD_DIM, HEAD_DIM), lambda b, c: (b, 0, 0, 0))]
    out_shape = [jax.ShapeDtypeStruct((nseq, RET_HEADS, HEAD_DIM, HEAD_DIM), F32)]
    if emit_y:
        out_specs = [pl.BlockSpec((rows, w), lambda b, c: (b * nc + c, 0))] + out_specs
        out_shape = [jax.ShapeDtypeStruct((nseq * seqlen, w), BF16)] + out_shape
    return pl.pallas_call(
        functools.partial(_ret_kernel, nb=nb, clen=clen, cps=cps, emit_y=emit_y),
        name="retention",
        grid=(nseq // nb, nc),
        in_specs=[
            pl.BlockSpec((rows, w), rmap(_ACT_Q)),
            pl.BlockSpec((rows, w), rmap(_ACT_K)),
            pl.BlockSpec((rows, w), rmap(_ACT_V)),
            pl.BlockSpec((rows, w), rmap(_ACT_SILU)),
            pl.BlockSpec(s_block, s_map),
            pl.BlockSpec((1, w), lambda b, c: (0, 0)),
        ],
        out_specs=out_specs,
        out_shape=out_shape,
        scratch_shapes=[pltpu.VMEM((nb, RET_HEADS, HEAD_DIM, HEAD_DIM), F32),
                        pltpu.VMEM((RET_HEADS, nb * clen, nb * clen), F32),
                        pltpu.VMEM((2, RET_HEADS, nb * clen, LANES), F32)],
        compiler_params=_cparams(2),
    )(act, act, act, act, s0, norm_g)


def _outproj_kernel(x_ref, ya_ref, yb_ref, sga_ref, sgb_ref, pa_ref, pb_ref, wo_ref, o_ref):
    for r0, nr in _row_chunks(o_ref.shape[0], 256):
        rows = slice(r0, r0 + nr)
        ma = jnp.dot(ya_ref[rows, :], pa_ref[...], preferred_element_type=F32)
        mb = jnp.dot(yb_ref[rows, :], pb_ref[...], preferred_element_type=F32)
        merged = sga_ref[rows, :].astype(F32) * ma + sgb_ref[rows, :].astype(F32) * mb
        o_ref[rows, :] = x_ref[rows, :] + jnp.dot(merged.astype(BF16), wo_ref[...], preferred_element_type=F32)


def _outproj(x, ya, yb, act, pa, pb, wo, bm):
    m, d = x.shape
    w = ya.shape[1]
    nd = d // 1024
    row = lambda i: (i, 0)
    const = lambda i: (0, 0)
    single = pl.Buffered(1)
    return pl.pallas_call(
        _outproj_kernel,
        name="outproj",
        grid=(m // bm,),
        in_specs=[
            pl.BlockSpec((bm, d), row),
            pl.BlockSpec((bm, w), row),
            pl.BlockSpec((bm, w), row),
            pl.BlockSpec((bm, d), lambda i: (i, _ACT_SGA // nd)),
            pl.BlockSpec((bm, d), lambda i: (i, _ACT_SGB // nd)),
            pl.BlockSpec((w, d), const, pipeline_mode=single),
            pl.BlockSpec((w, d), const, pipeline_mode=single),
            pl.BlockSpec((d, d), const, pipeline_mode=single),
        ],
        out_specs=pl.BlockSpec((bm, d), row),
        out_shape=jax.ShapeDtypeStruct((m, d), F32),
        compiler_params=_cparams(1),
    )(x, ya, yb, act, act, pa, pb, wo)


def _ffn_kernel(x_ref, g_ref, wu_ref, wd_ref, gf_ref, o_ref, hn_ref):
    j = pl.program_id(1)
    last = pl.num_programs(1) - 1

    def block(first, final):
        for r0, nr in _row_chunks(o_ref.shape[0], 512):
            rows = slice(r0, r0 + nr)
            if first:
                base = x_ref[rows, :]
                hn = _rmsnorm(base, g_ref[...]).astype(BF16)
                hn_ref[rows, :] = hn
            else:
                base = o_ref[rows, :]
                hn = hn_ref[rows, :]
            u = jnp.dot(hn, wu_ref[...], preferred_element_type=F32)
            r = jnp.square(jnp.maximum(u, 0.0)).astype(BF16)
            acc = base + jnp.dot(r, wd_ref[...], preferred_element_type=F32)
            o_ref[rows, :] = _rmsnorm(acc, gf_ref[...]) if final else acc

    pl.when(j == 0)(lambda: block(True, False))
    pl.when((j > 0) & (j < last))(lambda: block(False, False))
    pl.when(j == last)(lambda: block(False, True))


def _ffn(x, g, wu, wd, gf, bm, bf):
    m, d = x.shape
    dff = wu.shape[1]
    assert dff // bf >= 2
    return pl.pallas_call(
        _ffn_kernel,
        name="ffn",
        grid=(m // bm, dff // bf),
        in_specs=[
            pl.BlockSpec((bm, d), lambda i, j: (i, 0)),
            pl.BlockSpec((1, d), lambda i, j: (0, 0)),
            pl.BlockSpec((d, bf), lambda i, j: (0, j)),
            pl.BlockSpec((bf, d), lambda i, j: (j, 0)),
            pl.BlockSpec((1, d), lambda i, j: (0, 0)),
        ],
        out_specs=pl.BlockSpec((bm, d), lambda i, j: (i, 0)),
        out_shape=jax.ShapeDtypeStruct((m, d), F32),
        scratch_shapes=[pltpu.VMEM((bm, d), BF16)],
        compiler_params=_cparams(2),
    )(x, g, wu, wd, gf)


def _rope_tables(pos):
    inv = ROPE_BASE ** (-jnp.arange(0, HEAD_DIM, 2, dtype=F32) / HEAD_DIM)
    ang = pos.astype(F32)[:, None] * inv[None, :]
    cos, sin = jnp.cos(ang), jnp.sin(ang)
    return jnp.concatenate([cos, cos], axis=1), jnp.concatenate([-sin, sin], axis=1)


def _gate_weights(wa, wx, ba, bx):
    nblk, blk, _ = wa.shape
    per = LANES // blk
    ng = nblk // per

    def bd(wt):
        wt = wt.reshape(ng, per, blk, blk)
        eye = jnp.eye(per, dtype=wt.dtype)
        return jnp.einsum("gpcd,pq->gpcqd", wt, eye).reshape(ng, LANES, LANES)

    wbd = jnp.concatenate([bd(wa), bd(wx)], axis=2).astype(BF16)
    bg = jnp.concatenate([ba.reshape(ng, 1, LANES), bx.reshape(ng, 1, LANES)], axis=2)
    return wbd, bg


def kernel(x_prompt, x_sample, state_conv, state_lru, state_ret, meta_tokens, norm_mix_g, w_in, conv_w,
           conv_b, lru_wa, lru_ba, lru_wx, lru_bx, lru_lam, ret_norm_g, p_a, p_b, w_out, norm_ffn_g,
           w_up, w_down, norm_f_g):
    assert w_in.shape[0] == 1
    nb_p, t_p, d = x_prompt.shape
    nb_s, t_s, _ = x_sample.shape
    w = conv_w.shape[-1]
    rows_p, rows_s = nb_p * t_p, nb_s * t_s

    g_mix, g_ffn, g_f = norm_mix_g[0][None], norm_ffn_g[0][None], norm_f_g[None]
    wbd, bg = _gate_weights(lru_wa[0], lru_wx[0], lru_ba[0], lru_bx[0])
    lw = dict(conv_w=conv_w[0], conv_b=conv_b[0][None], wbd=wbd, bg=bg, lam=lru_lam[0][None])
    ret_g = ret_norm_g[0][None]

    cos_p, sin_p = _rope_tables(N_META + jnp.arange(t_p, dtype=jnp.int32))
    pos_sm = jnp.concatenate([jnp.tile(PAST_LEN + jnp.arange(t_s, dtype=jnp.int32), nb_s),
                              jnp.arange(N_META, dtype=jnp.int32)])
    cos_s, sin_s = _rope_tables(pos_sm)
    x_p2 = x_prompt.reshape(rows_p, d)
    x_s2 = x_sample.reshape(rows_s, d)
    xa_s, act_s, w_in_b = _inproj(x_s2, g_mix, w_in[0], cos_s, sin_s, rows_s + N_META, 512, 1,
                                  tail=meta_tokens.astype(x_sample.dtype))
    bm_p = 1024
    n_i, n_j = rows_p // bm_p, 8
    side_map = lambda i, j: (i, jnp.minimum(j, n_j - 1))
    later = (w_up, w_down, p_a, p_b, w_out)
    sides = tuple((wt[0], (wt.shape[1] // n_i, wt.shape[2] // n_j), side_map) for wt in later)
    xa_p, act_p, wu_b, wd_b, pa_b, pb_b, wo_b = _inproj(x_p2, g_mix, w_in_b, cos_p, sin_p, bm_p, 1024, t_p // bm_p,
                                                        side=sides)


    zc = jnp.zeros((1, CONV_W - 1, w), F32)
    zh = jnp.zeros((1, 1, w), F32)
    zs = jnp.zeros((1, RET_HEADS, HEAD_DIM, HEAD_DIM), F32)
    conv_m, h_m = _lru_seq(xa_s, act_s, rows_s, 1, N_META, N_META, zc, zh, lw, emit_y=False, first_pos_zero=True)
    (s_m,) = _retention(act_s, rows_s, 1, N_META, 1, N_META, 1, zs, ret_g, emit_y=False)

    ya_p, conv_p, h_p = _lru_seq(xa_p, act_p, 0, nb_p, t_p, 512, conv_m, h_m, lw, emit_y=True, first_pos_zero=False)
    yb_p, s_p = _retention(act_p, 0, nb_p, t_p, 1, CHUNK, 4, s_m, ret_g, emit_y=True)

    conv0_t = jnp.transpose(state_conv[0], (1, 0, 2))
    ya_s, conv_s_t, h_s = _lru_step(xa_s, act_s, nb_s, t_s, conv0_t, state_lru[0], lw)
    yb_s, s_s = _retention(act_s, 0, nb_s, t_s, 16, t_s, 1, state_ret[0], ret_g, emit_y=True)

    x1_p = _outproj(x_p2, ya_p, yb_p, act_p, pa_b, pb_b, wo_b, 512)
    x1_s = _outproj(x_s2, ya_s, yb_s, act_s, pa_b, pb_b, wo_b, 512)
    y_p = _ffn(x1_p, g_ffn, wu_b, wd_b, g_f, 1024, 1024)
    y_s = _ffn(x1_s, g_ffn, wu_b, wd_b, g_f, 1024, 1024)

    return (y_p.reshape(nb_p, t_p, d), y_s.reshape(nb_s, t_s, d),
            conv_p[None], h_p.reshape(1, nb_p, w), s_p[None],
            jnp.transpose(conv_s_t, (1, 0, 2))[None], h_s[None], s_s[None])
```

```python
import functools
import math

import jax
import jax.numpy as jnp
from jax import lax
from jax.experimental import pallas as pl
from jax.experimental.pallas import tpu as pltpu

F32 = jnp.float32
BF16 = jnp.bfloat16

N_META = 16
PAST_LEN = 16384
LRU_BLOCKS = 16
CONV_W = 4
LRU_C = 8.0
RET_HEADS = 8
HEAD_DIM = 128
CHUNK = 128
ROPE_BASE = 10000.0
EPS = 1e-6

LANES = 128
VMEM_LIMIT = 56 << 20


def _cparams(n_axes):
    return pltpu.CompilerParams(dimension_semantics=("arbitrary",) * n_axes, vmem_limit_bytes=VMEM_LIMIT)


def _call(spec):
    return pl.pallas_call(
        spec["kernel"], name=spec["name"], grid=spec["grid"], in_specs=spec["in_specs"],
        out_specs=spec["out_specs"], out_shape=spec["out_shape"], scratch_shapes=spec["scratch_shapes"],
        compiler_params=_cparams(len(spec["grid"])))(*spec["args"])


def _call_pair(a, b, name):
    n0, n1 = a["grid"]
    assert b["grid"] == (n0 * n1, 1)

    def remap(spec):
        return pl.BlockSpec(spec.block_shape, lambda i, j, m=spec.index_map: m(i * n1 + j, 0))

    counts = [len(a["in_specs"]), len(b["in_specs"]), len(a["out_specs"]), len(b["out_specs"]),
              len(a["scratch_shapes"]), len(b["scratch_shapes"])]

    def kernel(*refs):
        groups, pos = [], 0
        for n in counts:
            groups.append(refs[pos:pos + n])
            pos += n
        a["kernel"](*groups[0], *groups[2], *groups[4])
        b["kernel"](*groups[1], *groups[3], *groups[5])

    outs = pl.pallas_call(
        kernel, name=name, grid=a["grid"],
        in_specs=list(a["in_specs"]) + [remap(sp) for sp in b["in_specs"]],
        out_specs=list(a["out_specs"]) + [remap(sp) for sp in b["out_specs"]],
        out_shape=list(a["out_shape"]) + list(b["out_shape"]),
        scratch_shapes=list(a["scratch_shapes"]) + list(b["scratch_shapes"]),
        compiler_params=_cparams(2))(*a["args"], *b["args"])
    return outs[:counts[2]], outs[counts[2]:]


def _rmsnorm(x, g):
    return x * lax.rsqrt(jnp.mean(x * x, axis=-1, keepdims=True) + EPS) * g


def _sigmoid(x):
    return 0.5 * jnp.tanh(0.5 * x) + 0.5


def _row_chunks(rows, size):
    n = max(rows // size, 1)
    return [(c * size, size if c < n - 1 else rows - c * size) for c in range(n)]


_COL_GATE, _COL_Q, _COL_K, _COL_V, _COL_G, _COL_GATE_A = 1, 2, 3, 4, 5, 6
_ACT_GELU, _ACT_Q, _ACT_K, _ACT_V, _ACT_SGA, _ACT_SGB, _ACT_SILU = 0, 1, 2, 3, 4, 6, 8


def _act_block(kind):
    return jnp.where(kind <= _COL_V, kind - 1, jnp.where(kind == _COL_G, _ACT_SILU, kind - 2))


def _inproj_kernel(x_ref, g_ref, w_ref, cos_ref, sin_ref, *refs, nper, cast_w, n_side, has_tail):
    if has_tail:
        tail_ref, refs = refs[0], refs[1:]
    side_src, refs = refs[:n_side], refs[n_side:]
    xa_ref, act_ref = refs[0], refs[1]
    refs = refs[2:]
    if cast_w:
        wb_ref, refs = refs[0], refs[1:]
    side_dst, (xn_ref,) = refs[:n_side], refs[n_side:]
    j = pl.program_id(1)
    kind = j // nper
    bn = xa_ref.shape[1]

    for src, dst in zip(side_src, side_dst):
        dst[...] = src[...].astype(BF16)
    if cast_w:
        wb_ref[...] = w_ref[...].astype(BF16)
        w_bf = wb_ref
    else:
        w_bf = w_ref

    def project(epilogue, normalize=False):
        if normalize:
            nx = x_ref.shape[0]
            xn_ref[0:nx, :] = _rmsnorm(x_ref[...], g_ref[...]).astype(BF16)
            if has_tail:
                xn_ref[nx:, :] = _rmsnorm(tail_ref[...], g_ref[...]).astype(BF16)
        epilogue(jnp.dot(xn_ref[...], w_bf[...], preferred_element_type=F32))

    def to_xa(acc):
        xa_ref[...] = acc

    def rope(scale):
        def epilogue(acc):
            cos, sin = cos_ref[...], sin_ref[...]
            for h in range(bn // HEAD_DIM):
                a = acc[:, h * HEAD_DIM:(h + 1) * HEAD_DIM]
                r = a * cos + pltpu.roll(a, HEAD_DIM // 2, axis=1) * sin
                if scale is not None:
                    r = r * scale
                act_ref[:, h * HEAD_DIM:(h + 1) * HEAD_DIM] = r.astype(BF16)
        return epilogue

    def elementwise(fn):
        def epilogue(acc):
            act_ref[...] = fn(acc).astype(BF16)
        return epilogue

    pl.when(j == 0)(lambda: project(to_xa, normalize=True))
    if nper > 1:
        pl.when((j > 0) & (kind == 0))(lambda: project(to_xa))
    pl.when(kind == _COL_GATE)(lambda: project(elementwise(jax.nn.gelu)))
    pl.when(kind == _COL_Q)(lambda: project(rope(None)))
    pl.when(kind == _COL_K)(lambda: project(rope(HEAD_DIM ** -0.5)))
    pl.when(kind == _COL_V)(lambda: project(elementwise(lambda a: a)))
    pl.when(kind == _COL_G)(lambda: project(elementwise(lambda a: a * _sigmoid(a))))
    pl.when(kind >= _COL_GATE_A)(lambda: project(elementwise(_sigmoid)))


def _inproj(x, g, w, cos, sin, bm, bn, table_blocks, side=(), tail=None):
    xrows, d = x.shape
    m = xrows if tail is None else xrows + tail.shape[0]
    assert tail is None or m == bm
    n = w.shape[1]
    width = 1024
    nper = width // bn
    cast_w = w.dtype != BF16

    def act_map(i, j):
        kind = j // nper
        return i, jnp.where(kind == 0, 0, _act_block(kind) * nper + j % nper)

    in_specs = [
        pl.BlockSpec((min(bm, xrows), d), lambda i, j: (i, 0)),
        pl.BlockSpec((1, d), lambda i, j: (0, 0)),
        pl.BlockSpec((d, bn), lambda i, j: (0, j)),
        pl.BlockSpec((bm, LANES), lambda i, j: (i % table_blocks, 0)),
        pl.BlockSpec((bm, LANES), lambda i, j: (i % table_blocks, 0)),
    ]
    if tail is not None:
        in_specs.append(pl.BlockSpec(tail.shape, lambda i, j: (0, 0)))
    out_specs = [
        pl.BlockSpec((bm, bn), lambda i, j: (i, jnp.minimum(j, nper - 1))),
        pl.BlockSpec((bm, bn), act_map),
    ]
    out_shape = [jax.ShapeDtypeStruct((m, width), F32), jax.ShapeDtypeStruct((m, n - width), BF16)]
    if cast_w:
        out_specs.append(pl.BlockSpec((d, bn), lambda i, j: (0, j)))
        out_shape.append(jax.ShapeDtypeStruct(w.shape, BF16))
    for arr, block, index_map in side:
        in_specs.append(pl.BlockSpec(block, index_map))
        out_specs.append(pl.BlockSpec(block, index_map))
        out_shape.append(jax.ShapeDtypeStruct(arr.shape, BF16))
    return pl.pallas_call(
        functools.partial(_inproj_kernel, nper=nper, cast_w=cast_w, n_side=len(side), has_tail=tail is not None),
        name="inproj",
        grid=(m // bm, n // bn),
        in_specs=in_specs,
        out_specs=out_specs,
        out_shape=out_shape,
        scratch_shapes=[pltpu.VMEM((bm, d), BF16)],
        compiler_params=_cparams(2),
    )(x, g, w, cos, sin, *([] if tail is None else [tail]), *[arr for arr, _, _ in side])


def _lru_coeffs(xc, wbd_ref, bg_ref, cl_row, group):
    gates = jnp.dot(xc.astype(BF16), wbd_ref[group], preferred_element_type=F32) + bg_ref[group]
    rec = _sigmoid(gates[:, :LANES])
    ing = _sigmoid(gates[:, LANES:])
    log_a = rec * cl_row
    a = jnp.exp(log_a)
    mult = jnp.sqrt(jnp.tanh(-log_a) * (a * a + 1.0))
    return a, mult, ing


def _lru_seq_kernel(xa_ref, ga_ref, conv0_ref, h0_ref, cw_ref, cb_ref, wbd_ref, bg_ref, lam_ref,
                    *refs, tr, emit_y, first_pos_zero):
    if emit_y:
        ya_ref, convout_ref, hout_ref, halo_ref, h_ref, a_ref, b_ref = refs
    else:
        convout_ref, hout_ref, halo_ref, h_ref, a_ref, b_ref = refs
    t = pl.program_id(1)
    halo = 8

    @pl.when(t == 0)
    def _():
        halo_ref[0:halo - 3, :] = jnp.zeros((halo - 3, halo_ref.shape[1]), F32)
        halo_ref[halo - 3:halo, :] = conv0_ref[0]
        h_ref[...] = h0_ref[0]

    xa = xa_ref[...]
    cw = [cw_ref[CONV_W - 1 - k:CONV_W - k, :] for k in range(CONV_W)]
    xc = cb_ref[...] + cw[0] * xa + sum(cw[k] * pltpu.roll(xa, k, axis=0) for k in range(1, CONV_W))
    window = jnp.concatenate([halo_ref[0:halo, :], xa[0:halo, :]], axis=0)
    first = cb_ref[...] + sum(cw[k] * window[halo - k:2 * halo - k, :] for k in range(CONV_W))
    xc = jnp.concatenate([first, xc[halo:, :]], axis=0)
    halo_ref[0:halo, :] = xa[tr - halo:, :]

    cl = LRU_C * jax.nn.log_sigmoid(lam_ref[...])
    for g in range(xc.shape[1] // LANES):
        sl = slice(g * LANES, (g + 1) * LANES)
        a, mult, ing = _lru_coeffs(xc[:, sl], wbd_ref, bg_ref, cl[:, sl], g)
        if first_pos_zero:
            row = lax.broadcasted_iota(jnp.int32, a.shape, 0) + t * tr
            mult = jnp.where(row == 0, 1.0, mult)
        a_ref[:, sl] = a
        b_ref[:, sl] = mult * ing * xc[:, sl]

    h = h_ref[...]
    for r in range(tr):
        h = a_ref[r:r + 1, :] * h + b_ref[r:r + 1, :]
        halo_ref[halo + r:halo + r + 1, :] = h
    h_ref[...] = h
    if emit_y:
        ya_ref[...] = (halo_ref[halo:halo + tr, :] * ga_ref[...].astype(F32)).astype(BF16)

    @pl.when(t == pl.num_programs(1) - 1)
    def _():
        convout_ref[0] = halo_ref[halo - 3:halo, :]
        hout_ref[0] = h_ref[...]


def _lru_seq(xa, act, row0, nseq, seqlen, tr, conv0, h0, lw, emit_y, first_pos_zero):
    w = xa.shape[1]
    nt = seqlen // tr
    rb0 = row0 // tr
    ng = w // LANES
    state_map = (lambda b, t: (b, 0, 0)) if conv0.shape[0] == nseq else (lambda b, t: (0, 0, 0))
    const2 = lambda b, t: (0, 0)
    const3 = lambda b, t: (0, 0, 0)
    out_specs = [pl.BlockSpec((1, CONV_W - 1, w), lambda b, t: (b, 0, 0)),
                 pl.BlockSpec((1, 1, w), lambda b, t: (b, 0, 0))]
    out_shape = [jax.ShapeDtypeStruct((nseq, CONV_W - 1, w), F32), jax.ShapeDtypeStruct((nseq, 1, w), F32)]
    if emit_y:
        out_specs = [pl.BlockSpec((tr, w), lambda b, t: (b * nt + t, 0))] + out_specs
        out_shape = [jax.ShapeDtypeStruct((nseq * seqlen, w), BF16)] + out_shape
    return dict(
        name="lru_seq",
        kernel=functools.partial(_lru_seq_kernel, tr=tr, emit_y=emit_y, first_pos_zero=first_pos_zero),
        grid=(nseq, nt),
        in_specs=[
            pl.BlockSpec((tr, w), lambda b, t: (rb0 + b * nt + t, 0)),
            pl.BlockSpec((tr, w), lambda b, t: (rb0 + b * nt + t, _ACT_GELU)),
            pl.BlockSpec((1, CONV_W - 1, w), state_map),
            pl.BlockSpec((1, 1, w), state_map),
            pl.BlockSpec((CONV_W, w), const2),
            pl.BlockSpec((1, w), const2),
            pl.BlockSpec((ng, LANES, 2 * LANES), const3),
            pl.BlockSpec((ng, 1, 2 * LANES), const3),
            pl.BlockSpec((1, w), const2),
        ],
        out_specs=out_specs,
        out_shape=out_shape,
        scratch_shapes=[pltpu.VMEM((8 + tr, w), F32), pltpu.VMEM((1, w), F32),
                        pltpu.VMEM((tr, w), F32), pltpu.VMEM((tr, w), F32)],
        args=(xa, act, conv0, h0, lw["conv_w"], lw["conv_b"], lw["wbd"], lw["bg"], lw["lam"]))


def _lru_step_kernel(xa_ref, ga_ref, conv0_ref, h0_ref, cw_ref, cb_ref, wbd_ref, bg_ref, lam_ref,
                     ya_ref, convout_ref, hout_ref, hs_ref, *, nseq, seqlen):
    cl = LRU_C * jax.nn.log_sigmoid(lam_ref[...])
    full = [conv0_ref[j] for j in range(CONV_W - 1)]
    full += [xa_ref[pl.ds(t, nseq, stride=seqlen), :] for t in range(seqlen)]
    h = h0_ref[...]
    for t in range(seqlen):
        xc = cb_ref[...] + cw_ref[0:1, :] * full[t]
        for j in range(1, CONV_W):
            xc = xc + cw_ref[j:j + 1, :] * full[t + j]
        a, mult, ing = _lru_coeffs(xc, wbd_ref, bg_ref, cl, 0)
        h = a * h + mult * ing * xc
        hs_ref[pl.ds(t, nseq, stride=seqlen), :] = h
    ya_ref[...] = (hs_ref[...] * ga_ref[...].astype(F32)).astype(BF16)
    for j in range(CONV_W - 1):
        convout_ref[j] = full[seqlen + j]
    hout_ref[...] = h


def _lru_step(xa, act, nseq, seqlen, conv0_t, h0, lw):
    w = xa.shape[1]
    rows = nseq * seqlen
    ng = w // LANES
    col = lambda g: (0, g)
    return pl.pallas_call(
        functools.partial(_lru_step_kernel, nseq=nseq, seqlen=seqlen),
        name="lru_step",
        grid=(ng,),
        in_specs=[
            pl.BlockSpec((rows, LANES), col),
            pl.BlockSpec((rows, LANES), col),
            pl.BlockSpec((CONV_W - 1, nseq, LANES), lambda g: (0, 0, g)),
            pl.BlockSpec((nseq, LANES), col),
            pl.BlockSpec((CONV_W, LANES), col),
            pl.BlockSpec((1, LANES), col),
            pl.BlockSpec((1, LANES, 2 * LANES), lambda g: (g, 0, 0)),
            pl.BlockSpec((1, 1, 2 * LANES), lambda g: (g, 0, 0)),
            pl.BlockSpec((1, LANES), col),
        ],
        out_specs=[
            pl.BlockSpec((rows, LANES), col),
            pl.BlockSpec((CONV_W - 1, nseq, LANES), lambda g: (0, 0, g)),
            pl.BlockSpec((nseq, LANES), col),
        ],
        out_shape=[jax.ShapeDtypeStruct((rows, w), BF16),
                   jax.ShapeDtypeStruct((CONV_W - 1, nseq, w), F32),
                   jax.ShapeDtypeStruct((nseq, w), F32)],
        scratch_shapes=[pltpu.VMEM((rows, LANES), F32)],
        compiler_params=_cparams(1),
    )(xa, act, conv0_t, h0, lw["conv_w"], lw["conv_b"], lw["wbd"], lw["bg"], lw["lam"])


def _log_gamma(h):
    return math.log1p(-(2.0 ** (-5.0 - h)))


def _ret_kernel(q_ref, k_ref, v_ref, sg_ref, s0_ref, ng_ref, *refs, nb, clen, cps, emit_y, ids):
    if emit_y:
        y_ref, sout_ref, s_ref, dec_ref, rdec_ref = refs
    else:
        sout_ref, s_ref, dec_ref, rdec_ref = refs
    group, c, last = ids() if ids is not None else (pl.program_id(0), pl.program_id(1), pl.num_programs(1) - 1)
    rows = nb * clen
    heads = range(RET_HEADS)
    chunks = range(cps)

    @pl.when((group == 0) & (c == 0))
    def _():
        li = lax.broadcasted_iota(jnp.int32, (rows, rows), 0)
        mi = lax.broadcasted_iota(jnp.int32, (rows, rows), 1)
        keep = (li // clen == mi // clen) & (li >= mi)
        diff = jnp.where(keep, li - mi, 0).astype(F32)
        tpos = (lax.broadcasted_iota(jnp.int32, (rows, LANES), 0) % clen).astype(F32)
        for h in heads:
            lg = _log_gamma(h)
            dec_ref[h] = jnp.where(keep, jnp.exp(lg * diff), 0.0)
            rdec_ref[0, h] = jnp.exp(lg * (tpos + 1.0))
            rdec_ref[1, h] = jnp.exp(lg * (clen - 1.0 - tpos))

    @pl.when(c == 0)
    def _():
        for n in range(nb):
            s_ref[n] = s0_ref[n if s0_ref.shape[0] == nb else 0]

    if nb > 1:
        seq_of_row = lax.broadcasted_iota(jnp.int32, (rows, nb * HEAD_DIM), 0) // clen
        seq_of_col = lax.broadcasted_iota(jnp.int32, (rows, nb * HEAD_DIM), 1) // HEAD_DIM
        own = seq_of_row == seq_of_col
    tn = (((0,), (0,)), ((), ()))
    nt = (((1,), (1,)), ((), ()))
    hsl = [slice(h * HEAD_DIM, (h + 1) * HEAD_DIM) for h in heads]
    rsl = [slice(ci * rows, (ci + 1) * rows) for ci in chunks]
    q = [[q_ref[rsl[ci], sl] for sl in hsl] for ci in chunks]
    k = [[k_ref[rsl[ci], sl] for sl in hsl] for ci in chunks]
    v = [[v_ref[rsl[ci], sl] for sl in hsl] for ci in chunks]

    if emit_y:
        scores = [[lax.dot_general(q[ci][h], k[ci][h], nt, preferred_element_type=F32) for h in heads]
                  for ci in chunks]
    upd = []
    for ci in chunks:
        upd.append([])
        for h in heads:
            kd = (k[ci][h].astype(F32) * rdec_ref[1, h]).astype(BF16)
            if nb > 1:
                v_bd = jnp.where(own, jnp.concatenate([v[ci][h].astype(F32)] * nb, axis=1), 0.0).astype(BF16)
            else:
                v_bd = v[ci][h]
            upd[ci].append(lax.dot_general(kd, v_bd, tn, preferred_element_type=F32))

    state = [[[s_ref[n, h] for n in range(nb)] for h in heads]]
    for ci in chunks:
        state.append([[math.exp(clen * _log_gamma(h)) * state[ci][h][n]
                       + upd[ci][h][:, n * HEAD_DIM:(n + 1) * HEAD_DIM] for n in range(nb)] for h in heads])
    for h in heads:
        for n in range(nb):
            s_ref[n, h] = state[cps][h][n]

    if emit_y:
        inter = []
        for ci in chunks:
            inter.append([])
            for h in heads:
                s_cat = jnp.concatenate(state[ci][h], axis=1) if nb > 1 else state[ci][h][0]
                qs = jnp.dot(q[ci][h], s_cat.astype(BF16), preferred_element_type=F32)
                if nb > 1:
                    qs = jnp.where(own, qs, 0.0)
                    qs = sum(qs[:, n * HEAD_DIM:(n + 1) * HEAD_DIM] for n in range(nb))
                inter[ci].append(qs)
        for ci in chunks:
            for h in heads:
                p = (scores[ci][h] * dec_ref[h]).astype(BF16)
                o = jnp.dot(p, v[ci][h], preferred_element_type=F32) + inter[ci][h] * rdec_ref[0, h]
                mu = jnp.mean(o, axis=-1, keepdims=True)
                dev = o - mu
                var = jnp.mean(dev * dev, axis=-1, keepdims=True)
                normed = dev * lax.rsqrt(var + EPS) * ng_ref[:, hsl[h]]
                y_ref[rsl[ci], hsl[h]] = (sg_ref[rsl[ci], hsl[h]].astype(F32) * normed).astype(BF16)

    @pl.when(c == last)
    def _():
        sout_ref[...] = s_ref[...]


def _retention(act, row0, nseq, seqlen, nb, clen, cps, s0, norm_g, emit_y, ids=None):
    w = RET_HEADS * HEAD_DIM
    rows = nb * clen * cps
    nc = seqlen // (clen * cps)
    assert nb == 1 or (nc == 1 and cps == 1)
    rb0 = row0 // rows
    rmap = lambda col: (lambda b, c: (rb0 + b * nc + c, col))
    shared = s0.shape[0] != nseq
    s_map = (lambda b, c: (0, 0, 0, 0)) if shared else (lambda b, c: (b, 0, 0, 0))
    s_block = (1 if shared else nb, RET_HEADS, HEAD_DIM, HEAD_DIM)
    out_specs = [pl.BlockSpec((nb, RET_HEADS, HEAD_DIM, HEAD_DIM), lambda b, c: (b, 0, 0, 0))]
    out_shape = [jax.ShapeDtypeStruct((nseq, RET_HEADS, HEAD_DIM, HEAD_DIM), F32)]
    if emit_y:
        out_specs = [pl.BlockSpec((rows, w), lambda b, c: (b * nc + c, 0))] + out_specs
        out_shape = [jax.ShapeDtypeStruct((nseq * seqlen, w), BF16)] + out_shape
    return dict(
        name="retention",
        kernel=functools.partial(_ret_kernel, nb=nb, clen=clen, cps=cps, emit_y=emit_y, ids=ids),
        grid=(nseq // nb, nc),
        in_specs=[
            pl.BlockSpec((rows, w), rmap(_ACT_Q)),
            pl.BlockSpec((rows, w), rmap(_ACT_K)),
            pl.BlockSpec((rows, w), rmap(_ACT_V)),
            pl.BlockSpec((rows, w), rmap(_ACT_SILU)),
            pl.BlockSpec(s_block, s_map),
            pl.BlockSpec((1, w), lambda b, c: (0, 0)),
        ],
        out_specs=out_specs,
        out_shape=out_shape,
        scratch_shapes=[pltpu.VMEM((nb, RET_HEADS, HEAD_DIM, HEAD_DIM), F32),
                        pltpu.VMEM((RET_HEADS, nb * clen, nb * clen), F32),
                        pltpu.VMEM((2, RET_HEADS, nb * clen, LANES), F32)],
        args=(act, act, act, act, s0, norm_g))


def _outproj_kernel(x_ref, ya_ref, yb_ref, sga_ref, sgb_ref, pa_ref, pb_ref, wo_ref, o_ref):
    for r0, nr in _row_chunks(o_ref.shape[0], 256):
        rows = slice(r0, r0 + nr)
        ma = jnp.dot(ya_ref[rows, :], pa_ref[...], preferred_element_type=F32)
        mb = jnp.dot(yb_ref[rows, :], pb_ref[...], preferred_element_type=F32)
        merged = sga_ref[rows, :].astype(F32) * ma + sgb_ref[rows, :].astype(F32) * mb
        o_ref[rows, :] = x_ref[rows, :] + jnp.dot(merged.astype(BF16), wo_ref[...], preferred_element_type=F32)


def _outproj(x, ya, yb, act, pa, pb, wo, bm):
    m, d = x.shape
    w = ya.shape[1]
    nd = d // 1024
    row = lambda i: (i, 0)
    const = lambda i: (0, 0)
    single = pl.Buffered(1)
    return pl.pallas_call(
        _outproj_kernel,
        name="outproj",
        grid=(m // bm,),
        in_specs=[
            pl.BlockSpec((bm, d), row),
            pl.BlockSpec((bm, w), row),
            pl.BlockSpec((bm, w), row),
            pl.BlockSpec((bm, d), lambda i: (i, _ACT_SGA // nd)),
            pl.BlockSpec((bm, d), lambda i: (i, _ACT_SGB // nd)),
            pl.BlockSpec((w, d), const, pipeline_mode=single),
            pl.BlockSpec((w, d), const, pipeline_mode=single),
            pl.BlockSpec((d, d), const, pipeline_mode=single),
        ],
        out_specs=pl.BlockSpec((bm, d), row),
        out_shape=jax.ShapeDtypeStruct((m, d), F32),
        compiler_params=_cparams(1),
    )(x, ya, yb, act, act, pa, pb, wo)


def _ffn_kernel(x_ref, g_ref, wu_ref, wd_ref, gf_ref, o_ref, hn_ref):
    j = pl.program_id(1)
    last = pl.num_programs(1) - 1

    def block(first, final):
        for r0, nr in _row_chunks(o_ref.shape[0], 512):
            rows = slice(r0, r0 + nr)
            if first:
                base = x_ref[rows, :]
                hn = _rmsnorm(base, g_ref[...]).astype(BF16)
                hn_ref[rows, :] = hn
            else:
                base = o_ref[rows, :]
                hn = hn_ref[rows, :]
            u = jnp.dot(hn, wu_ref[...], preferred_element_type=F32)
            r = jnp.square(jnp.maximum(u, 0.0)).astype(BF16)
            acc = base + jnp.dot(r, wd_ref[...], preferred_element_type=F32)
            o_ref[rows, :] = _rmsnorm(acc, gf_ref[...]) if final else acc

    pl.when(j == 0)(lambda: block(True, False))
    pl.when((j > 0) & (j < last))(lambda: block(False, False))
    pl.when(j == last)(lambda: block(False, True))


def _ffn(x, g, wu, wd, gf, bm, bf):
    m, d = x.shape
    dff = wu.shape[1]
    assert dff // bf >= 2
    return pl.pallas_call(
        _ffn_kernel,
        name="ffn",
        grid=(m // bm, dff // bf),
        in_specs=[
            pl.BlockSpec((bm, d), lambda i, j: (i, 0)),
            pl.BlockSpec((1, d), lambda i, j: (0, 0)),
            pl.BlockSpec((d, bf), lambda i, j: (0, j)),
            pl.BlockSpec((bf, d), lambda i, j: (j, 0)),
            pl.BlockSpec((1, d), lambda i, j: (0, 0)),
        ],
        out_specs=pl.BlockSpec((bm, d), lambda i, j: (i, 0)),
        out_shape=jax.ShapeDtypeStruct((m, d), F32),
        scratch_shapes=[pltpu.VMEM((bm, d), BF16)],
        compiler_params=_cparams(2),
    )(x, g, wu, wd, gf)


def _rope_tables(pos):
    inv = ROPE_BASE ** (-jnp.arange(0, HEAD_DIM, 2, dtype=F32) / HEAD_DIM)
    ang = pos.astype(F32)[:, None] * inv[None, :]
    cos, sin = jnp.cos(ang), jnp.sin(ang)
    return jnp.concatenate([cos, cos], axis=1), jnp.concatenate([-sin, sin], axis=1)


def _gate_weights(wa, wx, ba, bx):
    nblk, blk, _ = wa.shape
    per = LANES // blk
    ng = nblk // per

    def bd(wt):
        wt = wt.reshape(ng, per, blk, blk)
        eye = jnp.eye(per, dtype=wt.dtype)
        return jnp.einsum("gpcd,pq->gpcqd", wt, eye).reshape(ng, LANES, LANES)

    wbd = jnp.concatenate([bd(wa), bd(wx)], axis=2).astype(BF16)
    bg = jnp.concatenate([ba.reshape(ng, 1, LANES), bx.reshape(ng, 1, LANES)], axis=2)
    return wbd, bg


def kernel(x_prompt, x_sample, state_conv, state_lru, state_ret, meta_tokens, norm_mix_g, w_in, conv_w,
           conv_b, lru_wa, lru_ba, lru_wx, lru_bx, lru_lam, ret_norm_g, p_a, p_b, w_out, norm_ffn_g,
           w_up, w_down, norm_f_g):
    assert w_in.shape[0] == 1
    nb_p, t_p, d = x_prompt.shape
    nb_s, t_s, _ = x_sample.shape
    w = conv_w.shape[-1]
    rows_p, rows_s = nb_p * t_p, nb_s * t_s

    g_mix, g_ffn, g_f = norm_mix_g[0][None], norm_ffn_g[0][None], norm_f_g[None]
    wbd, bg = _gate_weights(lru_wa[0], lru_wx[0], lru_ba[0], lru_bx[0])
    lw = dict(conv_w=conv_w[0], conv_b=conv_b[0][None], wbd=wbd, bg=bg, lam=lru_lam[0][None])
    ret_g = ret_norm_g[0][None]

    cos_p, sin_p = _rope_tables(N_META + jnp.arange(t_p, dtype=jnp.int32))
    pos_sm = jnp.concatenate([jnp.tile(PAST_LEN + jnp.arange(t_s, dtype=jnp.int32), nb_s),
                              jnp.arange(N_META, dtype=jnp.int32)])
    cos_s, sin_s = _rope_tables(pos_sm)
    x_p2 = x_prompt.reshape(rows_p, d)
    x_s2 = x_sample.reshape(rows_s, d)
    xa_s, act_s, w_in_b = _inproj(x_s2, g_mix, w_in[0], cos_s, sin_s, rows_s + N_META, 512, 1,
                                  tail=meta_tokens.astype(x_sample.dtype))
    bm_p = 1024
    n_i, n_j = rows_p // bm_p, 8
    side_map = lambda i, j: (i, jnp.minimum(j, n_j - 1))
    later = (w_up, w_down, p_a, p_b, w_out)
    sides = tuple((wt[0], (wt.shape[1] // n_i, wt.shape[2] // n_j), side_map) for wt in later)
    xa_p, act_p, wu_b, wd_b, pa_b, pb_b, wo_b = _inproj(x_p2, g_mix, w_in_b, cos_p, sin_p, bm_p, 1024, t_p // bm_p,
                                                        side=sides)


    zc = jnp.zeros((1, CONV_W - 1, w), F32)
    zh = jnp.zeros((1, 1, w), F32)
    zs = jnp.zeros((1, RET_HEADS, HEAD_DIM, HEAD_DIM), F32)
    conv_m, h_m = _call(_lru_seq(xa_s, act_s, rows_s, 1, N_META, N_META, zc, zh, lw, emit_y=False,
                                 first_pos_zero=True))
    (s_m,) = _call(_retention(act_s, rows_s, 1, N_META, 1, N_META, 1, zs, ret_g, emit_y=False))

    lru_tr = 512
    lru_p = _lru_seq(xa_p, act_p, 0, nb_p, t_p, lru_tr, conv_m, h_m, lw, emit_y=True, first_pos_zero=False)
    steps = nb_p * (t_p // lru_tr)
    ret_s = _retention(act_s, 0, nb_s, t_s, nb_s // steps, t_s, 1, state_ret[0], ret_g, emit_y=True,
                       ids=lambda: (pl.program_id(0) * (t_p // lru_tr) + pl.program_id(1), 0, 0))
    (ya_p, conv_p, h_p), (yb_s, s_s) = _call_pair(lru_p, ret_s, "lru_seq_retention")
    yb_p, s_p = _call(_retention(act_p, 0, nb_p, t_p, 1, CHUNK, 4, s_m, ret_g, emit_y=True))
    conv0_t = jnp.transpose(state_conv[0], (1, 0, 2))
    ya_s, conv_s_t, h_s = _lru_step(xa_s, act_s, nb_s, t_s, conv0_t, state_lru[0], lw)

    x1_p = _outproj(x_p2, ya_p, yb_p, act_p, pa_b, pb_b, wo_b, 512)
    x1_s = _outproj(x_s2, ya_s, yb_s, act_s, pa_b, pb_b, wo_b, 512)
    y_p = _ffn(x1_p, g_ffn, wu_b, wd_b, g_f, 1024, 1024)
    y_s = _ffn(x1_s, g_ffn, wu_b, wd_b, g_f, 1024, 1024)

    return (y_p.reshape(nb_p, t_p, d), y_s.reshape(nb_s, t_s, d),
            conv_p[None], h_p.reshape(1, nb_p, w), s_p[None],
            jnp.transpose(conv_s_t, (1, 0, 2))[None], h_s[None], s_s[None])
```

```python
import functools
import math

import jax
import jax.numpy as jnp
from jax import lax
from jax.experimental import pallas as pl
from jax.experimental.pallas import tpu as pltpu

F32 = jnp.float32
BF16 = jnp.bfloat16

N_META = 16
PAST_LEN = 16384
LRU_BLOCKS = 16
CONV_W = 4
LRU_C = 8.0
RET_HEADS = 8
HEAD_DIM = 128
CHUNK = 128
ROPE_BASE = 10000.0
EPS = 1e-6

LANES = 128
VMEM_LIMIT = 56 << 20


def _cparams(n_axes):
    return pltpu.CompilerParams(dimension_semantics=("arbitrary",) * n_axes, vmem_limit_bytes=VMEM_LIMIT)


def _call(spec):
    return pl.pallas_call(
        spec["kernel"], name=spec["name"], grid=spec["grid"], in_specs=spec["in_specs"],
        out_specs=spec["out_specs"], out_shape=spec["out_shape"], scratch_shapes=spec["scratch_shapes"],
        compiler_params=_cparams(len(spec["grid"])))(*spec["args"])


def _call_pair(a, b, name):
    n0, n1 = a["grid"]
    assert b["grid"] == (n0 * n1, 1)

    def remap(spec):
        return pl.BlockSpec(spec.block_shape, lambda i, j, m=spec.index_map: m(i * n1 + j, 0))

    counts = [len(a["in_specs"]), len(b["in_specs"]), len(a["out_specs"]), len(b["out_specs"]),
              len(a["scratch_shapes"]), len(b["scratch_shapes"])]

    def kernel(*refs):
        groups, pos = [], 0
        for n in counts:
            groups.append(refs[pos:pos + n])
            pos += n
        a["kernel"](*groups[0], *groups[2], *groups[4])
        b["kernel"](*groups[1], *groups[3], *groups[5])

    outs = pl.pallas_call(
        kernel, name=name, grid=a["grid"],
        in_specs=list(a["in_specs"]) + [remap(sp) for sp in b["in_specs"]],
        out_specs=list(a["out_specs"]) + [remap(sp) for sp in b["out_specs"]],
        out_shape=list(a["out_shape"]) + list(b["out_shape"]),
        scratch_shapes=list(a["scratch_shapes"]) + list(b["scratch_shapes"]),
        compiler_params=_cparams(2))(*a["args"], *b["args"])
    return outs[:counts[2]], outs[counts[2]:]


def _rmsnorm(x, g):
    return x * lax.rsqrt(jnp.mean(x * x, axis=-1, keepdims=True) + EPS) * g


def _sigmoid(x):
    return 0.5 * jnp.tanh(0.5 * x) + 0.5


def _row_chunks(rows, size):
    n = max(rows // size, 1)
    return [(c * size, size if c < n - 1 else rows - c * size) for c in range(n)]


_COL_GATE, _COL_Q, _COL_K, _COL_V, _COL_G, _COL_GATE_A = 1, 2, 3, 4, 5, 6
_ACT_GELU, _ACT_Q, _ACT_K, _ACT_V, _ACT_SGA, _ACT_SGB, _ACT_SILU = 0, 1, 2, 3, 4, 6, 8


def _act_block(kind):
    return jnp.where(kind <= _COL_V, kind - 1, jnp.where(kind == _COL_G, _ACT_SILU, kind - 2))


def _inproj_kernel(x_ref, g_ref, w_ref, cos_ref, sin_ref, *refs, nper, cast_w, n_side, has_tail):
    if has_tail:
        tail_ref, refs = refs[0], refs[1:]
    side_src, refs = refs[:n_side], refs[n_side:]
    xa_ref, act_ref = refs[0], refs[1]
    refs = refs[2:]
    if cast_w:
        wb_ref, refs = refs[0], refs[1:]
    side_dst, (xn_ref,) = refs[:n_side], refs[n_side:]
    j = pl.program_id(1)
    kind = j // nper
    bn = xa_ref.shape[1]

    for src, dst in zip(side_src, side_dst):
        dst[...] = src[...].astype(BF16)
    if cast_w:
        wb_ref[...] = w_ref[...].astype(BF16)
        w_bf = wb_ref
    else:
        w_bf = w_ref

    def project(epilogue, normalize=False):
        if normalize:
            nx = x_ref.shape[0]
            xn_ref[0:nx, :] = _rmsnorm(x_ref[...], g_ref[...]).astype(BF16)
            if has_tail:
                xn_ref[nx:, :] = _rmsnorm(tail_ref[...], g_ref[...]).astype(BF16)
        epilogue(jnp.dot(xn_ref[...], w_bf[...], preferred_element_type=F32))

    def to_xa(acc):
        xa_ref[...] = acc

    def rope(scale):
        def epilogue(acc):
            cos, sin = cos_ref[...], sin_ref[...]
            for h in range(bn // HEAD_DIM):
                a = acc[:, h * HEAD_DIM:(h + 1) * HEAD_DIM]
                r = a * cos + pltpu.roll(a, HEAD_DIM // 2, axis=1) * sin
                if scale is not None:
                    r = r * scale
                act_ref[:, h * HEAD_DIM:(h + 1) * HEAD_DIM] = r.astype(BF16)
        return epilogue

    def elementwise(fn):
        def epilogue(acc):
            act_ref[...] = fn(acc).astype(BF16)
        return epilogue

    pl.when(j == 0)(lambda: project(to_xa, normalize=True))
    if nper > 1:
        pl.when((j > 0) & (kind == 0))(lambda: project(to_xa))
    pl.when(kind == _COL_GATE)(lambda: project(elementwise(jax.nn.gelu)))
    pl.when(kind == _COL_Q)(lambda: project(rope(None)))
    pl.when(kind == _COL_K)(lambda: project(rope(HEAD_DIM ** -0.5)))
    pl.when(kind == _COL_V)(lambda: project(elementwise(lambda a: a)))
    pl.when(kind == _COL_G)(lambda: project(elementwise(lambda a: a * _sigmoid(a))))
    pl.when(kind >= _COL_GATE_A)(lambda: project(elementwise(_sigmoid)))


def _inproj(x, g, w, cos, sin, bm, bn, table_blocks, side=(), tail=None):
    xrows, d = x.shape
    m = xrows if tail is None else xrows + tail.shape[0]
    assert tail is None or m == bm
    n = w.shape[1]
    width = 1024
    nper = width // bn
    cast_w = w.dtype != BF16

    def act_map(i, j):
        kind = j // nper
        return i, jnp.where(kind == 0, 0, _act_block(kind) * nper + j % nper)

    in_specs = [
        pl.BlockSpec((min(bm, xrows), d), lambda i, j: (i, 0)),
        pl.BlockSpec((1, d), lambda i, j: (0, 0)),
        pl.BlockSpec((d, bn), lambda i, j: (0, j)),
        pl.BlockSpec((bm, LANES), lambda i, j: (i % table_blocks, 0)),
        pl.BlockSpec((bm, LANES), lambda i, j: (i % table_blocks, 0)),
    ]
    if tail is not None:
        in_specs.append(pl.BlockSpec(tail.shape, lambda i, j: (0, 0)))
    out_specs = [
        pl.BlockSpec((bm, bn), lambda i, j: (i, jnp.minimum(j, nper - 1))),
        pl.BlockSpec((bm, bn), act_map),
    ]
    out_shape = [jax.ShapeDtypeStruct((m, width), F32), jax.ShapeDtypeStruct((m, n - width), BF16)]
    if cast_w:
        out_specs.append(pl.BlockSpec((d, bn), lambda i, j: (0, j)))
        out_shape.append(jax.ShapeDtypeStruct(w.shape, BF16))
    for arr, block, index_map in side:
        in_specs.append(pl.BlockSpec(block, index_map))
        out_specs.append(pl.BlockSpec(block, index_map))
        out_shape.append(jax.ShapeDtypeStruct(arr.shape, BF16))
    return pl.pallas_call(
        functools.partial(_inproj_kernel, nper=nper, cast_w=cast_w, n_side=len(side), has_tail=tail is not None),
        name="inproj",
        grid=(m // bm, n // bn),
        in_specs=in_specs,
        out_specs=out_specs,
        out_shape=out_shape,
        scratch_shapes=[pltpu.VMEM((bm, d), BF16)],
        compiler_params=_cparams(2),
    )(x, g, w, cos, sin, *([] if tail is None else [tail]), *[arr for arr, _, _ in side])


def _lru_coeffs(xc, wbd_ref, bg_ref, cl_row, group):
    gates = jnp.dot(xc.astype(BF16), wbd_ref[group], preferred_element_type=F32) + bg_ref[group]
    rec = _sigmoid(gates[:, :LANES])
    ing = _sigmoid(gates[:, LANES:])
    log_a = rec * cl_row
    a = jnp.exp(log_a)
    mult = jnp.sqrt(jnp.tanh(-log_a) * (a * a + 1.0))
    return a, mult, ing


def _lru_seq_kernel(xa_ref, ga_ref, conv0_ref, h0_ref, cw_ref, cb_ref, wbd_ref, bg_ref, lam_ref,
                    *refs, tr, emit_y, first_pos_zero):
    if emit_y:
        ya_ref, convout_ref, hout_ref, halo_ref, h_ref, a_ref, b_ref = refs
    else:
        convout_ref, hout_ref, halo_ref, h_ref, a_ref, b_ref = refs
    t = pl.program_id(1)
    halo = 8

    @pl.when(t == 0)
    def _():
        halo_ref[0:halo - 3, :] = jnp.zeros((halo - 3, halo_ref.shape[1]), F32)
        halo_ref[halo - 3:halo, :] = conv0_ref[0]
        h_ref[...] = h0_ref[0]

    xa = xa_ref[...]
    cw = [cw_ref[CONV_W - 1 - k:CONV_W - k, :] for k in range(CONV_W)]
    xc = cb_ref[...] + cw[0] * xa + sum(cw[k] * pltpu.roll(xa, k, axis=0) for k in range(1, CONV_W))
    window = jnp.concatenate([halo_ref[0:halo, :], xa[0:halo, :]], axis=0)
    first = cb_ref[...] + sum(cw[k] * window[halo - k:2 * halo - k, :] for k in range(CONV_W))
    xc = jnp.concatenate([first, xc[halo:, :]], axis=0)
    halo_ref[0:halo, :] = xa[tr - halo:, :]

    cl = LRU_C * jax.nn.log_sigmoid(lam_ref[...])
    for g in range(xc.shape[1] // LANES):
        sl = slice(g * LANES, (g + 1) * LANES)
        a, mult, ing = _lru_coeffs(xc[:, sl], wbd_ref, bg_ref, cl[:, sl], g)
        if first_pos_zero:
            row = lax.broadcasted_iota(jnp.int32, a.shape, 0) + t * tr
            mult = jnp.where(row == 0, 1.0, mult)
        a_ref[:, sl] = a
        b_ref[:, sl] = mult * ing * xc[:, sl]

    h = h_ref[...]
    for r in range(tr):
        h = a_ref[r:r + 1, :] * h + b_ref[r:r + 1, :]
        halo_ref[halo + r:halo + r + 1, :] = h
    h_ref[...] = h
    if emit_y:
        ya_ref[...] = (halo_ref[halo:halo + tr, :] * ga_ref[...].astype(F32)).astype(BF16)

    @pl.when(t == pl.num_programs(1) - 1)
    def _():
        convout_ref[0] = halo_ref[halo - 3:halo, :]
        hout_ref[0] = h_ref[...]


def _lru_seq(xa, act, row0, nseq, seqlen, tr, conv0, h0, lw, emit_y, first_pos_zero):
    w = xa.shape[1]
    nt = seqlen // tr
    rb0 = row0 // tr
    ng = w // LANES
    state_map = (lambda b, t: (b, 0, 0)) if conv0.shape[0] == nseq else (lambda b, t: (0, 0, 0))
    const2 = lambda b, t: (0, 0)
    const3 = lambda b, t: (0, 0, 0)
    out_specs = [pl.BlockSpec((1, CONV_W - 1, w), lambda b, t: (b, 0, 0)),
                 pl.BlockSpec((1, 1, w), lambda b, t: (b, 0, 0))]
    out_shape = [jax.ShapeDtypeStruct((nseq, CONV_W - 1, w), F32), jax.ShapeDtypeStruct((nseq, 1, w), F32)]
    if emit_y:
        out_specs = [pl.BlockSpec((tr, w), lambda b, t: (b * nt + t, 0))] + out_specs
        out_shape = [jax.ShapeDtypeStruct((nseq * seqlen, w), BF16)] + out_shape
    return dict(
        name="lru_seq",
        kernel=functools.partial(_lru_seq_kernel, tr=tr, emit_y=emit_y, first_pos_zero=first_pos_zero),
        grid=(nseq, nt),
        in_specs=[
            pl.BlockSpec((tr, w), lambda b, t: (rb0 + b * nt + t, 0)),
            pl.BlockSpec((tr, w), lambda b, t: (rb0 + b * nt + t, _ACT_GELU)),
            pl.BlockSpec((1, CONV_W - 1, w), state_map),
            pl.BlockSpec((1, 1, w), state_map),
            pl.BlockSpec((CONV_W, w), const2),
            pl.BlockSpec((1, w), const2),
            pl.BlockSpec((ng, LANES, 2 * LANES), const3),
            pl.BlockSpec((ng, 1, 2 * LANES), const3),
            pl.BlockSpec((1, w), const2),
        ],
        out_specs=out_specs,
        out_shape=out_shape,
        scratch_shapes=[pltpu.VMEM((8 + tr, w), F32), pltpu.VMEM((1, w), F32),
                        pltpu.VMEM((tr, w), F32), pltpu.VMEM((tr, w), F32)],
        args=(xa, act, conv0, h0, lw["conv_w"], lw["conv_b"], lw["wbd"], lw["bg"], lw["lam"]))


def _lru_step_kernel(xa_ref, ga_ref, conv0_ref, h0_ref, cw_ref, cb_ref, wbd_ref, bg_ref, lam_ref,
                     ya_ref, convout_ref, hout_ref, hs_ref, *, nseq, seqlen):
    cl = LRU_C * jax.nn.log_sigmoid(lam_ref[...])
    full = [conv0_ref[j] for j in range(CONV_W - 1)]
    full += [xa_ref[pl.ds(t, nseq, stride=seqlen), :] for t in range(seqlen)]
    h = h0_ref[...]
    for t in range(seqlen):
        xc = cb_ref[...] + cw_ref[0:1, :] * full[t]
        for j in range(1, CONV_W):
            xc = xc + cw_ref[j:j + 1, :] * full[t + j]
        a, mult, ing = _lru_coeffs(xc, wbd_ref, bg_ref, cl, 0)
        h = a * h + mult * ing * xc
        hs_ref[pl.ds(t, nseq, stride=seqlen), :] = h
    ya_ref[...] = (hs_ref[...] * ga_ref[...].astype(F32)).astype(BF16)
    for j in range(CONV_W - 1):
        convout_ref[j] = full[seqlen + j]
    hout_ref[...] = h


def _lru_step(xa, act, nseq, seqlen, conv0_t, h0, lw):
    w = xa.shape[1]
    rows = nseq * seqlen
    ng = w // LANES
    col = lambda g: (0, g)
    return pl.pallas_call(
        functools.partial(_lru_step_kernel, nseq=nseq, seqlen=seqlen),
        name="lru_step",
        grid=(ng,),
        in_specs=[
            pl.BlockSpec((rows, LANES), col),
            pl.BlockSpec((rows, LANES), col),
            pl.BlockSpec((CONV_W - 1, nseq, LANES), lambda g: (0, 0, g)),
            pl.BlockSpec((nseq, LANES), col),
            pl.BlockSpec((CONV_W, LANES), col),
            pl.BlockSpec((1, LANES), col),
            pl.BlockSpec((1, LANES, 2 * LANES), lambda g: (g, 0, 0)),
            pl.BlockSpec((1, 1, 2 * LANES), lambda g: (g, 0, 0)),
            pl.BlockSpec((1, LANES), col),
        ],
        out_specs=[
            pl.BlockSpec((rows, LANES), col),
            pl.BlockSpec((CONV_W - 1, nseq, LANES), lambda g: (0, 0, g)),
            pl.BlockSpec((nseq, LANES), col),
        ],
        out_shape=[jax.ShapeDtypeStruct((rows, w), BF16),
                   jax.ShapeDtypeStruct((CONV_W - 1, nseq, w), F32),
                   jax.ShapeDtypeStruct((nseq, w), F32)],
        scratch_shapes=[pltpu.VMEM((rows, LANES), F32)],
        compiler_params=_cparams(1),
    )(xa, act, conv0_t, h0, lw["conv_w"], lw["conv_b"], lw["wbd"], lw["bg"], lw["lam"])


def _log_gamma(h):
    return math.log1p(-(2.0 ** (-5.0 - h)))


def _ret_kernel(q_ref, k_ref, v_ref, sg_ref, s0_ref, ng_ref, *refs, nb, clen, cps, emit_y, ids):
    if emit_y:
        y_ref, sout_ref, s_ref, dec_ref, rdec_ref = refs
    else:
        sout_ref, s_ref, dec_ref, rdec_ref = refs
    group, c, last = ids() if ids is not None else (pl.program_id(0), pl.program_id(1), pl.num_programs(1) - 1)
    rows = nb * clen
    heads = range(RET_HEADS)
    chunks = range(cps)

    @pl.when((group == 0) & (c == 0))
    def _():
        li = lax.broadcasted_iota(jnp.int32, (rows, rows), 0)
        mi = lax.broadcasted_iota(jnp.int32, (rows, rows), 1)
        keep = (li // clen == mi // clen) & (li >= mi)
        diff = jnp.where(keep, li - mi, 0).astype(F32)
        tpos = (lax.broadcasted_iota(jnp.int32, (rows, LANES), 0) % clen).astype(F32)
        for h in heads:
            lg = _log_gamma(h)
            dec_ref[h] = jnp.where(keep, jnp.exp(lg * diff), 0.0)
            rdec_ref[0, h] = jnp.exp(lg * (tpos + 1.0))
            rdec_ref[1, h] = jnp.exp(lg * (clen - 1.0 - tpos))

    @pl.when(c == 0)
    def _():
        for n in range(nb):
            s_ref[n] = s0_ref[n if s0_ref.shape[0] == nb else 0]

    if nb > 1:
        seq_of_row = lax.broadcasted_iota(jnp.int32, (rows, nb * HEAD_DIM), 0) // clen
        seq_of_col = lax.broadcasted_iota(jnp.int32, (rows, nb * HEAD_DIM), 1) // HEAD_DIM
        own = seq_of_row == seq_of_col
    tn = (((0,), (0,)), ((), ()))
    nt = (((1,), (1,)), ((), ()))
    hsl = [slice(h * HEAD_DIM, (h + 1) * HEAD_DIM) for h in heads]
    rsl = [slice(ci * rows, (ci + 1) * rows) for ci in chunks]
    q = [[q_ref[rsl[ci], sl] for sl in hsl] for ci in chunks]
    k = [[k_ref[rsl[ci], sl] for sl in hsl] for ci in chunks]
    v = [[v_ref[rsl[ci], sl] for sl in hsl] for ci in chunks]

    if emit_y:
        scores = [[lax.dot_general(q[ci][h], k[ci][h], nt, preferred_element_type=F32) for h in heads]
                  for ci in chunks]
    upd = []
    for ci in chunks:
        upd.append([])
        for h in heads:
            kd = (k[ci][h].astype(F32) * rdec_ref[1, h]).astype(BF16)
            if nb > 1:
                v_bd = jnp.where(own, jnp.concatenate([v[ci][h].astype(F32)] * nb, axis=1), 0.0).astype(BF16)
            else:
                v_bd = v[ci][h]
            upd[ci].append(lax.dot_general(kd, v_bd, tn, preferred_element_type=F32))

    state = [[[s_ref[n, h] for n in range(nb)] for h in heads]]
    for ci in chunks:
        state.append([[math.exp(clen * _log_gamma(h)) * state[ci][h][n]
                       + upd[ci][h][:, n * HEAD_DIM:(n + 1) * HEAD_DIM] for n in range(nb)] for h in heads])
    for h in heads:
        for n in range(nb):
            s_ref[n, h] = state[cps][h][n]

    if emit_y:
        inter = []
        for ci in chunks:
            inter.append([])
            for h in heads:
                s_cat = jnp.concatenate(state[ci][h], axis=1) if nb > 1 else state[ci][h][0]
                qs = jnp.dot(q[ci][h], s_cat.astype(BF16), preferred_element_type=F32)
                if nb > 1:
                    qs = jnp.where(own, qs, 0.0)
                    qs = sum(qs[:, n * HEAD_DIM:(n + 1) * HEAD_DIM] for n in range(nb))
                inter[ci].append(qs)
        for ci in chunks:
            for h in heads:
                p = (scores[ci][h] * dec_ref[h]).astype(BF16)
                o = jnp.dot(p, v[ci][h], preferred_element_type=F32) + inter[ci][h] * rdec_ref[0, h]
                mu = jnp.mean(o, axis=-1, keepdims=True)
                dev = o - mu
                var = jnp.mean(dev * dev, axis=-1, keepdims=True)
                normed = dev * lax.rsqrt(var + EPS) * ng_ref[:, hsl[h]]
                y_ref[rsl[ci], hsl[h]] = (sg_ref[rsl[ci], hsl[h]].astype(F32) * normed).astype(BF16)

    @pl.when(c == last)
    def _():
        sout_ref[...] = s_ref[...]


def _retention(act, row0, nseq, seqlen, nb, clen, cps, s0, norm_g, emit_y, ids=None):
    w = RET_HEADS * HEAD_DIM
    rows = nb * clen * cps
    nc = seqlen // (clen * cps)
    assert nb == 1 or (nc == 1 and cps == 1)
    rb0 = row0 // rows
    rmap = lambda col: (lambda b, c: (rb0 + b * nc + c, col))
    shared = s0.shape[0] != nseq
    s_map = (lambda b, c: (0, 0, 0, 0)) if shared else (lambda b, c: (b, 0, 0, 0))
    s_block = (1 if shared else nb, RET_HEADS, HEAD_DIM, HEAD_DIM)
    out_specs = [pl.BlockSpec((nb, RET_HEADS, HEAD_DIM, HEAD_DIM), lambda b, c: (b, 0, 0, 0))]
    out_shape = [jax.ShapeDtypeStruct((nseq, RET_HEADS, HEAD_DIM, HEAD_DIM), F32)]
    if emit_y:
        out_specs = [pl.BlockSpec((rows, w), lambda b, c: (b * nc + c, 0))] + out_specs
        out_shape = [jax.ShapeDtypeStruct((nseq * seqlen, w), BF16)] + out_shape
    return dict(
        name="retention",
        kernel=functools.partial(_ret_kernel, nb=nb, clen=clen, cps=cps, emit_y=emit_y, ids=ids),
        grid=(nseq // nb, nc),
        in_specs=[
            pl.BlockSpec((rows, w), rmap(_ACT_Q)),
            pl.BlockSpec((rows, w), rmap(_ACT_K)),
            pl.BlockSpec((rows, w), rmap(_ACT_V)),
            pl.BlockSpec((rows, w), rmap(_ACT_SILU)),
            pl.BlockSpec(s_block, s_map),
            pl.BlockSpec((1, w), lambda b, c: (0, 0)),
        ],
        out_specs=out_specs,
        out_shape=out_shape,
        scratch_shapes=[pltpu.VMEM((nb, RET_HEADS, HEAD_DIM, HEAD_DIM), F32),
                        pltpu.VMEM((RET_HEADS, nb * clen, nb * clen), F32),
                        pltpu.VMEM((2, RET_HEADS, nb * clen, LANES), F32)],
        args=(act, act, act, act, s0, norm_g))


def _outproj_kernel(x_ref, ya_ref, yb_ref, sga_ref, sgb_ref, pa_ref, pb_ref, wo_ref, o_ref):
    for r0, nr in _row_chunks(o_ref.shape[0], 256):
        rows = slice(r0, r0 + nr)
        ma = jnp.dot(ya_ref[rows, :], pa_ref[...], preferred_element_type=F32)
        mb = jnp.dot(yb_ref[rows, :], pb_ref[...], preferred_element_type=F32)
        merged = sga_ref[rows, :].astype(F32) * ma + sgb_ref[rows, :].astype(F32) * mb
        o_ref[rows, :] = x_ref[rows, :] + jnp.dot(merged.astype(BF16), wo_ref[...], preferred_element_type=F32)


def _outproj(x, ya, yb, act, pa, pb, wo, bm):
    m, d = x.shape
    w = ya.shape[1]
    nd = d // 1024
    row = lambda i: (i, 0)
    const = lambda i: (0, 0)
    single = pl.Buffered(1)
    return pl.pallas_call(
        _outproj_kernel,
        name="outproj",
        grid=(m // bm,),
        in_specs=[
            pl.BlockSpec((bm, d), row),
            pl.BlockSpec((bm, w), row),
            pl.BlockSpec((bm, w), row),
            pl.BlockSpec((bm, d), lambda i: (i, _ACT_SGA // nd)),
            pl.BlockSpec((bm, d), lambda i: (i, _ACT_SGB // nd)),
            pl.BlockSpec((w, d), const, pipeline_mode=single),
            pl.BlockSpec((w, d), const, pipeline_mode=single),
            pl.BlockSpec((d, d), const, pipeline_mode=single),
        ],
        out_specs=pl.BlockSpec((bm, d), row),
        out_shape=jax.ShapeDtypeStruct((m, d), F32),
        compiler_params=_cparams(1),
    )(x, ya, yb, act, act, pa, pb, wo)


def _ffn_kernel(x_ref, g_ref, wu_ref, wd_ref, gf_ref, o_ref, hn_ref):
    j = pl.program_id(1)
    last = pl.num_programs(1) - 1

    def block(first, final):
        for r0, nr in _row_chunks(o_ref.shape[0], 512):
            rows = slice(r0, r0 + nr)
            if first:
                base = x_ref[rows, :]
                hn = _rmsnorm(base, g_ref[...]).astype(BF16)
                hn_ref[rows, :] = hn
            else:
                base = o_ref[rows, :]
                hn = hn_ref[rows, :]
            u = jnp.dot(hn, wu_ref[...], preferred_element_type=F32)
            r = jnp.square(jnp.maximum(u, 0.0)).astype(BF16)
            acc = base + jnp.dot(r, wd_ref[...], preferred_element_type=F32)
            o_ref[rows, :] = _rmsnorm(acc, gf_ref[...]) if final else acc

    pl.when(j == 0)(lambda: block(True, False))
    pl.when((j > 0) & (j < last))(lambda: block(False, False))
    pl.when(j == last)(lambda: block(False, True))


def _ffn(x, g, wu, wd, gf, bm, bf):
    m, d = x.shape
    dff = wu.shape[1]
    assert dff // bf >= 2
    return pl.pallas_call(
        _ffn_kernel,
        name="ffn",
        grid=(m // bm, dff // bf),
        in_specs=[
            pl.BlockSpec((bm, d), lambda i, j: (i, 0)),
            pl.BlockSpec((1, d), lambda i, j: (0, 0)),
            pl.BlockSpec((d, bf), lambda i, j: (0, j)),
            pl.BlockSpec((bf, d), lambda i, j: (j, 0)),
            pl.BlockSpec((1, d), lambda i, j: (0, 0)),
        ],
        out_specs=pl.BlockSpec((bm, d), lambda i, j: (i, 0)),
        out_shape=jax.ShapeDtypeStruct((m, d), F32),
        scratch_shapes=[pltpu.VMEM((bm, d), BF16)],
        compiler_params=_cparams(2),
    )(x, g, wu, wd, gf)


def _rope_tables(pos):
    inv = ROPE_BASE ** (-jnp.arange(0, HEAD_DIM, 2, dtype=F32) / HEAD_DIM)
    ang = pos.astype(F32)[:, None] * inv[None, :]
    cos, sin = jnp.cos(ang), jnp.sin(ang)
    return jnp.concatenate([cos, cos], axis=1), jnp.concatenate([-sin, sin], axis=1)


def _gate_weights(wa, wx, ba, bx):
    nblk, blk, _ = wa.shape
    per = LANES // blk
    ng = nblk // per

    def bd(wt):
        wt = wt.reshape(ng, per, blk, blk)
        eye = jnp.eye(per, dtype=wt.dtype)
        return jnp.einsum("gpcd,pq->gpcqd", wt, eye).reshape(ng, LANES, LANES)

    wbd = jnp.concatenate([bd(wa), bd(wx)], axis=2).astype(BF16)
    bg = jnp.concatenate([ba.reshape(ng, 1, LANES), bx.reshape(ng, 1, LANES)], axis=2)
    return wbd, bg


def kernel(x_prompt, x_sample, state_conv, state_lru, state_ret, meta_tokens, norm_mix_g, w_in, conv_w,
           conv_b, lru_wa, lru_ba, lru_wx, lru_bx, lru_lam, ret_norm_g, p_a, p_b, w_out, norm_ffn_g,
           w_up, w_down, norm_f_g):
    assert w_in.shape[0] == 1
    nb_p, t_p, d = x_prompt.shape
    nb_s, t_s, _ = x_sample.shape
    w = conv_w.shape[-1]
    rows_p, rows_s = nb_p * t_p, nb_s * t_s

    g_mix, g_ffn, g_f = norm_mix_g[0][None], norm_ffn_g[0][None], norm_f_g[None]
    wbd, bg = _gate_weights(lru_wa[0], lru_wx[0], lru_ba[0], lru_bx[0])
    lw = dict(conv_w=conv_w[0], conv_b=conv_b[0][None], wbd=wbd, bg=bg, lam=lru_lam[0][None])
    ret_g = ret_norm_g[0][None]

    cos_p, sin_p = _rope_tables(N_META + jnp.arange(t_p, dtype=jnp.int32))
    pos_sm = jnp.concatenate([jnp.tile(PAST_LEN + jnp.arange(t_s, dtype=jnp.int32), nb_s),
                              jnp.arange(N_META, dtype=jnp.int32)])
    cos_s, sin_s = _rope_tables(pos_sm)
    x_p2 = x_prompt.reshape(rows_p, d)
    x_s2 = x_sample.reshape(rows_s, d)
    xa_s, act_s, w_in_b = _inproj(x_s2, g_mix, w_in[0], cos_s, sin_s, rows_s + N_META, 512, 1,
                                  tail=meta_tokens.astype(x_sample.dtype))
    bm_p = 1024
    n_i, n_j = rows_p // bm_p, 8
    side_map = lambda i, j: (i, jnp.minimum(j, n_j - 1))
    later = (w_up, w_down, p_a, p_b, w_out)
    sides = tuple((wt[0], (wt.shape[1] // n_i, wt.shape[2] // n_j), side_map) for wt in later)
    xa_p, act_p, wu_b, wd_b, pa_b, pb_b, wo_b = _inproj(x_p2, g_mix, w_in_b, cos_p, sin_p, bm_p, 1024, t_p // bm_p,
                                                        side=sides)


    zc = jnp.zeros((1, CONV_W - 1, w), F32)
    zh = jnp.zeros((1, 1, w), F32)
    zs = jnp.zeros((1, RET_HEADS, HEAD_DIM, HEAD_DIM), F32)
    conv_m, h_m = _call(_lru_seq(xa_s, act_s, rows_s, 1, N_META, N_META, zc, zh, lw, emit_y=False,
                                 first_pos_zero=True))
    (s_m,) = _call(_retention(act_s, rows_s, 1, N_META, 1, N_META, 1, zs, ret_g, emit_y=False))

    lru_tr = 512
    lru_p = _lru_seq(xa_p, act_p, 0, nb_p, t_p, lru_tr, conv_m, h_m, lw, emit_y=True, first_pos_zero=False)
    steps = nb_p * (t_p // lru_tr)
    ret_s = _retention(act_s, 0, nb_s, t_s, nb_s // steps, t_s, 1, state_ret[0], ret_g, emit_y=True,
                       ids=lambda: (pl.program_id(0) * (t_p // lru_tr) + pl.program_id(1), 0, 0))
    (ya_p, conv_p, h_p), (yb_s, s_s) = _call_pair(lru_p, ret_s, "lru_seq_retention")
    yb_p, s_p = _call(_retention(act_p, 0, nb_p, t_p, 1, CHUNK, 4, s_m, ret_g, emit_y=True))
    conv0_t = jnp.transpose(state_conv[0], (1, 0, 2))
    ya_s, conv_s_t, h_s = _lru_step(xa_s, act_s, nb_s, t_s, conv0_t, state_lru[0], lw)

    x1_p = _outproj(x_p2, ya_p, yb_p, act_p, pa_b, pb_b, wo_b, 512)
    x1_s = _outproj(x_s2, ya_s, yb_s, act_s, pa_b, pb_b, wo_b, 512)
    y_s = _ffn(x1_s, g_ffn, wu_b, wd_b, g_f, 1024, 1024)
    x1_p, y_s = lax.optimization_barrier((x1_p, y_s))
    y_p = _ffn(x1_p, g_ffn, wu_b, wd_b, g_f, 1024, 1024)

    return (y_p.reshape(nb_p, t_p, d), y_s.reshape(nb_s, t_s, d),
            conv_p[None], h_p.reshape(1, nb_p, w), s_p[None],
            jnp.transpose(conv_s_t, (1, 0, 2))[None], h_s[None], s_s[None])
```

```python
import functools
import math

import jax
import jax.numpy as jnp
import numpy as np
from jax import lax
from jax.experimental import pallas as pl
from jax.experimental.pallas import tpu as pltpu

F32 = jnp.float32
BF16 = jnp.bfloat16

N_META = 16
PAST_LEN = 16384
LRU_BLOCKS = 16
CONV_W = 4
LRU_C = 8.0
RET_HEADS = 8
HEAD_DIM = 128
CHUNK = 128
ROPE_BASE = 10000.0
EPS = 1e-6

LANES = 128
VMEM_LIMIT = 56 << 20


def _cparams(n_axes):
    return pltpu.CompilerParams(dimension_semantics=("arbitrary",) * n_axes, vmem_limit_bytes=VMEM_LIMIT)


def _call(spec):
    return pl.pallas_call(
        spec["kernel"], name=spec["name"], grid=spec["grid"], in_specs=spec["in_specs"],
        out_specs=spec["out_specs"], out_shape=spec["out_shape"], scratch_shapes=spec["scratch_shapes"],
        compiler_params=_cparams(len(spec["grid"])))(*spec["args"])


def _call_pair(a, b, name):
    n0, n1 = a["grid"]
    assert b["grid"] == (n0 * n1, 1)

    def remap(spec):
        return pl.BlockSpec(spec.block_shape, lambda i, j, m=spec.index_map: m(i * n1 + j, 0))

    counts = [len(a["in_specs"]), len(b["in_specs"]), len(a["out_specs"]), len(b["out_specs"]),
              len(a["scratch_shapes"]), len(b["scratch_shapes"])]

    def kernel(*refs):
        groups, pos = [], 0
        for n in counts:
            groups.append(refs[pos:pos + n])
            pos += n
        a["kernel"](*groups[0], *groups[2], *groups[4])
        b["kernel"](*groups[1], *groups[3], *groups[5])

    outs = pl.pallas_call(
        kernel, name=name, grid=a["grid"],
        in_specs=list(a["in_specs"]) + [remap(sp) for sp in b["in_specs"]],
        out_specs=list(a["out_specs"]) + [remap(sp) for sp in b["out_specs"]],
        out_shape=list(a["out_shape"]) + list(b["out_shape"]),
        scratch_shapes=list(a["scratch_shapes"]) + list(b["scratch_shapes"]),
        compiler_params=_cparams(2))(*a["args"], *b["args"])
    return outs[:counts[2]], outs[counts[2]:]


def _rmsnorm(x, g):
    return x * lax.rsqrt(jnp.mean(x * x, axis=-1, keepdims=True) + EPS) * g


def _sigmoid(x):
    return 0.5 * jnp.tanh(0.5 * x) + 0.5


def _row_chunks(rows, size):
    n = max(rows // size, 1)
    return [(c * size, size if c < n - 1 else rows - c * size) for c in range(n)]


_COL_GATE, _COL_Q, _COL_K, _COL_V, _COL_G, _COL_GATE_A = 1, 2, 3, 4, 5, 6
_ACT_GELU, _ACT_Q, _ACT_K, _ACT_V, _ACT_SGA, _ACT_SGB, _ACT_SILU = 0, 1, 2, 3, 4, 6, 8


def _act_block(kind):
    return jnp.where(kind <= _COL_V, kind - 1, jnp.where(kind == _COL_G, _ACT_SILU, kind - 2))


def _inproj_kernel(x_ref, g_ref, w_ref, cos_ref, sin_ref, *refs, nper, cast_w, n_side, has_tail):
    if has_tail:
        tail_ref, refs = refs[0], refs[1:]
    side_src, refs = refs[:n_side], refs[n_side:]
    xa_ref, act_ref = refs[0], refs[1]
    refs = refs[2:]
    if cast_w:
        wb_ref, refs = refs[0], refs[1:]
    side_dst, (xn_ref,) = refs[:n_side], refs[n_side:]
    j = pl.program_id(1)
    kind = j // nper
    bn = xa_ref.shape[1]

    for src, dst in zip(side_src, side_dst):
        dst[...] = src[...].astype(BF16)
    if cast_w:
        wb_ref[...] = w_ref[...].astype(BF16)
        w_bf = wb_ref
    else:
        w_bf = w_ref

    def project(epilogue, normalize=False):
        if normalize:
            nx = x_ref.shape[0]
            xn_ref[0:nx, :] = _rmsnorm(x_ref[...], g_ref[...]).astype(BF16)
            if has_tail:
                xn_ref[nx:, :] = _rmsnorm(tail_ref[...], g_ref[...]).astype(BF16)
        epilogue(jnp.dot(xn_ref[...], w_bf[...], preferred_element_type=F32))

    def to_xa(acc):
        xa_ref[...] = acc

    def rope(scale):
        def epilogue(acc):
            cos, sin = cos_ref[...], sin_ref[...]
            for h in range(bn // HEAD_DIM):
                a = acc[:, h * HEAD_DIM:(h + 1) * HEAD_DIM]
                r = a * cos + pltpu.roll(a, HEAD_DIM // 2, axis=1) * sin
                if scale is not None:
                    r = r * scale
                act_ref[:, h * HEAD_DIM:(h + 1) * HEAD_DIM] = r.astype(BF16)
        return epilogue

    def elementwise(fn):
        def epilogue(acc):
            act_ref[...] = fn(acc).astype(BF16)
        return epilogue

    pl.when(j == 0)(lambda: project(to_xa, normalize=True))
    if nper > 1:
        pl.when((j > 0) & (kind == 0))(lambda: project(to_xa))
    pl.when(kind == _COL_GATE)(lambda: project(elementwise(jax.nn.gelu)))
    pl.when(kind == _COL_Q)(lambda: project(rope(None)))
    pl.when(kind == _COL_K)(lambda: project(rope(HEAD_DIM ** -0.5)))
    pl.when(kind == _COL_V)(lambda: project(elementwise(lambda a: a)))
    pl.when(kind == _COL_G)(lambda: project(elementwise(lambda a: a * _sigmoid(a))))
    pl.when(kind >= _COL_GATE_A)(lambda: project(elementwise(_sigmoid)))


def _inproj(x, g, w, cos, sin, bm, bn, table_blocks, side=(), tail=None):
    xrows, d = x.shape
    m = xrows if tail is None else xrows + tail.shape[0]
    assert tail is None or m == bm
    n = w.shape[1]
    width = 1024
    nper = width // bn
    cast_w = w.dtype != BF16

    def act_map(i, j):
        kind = j // nper
        return i, jnp.where(kind == 0, 0, _act_block(kind) * nper + j % nper)

    in_specs = [
        pl.BlockSpec((min(bm, xrows), d), lambda i, j: (i, 0)),
        pl.BlockSpec((1, d), lambda i, j: (0, 0)),
        pl.BlockSpec((d, bn), lambda i, j: (0, j)),
        pl.BlockSpec((bm, LANES), lambda i, j: (i % table_blocks, 0)),
        pl.BlockSpec((bm, LANES), lambda i, j: (i % table_blocks, 0)),
    ]
    if tail is not None:
        in_specs.append(pl.BlockSpec(tail.shape, lambda i, j: (0, 0)))
    out_specs = [
        pl.BlockSpec((bm, bn), lambda i, j: (i, jnp.minimum(j, nper - 1))),
        pl.BlockSpec((bm, bn), act_map),
    ]
    out_shape = [jax.ShapeDtypeStruct((m, width), F32), jax.ShapeDtypeStruct((m, n - width), BF16)]
    if cast_w:
        out_specs.append(pl.BlockSpec((d, bn), lambda i, j: (0, j)))
        out_shape.append(jax.ShapeDtypeStruct(w.shape, BF16))
    for arr, block, index_map in side:
        in_specs.append(pl.BlockSpec(block, index_map))
        out_specs.append(pl.BlockSpec(block, index_map))
        out_shape.append(jax.ShapeDtypeStruct(arr.shape, BF16))
    return pl.pallas_call(
        functools.partial(_inproj_kernel, nper=nper, cast_w=cast_w, n_side=len(side), has_tail=tail is not None),
        name="inproj",
        grid=(m // bm, n // bn),
        in_specs=in_specs,
        out_specs=out_specs,
        out_shape=out_shape,
        scratch_shapes=[pltpu.VMEM((bm, d), BF16)],
        compiler_params=_cparams(2),
    )(x, g, w, cos, sin, *([] if tail is None else [tail]), *[arr for arr, _, _ in side])


def _lru_coeffs(xc, wbd_ref, bg_ref, cl_row, group):
    gates = jnp.dot(xc.astype(BF16), wbd_ref[group], preferred_element_type=F32) + bg_ref[group]
    rec = _sigmoid(gates[:, :LANES])
    ing = _sigmoid(gates[:, LANES:])
    log_a = rec * cl_row
    a = jnp.exp(log_a)
    mult = jnp.sqrt(jnp.tanh(-log_a) * (a * a + 1.0))
    return a, mult, ing


def _lru_seq_kernel(xa_ref, ga_ref, conv0_ref, h0_ref, cw_ref, cb_ref, wbd_ref, bg_ref, lam_ref,
                    *refs, tr, emit_y, first_pos_zero):
    if emit_y:
        ya_ref, convout_ref, hout_ref, halo_ref, h_ref, a_ref, b_ref = refs
    else:
        convout_ref, hout_ref, halo_ref, h_ref, a_ref, b_ref = refs
    t = pl.program_id(1)
    halo = 8

    @pl.when(t == 0)
    def _():
        halo_ref[0:halo - 3, :] = jnp.zeros((halo - 3, halo_ref.shape[1]), F32)
        halo_ref[halo - 3:halo, :] = conv0_ref[0]
        h_ref[...] = h0_ref[0]

    xa = xa_ref[...]
    cw = [cw_ref[CONV_W - 1 - k:CONV_W - k, :] for k in range(CONV_W)]
    xc = cb_ref[...] + cw[0] * xa + sum(cw[k] * pltpu.roll(xa, k, axis=0) for k in range(1, CONV_W))
    window = jnp.concatenate([halo_ref[0:halo, :], xa[0:halo, :]], axis=0)
    first = cb_ref[...] + sum(cw[k] * window[halo - k:2 * halo - k, :] for k in range(CONV_W))
    xc = jnp.concatenate([first, xc[halo:, :]], axis=0)
    halo_ref[0:halo, :] = xa[tr - halo:, :]

    cl = LRU_C * jax.nn.log_sigmoid(lam_ref[...])
    for g in range(xc.shape[1] // LANES):
        sl = slice(g * LANES, (g + 1) * LANES)
        a, mult, ing = _lru_coeffs(xc[:, sl], wbd_ref, bg_ref, cl[:, sl], g)
        if first_pos_zero:
            row = lax.broadcasted_iota(jnp.int32, a.shape, 0) + t * tr
            mult = jnp.where(row == 0, 1.0, mult)
        a_ref[:, sl] = a
        b_ref[:, sl] = mult * ing * xc[:, sl]

    h = h_ref[...]
    for r in range(tr):
        h = a_ref[r:r + 1, :] * h + b_ref[r:r + 1, :]
        halo_ref[halo + r:halo + r + 1, :] = h
    h_ref[...] = h
    if emit_y:
        ya_ref[...] = (halo_ref[halo:halo + tr, :] * ga_ref[...].astype(F32)).astype(BF16)

    @pl.when(t == pl.num_programs(1) - 1)
    def _():
        convout_ref[0] = halo_ref[halo - 3:halo, :]
        hout_ref[0] = h_ref[...]


def _lru_seq(xa, act, row0, nseq, seqlen, tr, conv0, h0, lw, emit_y, first_pos_zero):
    w = xa.shape[1]
    nt = seqlen // tr
    rb0 = row0 // tr
    ng = w // LANES
    state_map = (lambda b, t: (b, 0, 0)) if conv0.shape[0] == nseq else (lambda b, t: (0, 0, 0))
    const2 = lambda b, t: (0, 0)
    const3 = lambda b, t: (0, 0, 0)
    out_specs = [pl.BlockSpec((1, CONV_W - 1, w), lambda b, t: (b, 0, 0)),
                 pl.BlockSpec((1, 1, w), lambda b, t: (b, 0, 0))]
    out_shape = [jax.ShapeDtypeStruct((nseq, CONV_W - 1, w), F32), jax.ShapeDtypeStruct((nseq, 1, w), F32)]
    if emit_y:
        out_specs = [pl.BlockSpec((tr, w), lambda b, t: (b * nt + t, 0))] + out_specs
        out_shape = [jax.ShapeDtypeStruct((nseq * seqlen, w), BF16)] + out_shape
    return dict(
        name="lru_seq",
        kernel=functools.partial(_lru_seq_kernel, tr=tr, emit_y=emit_y, first_pos_zero=first_pos_zero),
        grid=(nseq, nt),
        in_specs=[
            pl.BlockSpec((tr, w), lambda b, t: (rb0 + b * nt + t, 0)),
            pl.BlockSpec((tr, w), lambda b, t: (rb0 + b * nt + t, _ACT_GELU)),
            pl.BlockSpec((1, CONV_W - 1, w), state_map),
            pl.BlockSpec((1, 1, w), state_map),
            pl.BlockSpec((CONV_W, w), const2),
            pl.BlockSpec((1, w), const2),
            pl.BlockSpec((ng, LANES, 2 * LANES), const3),
            pl.BlockSpec((ng, 1, 2 * LANES), const3),
            pl.BlockSpec((1, w), const2),
        ],
        out_specs=out_specs,
        out_shape=out_shape,
        scratch_shapes=[pltpu.VMEM((8 + tr, w), F32), pltpu.VMEM((1, w), F32),
                        pltpu.VMEM((tr, w), F32), pltpu.VMEM((tr, w), F32)],
        args=(xa, act, conv0, h0, lw["conv_w"], lw["conv_b"], lw["wbd"], lw["bg"], lw["lam"]))


def _lru_step_kernel(xa_ref, ga_ref, conv0_ref, h0_ref, cw_ref, cb_ref, wbd_ref, bg_ref, lam_ref,
                     ya_ref, convout_ref, hout_ref, hs_ref, *, nseq, seqlen):
    cl = LRU_C * jax.nn.log_sigmoid(lam_ref[...])
    full = [conv0_ref[j] for j in range(CONV_W - 1)]
    full += [xa_ref[pl.ds(t, nseq, stride=seqlen), :] for t in range(seqlen)]
    h = h0_ref[...]
    for t in range(seqlen):
        xc = cb_ref[...] + cw_ref[0:1, :] * full[t]
        for j in range(1, CONV_W):
            xc = xc + cw_ref[j:j + 1, :] * full[t + j]
        a, mult, ing = _lru_coeffs(xc, wbd_ref, bg_ref, cl, 0)
        h = a * h + mult * ing * xc
        hs_ref[pl.ds(t, nseq, stride=seqlen), :] = h
    ya_ref[...] = (hs_ref[...] * ga_ref[...].astype(F32)).astype(BF16)
    for j in range(CONV_W - 1):
        convout_ref[j] = full[seqlen + j]
    hout_ref[...] = h


def _lru_step(xa, act, nseq, seqlen, conv0_t, h0, lw):
    w = xa.shape[1]
    rows = nseq * seqlen
    ng = w // LANES
    col = lambda g: (0, g)
    return pl.pallas_call(
        functools.partial(_lru_step_kernel, nseq=nseq, seqlen=seqlen),
        name="lru_step",
        grid=(ng,),
        in_specs=[
            pl.BlockSpec((rows, LANES), col),
            pl.BlockSpec((rows, LANES), col),
            pl.BlockSpec((CONV_W - 1, nseq, LANES), lambda g: (0, 0, g)),
            pl.BlockSpec((nseq, LANES), col),
            pl.BlockSpec((CONV_W, LANES), col),
            pl.BlockSpec((1, LANES), col),
            pl.BlockSpec((1, LANES, 2 * LANES), lambda g: (g, 0, 0)),
            pl.BlockSpec((1, 1, 2 * LANES), lambda g: (g, 0, 0)),
            pl.BlockSpec((1, LANES), col),
        ],
        out_specs=[
            pl.BlockSpec((rows, LANES), col),
            pl.BlockSpec((CONV_W - 1, nseq, LANES), lambda g: (0, 0, g)),
            pl.BlockSpec((nseq, LANES), col),
        ],
        out_shape=[jax.ShapeDtypeStruct((rows, w), BF16),
                   jax.ShapeDtypeStruct((CONV_W - 1, nseq, w), F32),
                   jax.ShapeDtypeStruct((nseq, w), F32)],
        scratch_shapes=[pltpu.VMEM((rows, LANES), F32)],
        compiler_params=_cparams(1),
    )(xa, act, conv0_t, h0, lw["conv_w"], lw["conv_b"], lw["wbd"], lw["bg"], lw["lam"])


def _log_gamma(h):
    return math.log1p(-(2.0 ** (-5.0 - h)))


def _ret_kernel(q_ref, k_ref, v_ref, sg_ref, s0_ref, ng_ref, *refs, nb, clen, cps, emit_y, ids):
    if emit_y:
        y_ref, sout_ref, s_ref, dec_ref, rdec_ref = refs
    else:
        sout_ref, s_ref, dec_ref, rdec_ref = refs
    group, c, last = ids() if ids is not None else (pl.program_id(0), pl.program_id(1), pl.num_programs(1) - 1)
    rows = nb * clen
    heads = range(RET_HEADS)
    chunks = range(cps)

    @pl.when((group == 0) & (c == 0))
    def _():
        li = lax.broadcasted_iota(jnp.int32, (rows, rows), 0)
        mi = lax.broadcasted_iota(jnp.int32, (rows, rows), 1)
        keep = (li // clen == mi // clen) & (li >= mi)
        diff = jnp.where(keep, li - mi, 0).astype(F32)
        tpos = (lax.broadcasted_iota(jnp.int32, (rows, LANES), 0) % clen).astype(F32)
        for h in heads:
            lg = _log_gamma(h)
            dec_ref[h] = jnp.where(keep, jnp.exp(lg * diff), 0.0)
            rdec_ref[0, h] = jnp.exp(lg * (tpos + 1.0))
            rdec_ref[1, h] = jnp.exp(lg * (clen - 1.0 - tpos))

    @pl.when(c == 0)
    def _():
        for n in range(nb):
            s_ref[n] = s0_ref[n if s0_ref.shape[0] == nb else 0]

    if nb > 1:
        seq_of_row = lax.broadcasted_iota(jnp.int32, (rows, nb * HEAD_DIM), 0) // clen
        seq_of_col = lax.broadcasted_iota(jnp.int32, (rows, nb * HEAD_DIM), 1) // HEAD_DIM
        own = seq_of_row == seq_of_col
    tn = (((0,), (0,)), ((), ()))
    nt = (((1,), (1,)), ((), ()))
    hsl = [slice(h * HEAD_DIM, (h + 1) * HEAD_DIM) for h in heads]
    rsl = [slice(ci * rows, (ci + 1) * rows) for ci in chunks]
    q = [[q_ref[rsl[ci], sl] for sl in hsl] for ci in chunks]
    k = [[k_ref[rsl[ci], sl] for sl in hsl] for ci in chunks]
    v = [[v_ref[rsl[ci], sl] for sl in hsl] for ci in chunks]

    if emit_y:
        scores = [[lax.dot_general(q[ci][h], k[ci][h], nt, preferred_element_type=F32) for h in heads]
                  for ci in chunks]
    upd = []
    for ci in chunks:
        upd.append([])
        for h in heads:
            kd = (k[ci][h].astype(F32) * rdec_ref[1, h]).astype(BF16)
            if nb > 1:
                v_bd = jnp.where(own, jnp.concatenate([v[ci][h].astype(F32)] * nb, axis=1), 0.0).astype(BF16)
            else:
                v_bd = v[ci][h]
            upd[ci].append(lax.dot_general(kd, v_bd, tn, preferred_element_type=F32))

    state = [[[s_ref[n, h] for n in range(nb)] for h in heads]]
    for ci in chunks:
        state.append([[math.exp(clen * _log_gamma(h)) * state[ci][h][n]
                       + upd[ci][h][:, n * HEAD_DIM:(n + 1) * HEAD_DIM] for n in range(nb)] for h in heads])
    for h in heads:
        for n in range(nb):
            s_ref[n, h] = state[cps][h][n]

    if emit_y:
        inter = []
        for ci in chunks:
            inter.append([])
            for h in heads:
                s_cat = jnp.concatenate(state[ci][h], axis=1) if nb > 1 else state[ci][h][0]
                qs = jnp.dot(q[ci][h], s_cat.astype(BF16), preferred_element_type=F32)
                if nb > 1:
                    qs = jnp.where(own, qs, 0.0)
                    qs = sum(qs[:, n * HEAD_DIM:(n + 1) * HEAD_DIM] for n in range(nb))
                inter[ci].append(qs)
        for ci in chunks:
            for h in heads:
                p = (scores[ci][h] * dec_ref[h]).astype(BF16)
                o = jnp.dot(p, v[ci][h], preferred_element_type=F32) + inter[ci][h] * rdec_ref[0, h]
                mu = jnp.mean(o, axis=-1, keepdims=True)
                dev = o - mu
                var = jnp.mean(dev * dev, axis=-1, keepdims=True)
                normed = dev * lax.rsqrt(var + EPS) * ng_ref[:, hsl[h]]
                y_ref[rsl[ci], hsl[h]] = (sg_ref[rsl[ci], hsl[h]].astype(F32) * normed).astype(BF16)

    @pl.when(c == last)
    def _():
        sout_ref[...] = s_ref[...]


def _retention(act, row0, nseq, seqlen, nb, clen, cps, s0, norm_g, emit_y, ids=None):
    w = RET_HEADS * HEAD_DIM
    rows = nb * clen * cps
    nc = seqlen // (clen * cps)
    assert nb == 1 or (nc == 1 and cps == 1)
    rb0 = row0 // rows
    rmap = lambda col: (lambda b, c: (rb0 + b * nc + c, col))
    shared = s0.shape[0] != nseq
    s_map = (lambda b, c: (0, 0, 0, 0)) if shared else (lambda b, c: (b, 0, 0, 0))
    s_block = (1 if shared else nb, RET_HEADS, HEAD_DIM, HEAD_DIM)
    out_specs = [pl.BlockSpec((nb, RET_HEADS, HEAD_DIM, HEAD_DIM), lambda b, c: (b, 0, 0, 0))]
    out_shape = [jax.ShapeDtypeStruct((nseq, RET_HEADS, HEAD_DIM, HEAD_DIM), F32)]
    if emit_y:
        out_specs = [pl.BlockSpec((rows, w), lambda b, c: (b * nc + c, 0))] + out_specs
        out_shape = [jax.ShapeDtypeStruct((nseq * seqlen, w), BF16)] + out_shape
    return dict(
        name="retention",
        kernel=functools.partial(_ret_kernel, nb=nb, clen=clen, cps=cps, emit_y=emit_y, ids=ids),
        grid=(nseq // nb, nc),
        in_specs=[
            pl.BlockSpec((rows, w), rmap(_ACT_Q)),
            pl.BlockSpec((rows, w), rmap(_ACT_K)),
            pl.BlockSpec((rows, w), rmap(_ACT_V)),
            pl.BlockSpec((rows, w), rmap(_ACT_SILU)),
            pl.BlockSpec(s_block, s_map),
            pl.BlockSpec((1, w), lambda b, c: (0, 0)),
        ],
        out_specs=out_specs,
        out_shape=out_shape,
        scratch_shapes=[pltpu.VMEM((nb, RET_HEADS, HEAD_DIM, HEAD_DIM), F32),
                        pltpu.VMEM((RET_HEADS, nb * clen, nb * clen), F32),
                        pltpu.VMEM((2, RET_HEADS, nb * clen, LANES), F32)],
        args=(act, act, act, act, s0, norm_g))


def _outproj_kernel(x_ref, ya_ref, yb_ref, sga_ref, sgb_ref, pa_ref, pb_ref, wo_ref, o_ref):
    for r0, nr in _row_chunks(o_ref.shape[0], 256):
        rows = slice(r0, r0 + nr)
        ma = jnp.dot(ya_ref[rows, :], pa_ref[...], preferred_element_type=F32)
        mb = jnp.dot(yb_ref[rows, :], pb_ref[...], preferred_element_type=F32)
        merged = sga_ref[rows, :].astype(F32) * ma + sgb_ref[rows, :].astype(F32) * mb
        o_ref[rows, :] = x_ref[rows, :] + jnp.dot(merged.astype(BF16), wo_ref[...], preferred_element_type=F32)


def _outproj(x, ya, yb, act, pa, pb, wo, bm):
    m, d = x.shape
    w = ya.shape[1]
    nd = d // 1024
    row = lambda i: (i, 0)
    const = lambda i: (0, 0)
    single = pl.Buffered(1)
    return pl.pallas_call(
        _outproj_kernel,
        name="outproj",
        grid=(m // bm,),
        in_specs=[
            pl.BlockSpec((bm, d), row),
            pl.BlockSpec((bm, w), row),
            pl.BlockSpec((bm, w), row),
            pl.BlockSpec((bm, d), lambda i: (i, _ACT_SGA // nd)),
            pl.BlockSpec((bm, d), lambda i: (i, _ACT_SGB // nd)),
            pl.BlockSpec((w, d), const, pipeline_mode=single),
            pl.BlockSpec((w, d), const, pipeline_mode=single),
            pl.BlockSpec((d, d), const, pipeline_mode=single),
        ],
        out_specs=pl.BlockSpec((bm, d), row),
        out_shape=jax.ShapeDtypeStruct((m, d), F32),
        compiler_params=_cparams(1),
    )(x, ya, yb, act, act, pa, pb, wo)


def _ffn_kernel(x_ref, g_ref, wu_ref, wd_ref, gf_ref, o_ref, hn_ref):
    j = pl.program_id(1)
    last = pl.num_programs(1) - 1

    def block(first, final):
        for r0, nr in _row_chunks(o_ref.shape[0], 512):
            rows = slice(r0, r0 + nr)
            if first:
                base = x_ref[rows, :]
                hn = _rmsnorm(base, g_ref[...]).astype(BF16)
                hn_ref[rows, :] = hn
            else:
                base = o_ref[rows, :]
                hn = hn_ref[rows, :]
            u = jnp.dot(hn, wu_ref[...], preferred_element_type=F32)
            r = jnp.square(jnp.maximum(u, 0.0)).astype(BF16)
            acc = base + jnp.dot(r, wd_ref[...], preferred_element_type=F32)
            o_ref[rows, :] = _rmsnorm(acc, gf_ref[...]) if final else acc

    pl.when(j == 0)(lambda: block(True, False))
    pl.when((j > 0) & (j < last))(lambda: block(False, False))
    pl.when(j == last)(lambda: block(False, True))


def _ffn(x, g, wu, wd, gf, bm, bf):
    m, d = x.shape
    dff = wu.shape[1]
    assert dff // bf >= 2
    return pl.pallas_call(
        _ffn_kernel,
        name="ffn",
        grid=(m // bm, dff // bf),
        in_specs=[
            pl.BlockSpec((bm, d), lambda i, j: (i, 0)),
            pl.BlockSpec((1, d), lambda i, j: (0, 0)),
            pl.BlockSpec((d, bf), lambda i, j: (0, j)),
            pl.BlockSpec((bf, d), lambda i, j: (j, 0)),
            pl.BlockSpec((1, d), lambda i, j: (0, 0)),
        ],
        out_specs=pl.BlockSpec((bm, d), lambda i, j: (i, 0)),
        out_shape=jax.ShapeDtypeStruct((m, d), F32),
        scratch_shapes=[pltpu.VMEM((bm, d), BF16)],
        compiler_params=_cparams(2),
    )(x, g, wu, wd, gf)


def _rope_tables(pos):
    inv = ROPE_BASE ** (-np.arange(0, HEAD_DIM, 2, dtype=np.float64) / HEAD_DIM)
    ang = np.asarray(pos, np.float64)[:, None] * inv[None, :]
    cos, sin = np.cos(ang), np.sin(ang)
    return (jnp.asarray(np.concatenate([cos, cos], axis=1), F32),
            jnp.asarray(np.concatenate([-sin, sin], axis=1), F32))


def _gate_weights(wa, wx, ba, bx):
    nblk, blk, _ = wa.shape
    per = LANES // blk
    ng = nblk // per

    def bd(wt):
        wt = wt.reshape(ng, per, blk, blk)
        eye = jnp.eye(per, dtype=wt.dtype)
        return jnp.einsum("gpcd,pq->gpcqd", wt, eye).reshape(ng, LANES, LANES)

    wbd = jnp.concatenate([bd(wa), bd(wx)], axis=2).astype(BF16)
    bg = jnp.concatenate([ba.reshape(ng, 1, LANES), bx.reshape(ng, 1, LANES)], axis=2)
    return wbd, bg


def kernel(x_prompt, x_sample, state_conv, state_lru, state_ret, meta_tokens, norm_mix_g, w_in, conv_w,
           conv_b, lru_wa, lru_ba, lru_wx, lru_bx, lru_lam, ret_norm_g, p_a, p_b, w_out, norm_ffn_g,
           w_up, w_down, norm_f_g):
    assert w_in.shape[0] == 1
    nb_p, t_p, d = x_prompt.shape
    nb_s, t_s, _ = x_sample.shape
    w = conv_w.shape[-1]
    rows_p, rows_s = nb_p * t_p, nb_s * t_s

    g_mix, g_ffn, g_f = norm_mix_g[0][None], norm_ffn_g[0][None], norm_f_g[None]
    wbd, bg = _gate_weights(lru_wa[0], lru_wx[0], lru_ba[0], lru_bx[0])
    lw = dict(conv_w=conv_w[0], conv_b=conv_b[0][None], wbd=wbd, bg=bg, lam=lru_lam[0][None])
    ret_g = ret_norm_g[0][None]

    cos_p, sin_p = _rope_tables(N_META + np.arange(t_p))
    cos_s, sin_s = _rope_tables(np.concatenate([np.tile(PAST_LEN + np.arange(t_s), nb_s), np.arange(N_META)]))
    x_p2 = x_prompt.reshape(rows_p, d)
    x_s2 = x_sample.reshape(rows_s, d)
    xa_s, act_s, w_in_b = _inproj(x_s2, g_mix, w_in[0], cos_s, sin_s, rows_s + N_META, 512, 1,
                                  tail=meta_tokens.astype(x_sample.dtype))
    bm_p = 1024
    n_i, n_j = rows_p // bm_p, 8
    side_map = lambda i, j: (i, jnp.minimum(j, n_j - 1))
    later = (w_up, w_down, p_a, p_b, w_out)
    sides = tuple((wt[0], (wt.shape[1] // n_i, wt.shape[2] // n_j), side_map) for wt in later)
    xa_p, act_p, wu_b, wd_b, pa_b, pb_b, wo_b = _inproj(x_p2, g_mix, w_in_b, cos_p, sin_p, bm_p, 1024, t_p // bm_p,
                                                        side=sides)


    zc = jnp.zeros((1, CONV_W - 1, w), F32)
    zh = jnp.zeros((1, 1, w), F32)
    zs = jnp.zeros((1, RET_HEADS, HEAD_DIM, HEAD_DIM), F32)
    conv_m, h_m = _call(_lru_seq(xa_s, act_s, rows_s, 1, N_META, N_META, zc, zh, lw, emit_y=False,
                                 first_pos_zero=True))
    (s_m,) = _call(_retention(act_s, rows_s, 1, N_META, 1, N_META, 1, zs, ret_g, emit_y=False))

    lru_tr = 512
    lru_p = _lru_seq(xa_p, act_p, 0, nb_p, t_p, lru_tr, conv_m, h_m, lw, emit_y=True, first_pos_zero=False)
    steps = nb_p * (t_p // lru_tr)
    ret_s = _retention(act_s, 0, nb_s, t_s, nb_s // steps, t_s, 1, state_ret[0], ret_g, emit_y=True,
                       ids=lambda: (pl.program_id(0) * (t_p // lru_tr) + pl.program_id(1), 0, 0))
    (ya_p, conv_p, h_p), (yb_s, s_s) = _call_pair(lru_p, ret_s, "lru_seq_retention")
    yb_p, s_p = _call(_retention(act_p, 0, nb_p, t_p, 1, CHUNK, 4, s_m, ret_g, emit_y=True))
    conv0_t = jnp.transpose(state_conv[0], (1, 0, 2))
    ya_s, conv_s_t, h_s = _lru_step(xa_s, act_s, nb_s, t_s, conv0_t, state_lru[0], lw)

    x1_p = _outproj(x_p2, ya_p, yb_p, act_p, pa_b, pb_b, wo_b, 512)
    x1_s = _outproj(x_s2, ya_s, yb_s, act_s, pa_b, pb_b, wo_b, 512)
    y_s = _ffn(x1_s, g_ffn, wu_b, wd_b, g_f, 1024, 1024)
    x1_p, y_s = lax.optimization_barrier((x1_p, y_s))
    y_p = _ffn(x1_p, g_ffn, wu_b, wd_b, g_f, 1024, 1024)

    return (y_p.reshape(nb_p, t_p, d), y_s.reshape(nb_s, t_s, d),
            conv_p[None], h_p.reshape(1, nb_p, w), s_p[None],
            jnp.transpose(conv_s_t, (1, 0, 2))[None], h_s[None], s_s[None])
```

```python
import functools
import math

import jax
import jax.numpy as jnp
import numpy as np
from jax import lax
from jax.experimental import pallas as pl
from jax.experimental.pallas import tpu as pltpu

F32 = jnp.float32
BF16 = jnp.bfloat16

N_META = 16
PAST_LEN = 16384
LRU_BLOCKS = 16
CONV_W = 4
LRU_C = 8.0
RET_HEADS = 8
HEAD_DIM = 128
CHUNK = 128
ROPE_BASE = 10000.0
EPS = 1e-6

LANES = 128
VMEM_LIMIT = 56 << 20


def _cparams(n_axes):
    return pltpu.CompilerParams(dimension_semantics=("arbitrary",) * n_axes, vmem_limit_bytes=VMEM_LIMIT)


def _call(spec):
    return pl.pallas_call(
        spec["kernel"], name=spec["name"], grid=spec["grid"], in_specs=spec["in_specs"],
        out_specs=spec["out_specs"], out_shape=spec["out_shape"], scratch_shapes=spec["scratch_shapes"],
        compiler_params=_cparams(len(spec["grid"])))(*spec["args"])


def _call_pair(a, b, name):
    n0, n1 = a["grid"]
    assert b["grid"] == (n0 * n1, 1)

    def remap(spec):
        return pl.BlockSpec(spec.block_shape, lambda i, j, m=spec.index_map: m(i * n1 + j, 0))

    counts = [len(a["in_specs"]), len(b["in_specs"]), len(a["out_specs"]), len(b["out_specs"]),
              len(a["scratch_shapes"]), len(b["scratch_shapes"])]

    def kernel(*refs):
        groups, pos = [], 0
        for n in counts:
            groups.append(refs[pos:pos + n])
            pos += n
        a["kernel"](*groups[0], *groups[2], *groups[4])
        b["kernel"](*groups[1], *groups[3], *groups[5])

    outs = pl.pallas_call(
        kernel, name=name, grid=a["grid"],
        in_specs=list(a["in_specs"]) + [remap(sp) for sp in b["in_specs"]],
        out_specs=list(a["out_specs"]) + [remap(sp) for sp in b["out_specs"]],
        out_shape=list(a["out_shape"]) + list(b["out_shape"]),
        scratch_shapes=list(a["scratch_shapes"]) + list(b["scratch_shapes"]),
        compiler_params=_cparams(2))(*a["args"], *b["args"])
    return outs[:counts[2]], outs[counts[2]:]


def _rmsnorm(x, g):
    return x * lax.rsqrt(jnp.mean(x * x, axis=-1, keepdims=True) + EPS) * g


def _sigmoid(x):
    return 0.5 * jnp.tanh(0.5 * x) + 0.5


def _row_chunks(rows, size):
    n = max(rows // size, 1)
    return [(c * size, size if c < n - 1 else rows - c * size) for c in range(n)]


_COL_GATE, _COL_Q, _COL_K, _COL_V, _COL_G, _COL_GATE_A = 1, 2, 3, 4, 5, 6
_ACT_GELU, _ACT_Q, _ACT_K, _ACT_V, _ACT_SGA, _ACT_SGB, _ACT_SILU = 0, 1, 2, 3, 4, 6, 8


def _act_block(kind):
    return jnp.where(kind <= _COL_V, kind - 1, jnp.where(kind == _COL_G, _ACT_SILU, kind - 2))


def _inproj_kernel(x_ref, g_ref, w_ref, cos_ref, sin_ref, *refs, nper, cast_w, n_side, has_tail):
    if has_tail:
        tail_ref, refs = refs[0], refs[1:]
    side_src, refs = refs[:n_side], refs[n_side:]
    xa_ref, act_ref = refs[0], refs[1]
    refs = refs[2:]
    if cast_w:
        wb_ref, refs = refs[0], refs[1:]
    side_dst, (xn_ref,) = refs[:n_side], refs[n_side:]
    j = pl.program_id(1)
    kind = j // nper
    bn = xa_ref.shape[1]

    for src, dst in zip(side_src, side_dst):
        dst[...] = src[...].astype(BF16)
    if cast_w:
        wb_ref[...] = w_ref[...].astype(BF16)
        w_bf = wb_ref
    else:
        w_bf = w_ref

    def project(epilogue, normalize=False):
        if normalize:
            nx = x_ref.shape[0]
            xn_ref[0:nx, :] = _rmsnorm(x_ref[...], g_ref[...]).astype(BF16)
            if has_tail:
                xn_ref[nx:, :] = _rmsnorm(tail_ref[...], g_ref[...]).astype(BF16)
        epilogue(jnp.dot(xn_ref[...], w_bf[...], preferred_element_type=F32))

    def to_xa(acc):
        xa_ref[...] = acc

    def rope(scale):
        def epilogue(acc):
            cos, sin = cos_ref[...], sin_ref[...]
            for h in range(bn // HEAD_DIM):
                a = acc[:, h * HEAD_DIM:(h + 1) * HEAD_DIM]
                r = a * cos + pltpu.roll(a, HEAD_DIM // 2, axis=1) * sin
                if scale is not None:
                    r = r * scale
                act_ref[:, h * HEAD_DIM:(h + 1) * HEAD_DIM] = r.astype(BF16)
        return epilogue

    def elementwise(fn):
        def epilogue(acc):
            act_ref[...] = fn(acc).astype(BF16)
        return epilogue

    pl.when(j == 0)(lambda: project(to_xa, normalize=True))
    if nper > 1:
        pl.when((j > 0) & (kind == 0))(lambda: project(to_xa))
    pl.when(kind == _COL_GATE)(lambda: project(elementwise(jax.nn.gelu)))
    pl.when(kind == _COL_Q)(lambda: project(rope(None)))
    pl.when(kind == _COL_K)(lambda: project(rope(HEAD_DIM ** -0.5)))
    pl.when(kind == _COL_V)(lambda: project(elementwise(lambda a: a)))
    pl.when(kind == _COL_G)(lambda: project(elementwise(lambda a: a * _sigmoid(a))))
    pl.when(kind >= _COL_GATE_A)(lambda: project(elementwise(_sigmoid)))


def _inproj(x, g, w, cos, sin, bm, bn, table_blocks, side=(), tail=None):
    xrows, d = x.shape
    m = xrows if tail is None else xrows + tail.shape[0]
    assert tail is None or m == bm
    n = w.shape[1]
    width = 1024
    nper = width // bn
    cast_w = w.dtype != BF16

    def act_map(i, j):
        kind = j // nper
        return i, jnp.where(kind == 0, 0, _act_block(kind) * nper + j % nper)

    in_specs = [
        pl.BlockSpec((min(bm, xrows), d), lambda i, j: (i, 0)),
        pl.BlockSpec((1, d), lambda i, j: (0, 0)),
        pl.BlockSpec((d, bn), lambda i, j: (0, j)),
        pl.BlockSpec((bm, LANES), lambda i, j: (i % table_blocks, 0)),
        pl.BlockSpec((bm, LANES), lambda i, j: (i % table_blocks, 0)),
    ]
    if tail is not None:
        in_specs.append(pl.BlockSpec(tail.shape, lambda i, j: (0, 0)))
    out_specs = [
        pl.BlockSpec((bm, bn), lambda i, j: (i, jnp.minimum(j, nper - 1))),
        pl.BlockSpec((bm, bn), act_map),
    ]
    out_shape = [jax.ShapeDtypeStruct((m, width), F32), jax.ShapeDtypeStruct((m, n - width), BF16)]
    if cast_w:
        out_specs.append(pl.BlockSpec((d, bn), lambda i, j: (0, j)))
        out_shape.append(jax.ShapeDtypeStruct(w.shape, BF16))
    for arr, block, index_map in side:
        in_specs.append(pl.BlockSpec(block, index_map))
        out_specs.append(pl.BlockSpec(block, index_map))
        out_shape.append(jax.ShapeDtypeStruct(arr.shape, BF16))
    return pl.pallas_call(
        functools.partial(_inproj_kernel, nper=nper, cast_w=cast_w, n_side=len(side), has_tail=tail is not None),
        name="inproj",
        grid=(m // bm, n // bn),
        in_specs=in_specs,
        out_specs=out_specs,
        out_shape=out_shape,
        scratch_shapes=[pltpu.VMEM((bm, d), BF16)],
        compiler_params=_cparams(2),
    )(x, g, w, cos, sin, *([] if tail is None else [tail]), *[arr for arr, _, _ in side])


def _lru_coeffs(xc, wbd_ref, bg_ref, cl_row, group):
    gates = jnp.dot(xc.astype(BF16), wbd_ref[group], preferred_element_type=F32) + bg_ref[group]
    rec = _sigmoid(gates[:, :LANES])
    ing = _sigmoid(gates[:, LANES:])
    log_a = rec * cl_row
    a = jnp.exp(log_a)
    mult = jnp.sqrt(jnp.tanh(-log_a) * (a * a + 1.0))
    return a, mult, ing


def _lru_seq_kernel(xa_ref, ga_ref, conv0_ref, h0_ref, cw_ref, cb_ref, wbd_ref, bg_ref, lam_ref,
                    *refs, tr, emit_y, first_pos_zero):
    if emit_y:
        ya_ref, convout_ref, hout_ref, halo_ref, h_ref, a_ref, b_ref = refs
    else:
        convout_ref, hout_ref, halo_ref, h_ref, a_ref, b_ref = refs
    t = pl.program_id(1)
    halo = 8

    @pl.when(t == 0)
    def _():
        halo_ref[0:halo - 3, :] = jnp.zeros((halo - 3, halo_ref.shape[1]), F32)
        halo_ref[halo - 3:halo, :] = conv0_ref[0]
        h_ref[...] = h0_ref[0]

    xa = xa_ref[...]
    cw = [cw_ref[CONV_W - 1 - k:CONV_W - k, :] for k in range(CONV_W)]
    xc = cb_ref[...] + cw[0] * xa + sum(cw[k] * pltpu.roll(xa, k, axis=0) for k in range(1, CONV_W))
    window = jnp.concatenate([halo_ref[0:halo, :], xa[0:halo, :]], axis=0)
    first = cb_ref[...] + sum(cw[k] * window[halo - k:2 * halo - k, :] for k in range(CONV_W))
    xc = jnp.concatenate([first, xc[halo:, :]], axis=0)
    halo_ref[0:halo, :] = xa[tr - halo:, :]

    cl = LRU_C * jax.nn.log_sigmoid(lam_ref[...])
    for g in range(xc.shape[1] // LANES):
        sl = slice(g * LANES, (g + 1) * LANES)
        a, mult, ing = _lru_coeffs(xc[:, sl], wbd_ref, bg_ref, cl[:, sl], g)
        if first_pos_zero:
            row = lax.broadcasted_iota(jnp.int32, a.shape, 0) + t * tr
            mult = jnp.where(row == 0, 1.0, mult)
        a_ref[:, sl] = a
        b_ref[:, sl] = mult * ing * xc[:, sl]

    h = h_ref[...]
    for r in range(tr):
        h = a_ref[r:r + 1, :] * h + b_ref[r:r + 1, :]
        halo_ref[halo + r:halo + r + 1, :] = h
    h_ref[...] = h
    if emit_y:
        ya_ref[...] = (halo_ref[halo:halo + tr, :] * ga_ref[...].astype(F32)).astype(BF16)

    @pl.when(t == pl.num_programs(1) - 1)
    def _():
        convout_ref[0] = halo_ref[halo - 3:halo, :]
        hout_ref[0] = h_ref[...]


def _lru_seq(xa, act, row0, nseq, seqlen, tr, conv0, h0, lw, emit_y, first_pos_zero):
    w = xa.shape[1]
    nt = seqlen // tr
    rb0 = row0 // tr
    ng = w // LANES
    state_map = (lambda b, t: (b, 0, 0)) if conv0.shape[0] == nseq else (lambda b, t: (0, 0, 0))
    const2 = lambda b, t: (0, 0)
    const3 = lambda b, t: (0, 0, 0)
    out_specs = [pl.BlockSpec((1, CONV_W - 1, w), lambda b, t: (b, 0, 0)),
                 pl.BlockSpec((1, 1, w), lambda b, t: (b, 0, 0))]
    out_shape = [jax.ShapeDtypeStruct((nseq, CONV_W - 1, w), F32), jax.ShapeDtypeStruct((nseq, 1, w), F32)]
    if emit_y:
        out_specs = [pl.BlockSpec((tr, w), lambda b, t: (b * nt + t, 0))] + out_specs
        out_shape = [jax.ShapeDtypeStruct((nseq * seqlen, w), BF16)] + out_shape
    return dict(
        name="lru_seq",
        kernel=functools.partial(_lru_seq_kernel, tr=tr, emit_y=emit_y, first_pos_zero=first_pos_zero),
        grid=(nseq, nt),
        in_specs=[
            pl.BlockSpec((tr, w), lambda b, t: (rb0 + b * nt + t, 0)),
            pl.BlockSpec((tr, w), lambda b, t: (rb0 + b * nt + t, _ACT_GELU)),
            pl.BlockSpec((1, CONV_W - 1, w), state_map),
            pl.BlockSpec((1, 1, w), state_map),
            pl.BlockSpec((CONV_W, w), const2),
            pl.BlockSpec((1, w), const2),
            pl.BlockSpec((ng, LANES, 2 * LANES), const3),
            pl.BlockSpec((ng, 1, 2 * LANES), const3),
            pl.BlockSpec((1, w), const2),
        ],
        out_specs=out_specs,
        out_shape=out_shape,
        scratch_shapes=[pltpu.VMEM((8 + tr, w), F32), pltpu.VMEM((1, w), F32),
                        pltpu.VMEM((tr, w), F32), pltpu.VMEM((tr, w), F32)],
        args=(xa, act, conv0, h0, lw["conv_w"], lw["conv_b"], lw["wbd"], lw["bg"], lw["lam"]))


def _lru_step_kernel(xa_ref, ga_ref, conv0_ref, h0_ref, cw_ref, cb_ref, wbd_ref, bg_ref, lam_ref,
                     ya_ref, convout_ref, hout_ref, hs_ref, *, nseq, seqlen):
    cl = LRU_C * jax.nn.log_sigmoid(lam_ref[...])
    full = [conv0_ref[j] for j in range(CONV_W - 1)]
    full += [xa_ref[pl.ds(t, nseq, stride=seqlen), :] for t in range(seqlen)]
    h = h0_ref[...]
    for t in range(seqlen):
        xc = cb_ref[...] + cw_ref[0:1, :] * full[t]
        for j in range(1, CONV_W):
            xc = xc + cw_ref[j:j + 1, :] * full[t + j]
        a, mult, ing = _lru_coeffs(xc, wbd_ref, bg_ref, cl, 0)
        h = a * h + mult * ing * xc
        hs_ref[pl.ds(t, nseq, stride=seqlen), :] = h
    ya_ref[...] = (hs_ref[...] * ga_ref[...].astype(F32)).astype(BF16)
    for j in range(CONV_W - 1):
        convout_ref[j] = full[seqlen + j]
    hout_ref[...] = h


def _lru_step(xa, act, nseq, seqlen, conv0_t, h0, lw):
    w = xa.shape[1]
    rows = nseq * seqlen
    ng = w // LANES
    col = lambda g: (0, g)
    return pl.pallas_call(
        functools.partial(_lru_step_kernel, nseq=nseq, seqlen=seqlen),
        name="lru_step",
        grid=(ng,),
        in_specs=[
            pl.BlockSpec((rows, LANES), col),
            pl.BlockSpec((rows, LANES), col),
            pl.BlockSpec((CONV_W - 1, nseq, LANES), lambda g: (0, 0, g)),
            pl.BlockSpec((nseq, LANES), col),
            pl.BlockSpec((CONV_W, LANES), col),
            pl.BlockSpec((1, LANES), col),
            pl.BlockSpec((1, LANES, 2 * LANES), lambda g: (g, 0, 0)),
            pl.BlockSpec((1, 1, 2 * LANES), lambda g: (g, 0, 0)),
            pl.BlockSpec((1, LANES), col),
        ],
        out_specs=[
            pl.BlockSpec((rows, LANES), col),
            pl.BlockSpec((CONV_W - 1, nseq, LANES), lambda g: (0, 0, g)),
            pl.BlockSpec((nseq, LANES), col),
        ],
        out_shape=[jax.ShapeDtypeStruct((rows, w), BF16),
                   jax.ShapeDtypeStruct((CONV_W - 1, nseq, w), F32),
                   jax.ShapeDtypeStruct((nseq, w), F32)],
        scratch_shapes=[pltpu.VMEM((rows, LANES), F32)],
        compiler_params=_cparams(1),
    )(xa, act, conv0_t, h0, lw["conv_w"], lw["conv_b"], lw["wbd"], lw["bg"], lw["lam"])


def _log_gamma(h):
    return math.log1p(-(2.0 ** (-5.0 - h)))


def _ret_kernel(q_ref, k_ref, v_ref, sg_ref, s0_ref, ng_ref, *refs, nb, clen, cps, emit_y, ids):
    if emit_y:
        y_ref, sout_ref, s_ref, dec_ref, rdec_ref = refs
    else:
        sout_ref, s_ref, dec_ref, rdec_ref = refs
    group, c, last = ids() if ids is not None else (pl.program_id(0), pl.program_id(1), pl.num_programs(1) - 1)
    rows = nb * clen
    heads = range(RET_HEADS)
    chunks = range(cps)

    @pl.when((group == 0) & (c == 0))
    def _():
        li = lax.broadcasted_iota(jnp.int32, (rows, rows), 0)
        mi = lax.broadcasted_iota(jnp.int32, (rows, rows), 1)
        keep = (li // clen == mi // clen) & (li >= mi)
        diff = jnp.where(keep, li - mi, 0).astype(F32)
        tpos = (lax.broadcasted_iota(jnp.int32, (rows, LANES), 0) % clen).astype(F32)
        for h in heads:
            lg = _log_gamma(h)
            dec_ref[h] = jnp.where(keep, jnp.exp(lg * diff), 0.0)
            rdec_ref[0, h] = jnp.exp(lg * (tpos + 1.0))
            rdec_ref[1, h] = jnp.exp(lg * (clen - 1.0 - tpos))

    @pl.when(c == 0)
    def _():
        for n in range(nb):
            s_ref[n] = s0_ref[n if s0_ref.shape[0] == nb else 0]

    if nb > 1:
        seq_of_row = lax.broadcasted_iota(jnp.int32, (rows, nb * HEAD_DIM), 0) // clen
        seq_of_col = lax.broadcasted_iota(jnp.int32, (rows, nb * HEAD_DIM), 1) // HEAD_DIM
        own = seq_of_row == seq_of_col
    tn = (((0,), (0,)), ((), ()))
    nt = (((1,), (1,)), ((), ()))
    hsl = [slice(h * HEAD_DIM, (h + 1) * HEAD_DIM) for h in heads]
    rsl = [slice(ci * rows, (ci + 1) * rows) for ci in chunks]
    q = [[q_ref[rsl[ci], sl] for sl in hsl] for ci in chunks]
    k = [[k_ref[rsl[ci], sl] for sl in hsl] for ci in chunks]
    v = [[v_ref[rsl[ci], sl] for sl in hsl] for ci in chunks]

    if emit_y:
        scores = [[lax.dot_general(q[ci][h], k[ci][h], nt, preferred_element_type=F32) for h in heads]
                  for ci in chunks]
    upd = []
    for ci in chunks:
        upd.append([])
        for h in heads:
            kd = (k[ci][h].astype(F32) * rdec_ref[1, h]).astype(BF16)
            if nb > 1:
                v_bd = jnp.where(own, jnp.concatenate([v[ci][h].astype(F32)] * nb, axis=1), 0.0).astype(BF16)
            else:
                v_bd = v[ci][h]
            upd[ci].append(lax.dot_general(kd, v_bd, tn, preferred_element_type=F32))

    state = [[[s_ref[n, h] for n in range(nb)] for h in heads]]
    for ci in chunks:
        state.append([[math.exp(clen * _log_gamma(h)) * state[ci][h][n]
                       + upd[ci][h][:, n * HEAD_DIM:(n + 1) * HEAD_DIM] for n in range(nb)] for h in heads])
    for h in heads:
        for n in range(nb):
            s_ref[n, h] = state[cps][h][n]

    if emit_y:
        inter = []
        for ci in chunks:
            inter.append([])
            for h in heads:
                s_cat = jnp.concatenate(state[ci][h], axis=1) if nb > 1 else state[ci][h][0]
                qs = jnp.dot(q[ci][h], s_cat.astype(BF16), preferred_element_type=F32)
                if nb > 1:
                    qs = jnp.where(own, qs, 0.0)
                    qs = sum(qs[:, n * HEAD_DIM:(n + 1) * HEAD_DIM] for n in range(nb))
                inter[ci].append(qs)
        for ci in chunks:
            for h in heads:
                p = (scores[ci][h] * dec_ref[h]).astype(BF16)
                o = jnp.dot(p, v[ci][h], preferred_element_type=F32) + inter[ci][h] * rdec_ref[0, h]
                mu = jnp.mean(o, axis=-1, keepdims=True)
                dev = o - mu
                var = jnp.mean(dev * dev, axis=-1, keepdims=True)
                normed = dev * lax.rsqrt(var + EPS) * ng_ref[:, hsl[h]]
                y_ref[rsl[ci], hsl[h]] = (sg_ref[rsl[ci], hsl[h]].astype(F32) * normed).astype(BF16)

    @pl.when(c == last)
    def _():
        sout_ref[...] = s_ref[...]


def _retention(act, row0, nseq, seqlen, nb, clen, cps, s0, norm_g, emit_y, ids=None):
    w = RET_HEADS * HEAD_DIM
    rows = nb * clen * cps
    nc = seqlen // (clen * cps)
    assert nb == 1 or (nc == 1 and cps == 1)
    rb0 = row0 // rows
    rmap = lambda col: (lambda b, c: (rb0 + b * nc + c, col))
    shared = s0.shape[0] != nseq
    s_map = (lambda b, c: (0, 0, 0, 0)) if shared else (lambda b, c: (b, 0, 0, 0))
    s_block = (1 if shared else nb, RET_HEADS, HEAD_DIM, HEAD_DIM)
    out_specs = [pl.BlockSpec((nb, RET_HEADS, HEAD_DIM, HEAD_DIM), lambda b, c: (b, 0, 0, 0))]
    out_shape = [jax.ShapeDtypeStruct((nseq, RET_HEADS, HEAD_DIM, HEAD_DIM), F32)]
    if emit_y:
        out_specs = [pl.BlockSpec((rows, w), lambda b, c: (b * nc + c, 0))] + out_specs
        out_shape = [jax.ShapeDtypeStruct((nseq * seqlen, w), BF16)] + out_shape
    return dict(
        name="retention",
        kernel=functools.partial(_ret_kernel, nb=nb, clen=clen, cps=cps, emit_y=emit_y, ids=ids),
        grid=(nseq // nb, nc),
        in_specs=[
            pl.BlockSpec((rows, w), rmap(_ACT_Q)),
            pl.BlockSpec((rows, w), rmap(_ACT_K)),
            pl.BlockSpec((rows, w), rmap(_ACT_V)),
            pl.BlockSpec((rows, w), rmap(_ACT_SILU)),
            pl.BlockSpec(s_block, s_map),
            pl.BlockSpec((1, w), lambda b, c: (0, 0)),
        ],
        out_specs=out_specs,
        out_shape=out_shape,
        scratch_shapes=[pltpu.VMEM((nb, RET_HEADS, HEAD_DIM, HEAD_DIM), F32),
                        pltpu.VMEM((RET_HEADS, nb * clen, nb * clen), F32),
                        pltpu.VMEM((2, RET_HEADS, nb * clen, LANES), F32)],
        args=(act, act, act, act, s0, norm_g))


def _outproj_kernel(x_ref, ya_ref, yb_ref, sga_ref, sgb_ref, pa_ref, pb_ref, wo_ref, o_ref):
    for r0, nr in _row_chunks(o_ref.shape[0], 256):
        rows = slice(r0, r0 + nr)
        ma = jnp.dot(ya_ref[rows, :], pa_ref[...], preferred_element_type=F32)
        mb = jnp.dot(yb_ref[rows, :], pb_ref[...], preferred_element_type=F32)
        merged = sga_ref[rows, :].astype(F32) * ma + sgb_ref[rows, :].astype(F32) * mb
        o_ref[rows, :] = x_ref[rows, :] + jnp.dot(merged.astype(BF16), wo_ref[...], preferred_element_type=F32)


def _outproj(x, ya, yb, act, pa, pb, wo, bm):
    m, d = x.shape
    w = ya.shape[1]
    nd = d // 1024
    row = lambda i: (i, 0)
    const = lambda i: (0, 0)
    single = pl.Buffered(1)
    return pl.pallas_call(
        _outproj_kernel,
        name="outproj",
        grid=(m // bm,),
        in_specs=[
            pl.BlockSpec((bm, d), row),
            pl.BlockSpec((bm, w), row),
            pl.BlockSpec((bm, w), row),
            pl.BlockSpec((bm, d), lambda i: (i, _ACT_SGA // nd)),
            pl.BlockSpec((bm, d), lambda i: (i, _ACT_SGB // nd)),
            pl.BlockSpec((w, d), const, pipeline_mode=single),
            pl.BlockSpec((w, d), const, pipeline_mode=single),
            pl.BlockSpec((d, d), const, pipeline_mode=single),
        ],
        out_specs=pl.BlockSpec((bm, d), row),
        out_shape=jax.ShapeDtypeStruct((m, d), F32),
        compiler_params=_cparams(1),
    )(x, ya, yb, act, act, pa, pb, wo)


def _ffn_kernel(x_ref, g_ref, wu_ref, wd_ref, gf_ref, o_ref, hn_ref):
    j = pl.program_id(1)
    last = pl.num_programs(1) - 1

    def block(first, final):
        for r0, nr in _row_chunks(o_ref.shape[0], 512):
            rows = slice(r0, r0 + nr)
            if first:
                base = x_ref[rows, :]
                hn = _rmsnorm(base, g_ref[...]).astype(BF16)
                hn_ref[rows, :] = hn
            else:
                base = o_ref[rows, :]
                hn = hn_ref[rows, :]
            u = jnp.dot(hn, wu_ref[...], preferred_element_type=F32)
            r = jnp.square(jnp.maximum(u, 0.0)).astype(BF16)
            acc = base + jnp.dot(r, wd_ref[...], preferred_element_type=F32)
            o_ref[rows, :] = _rmsnorm(acc, gf_ref[...]) if final else acc

    pl.when(j == 0)(lambda: block(True, False))
    pl.when((j > 0) & (j < last))(lambda: block(False, False))
    pl.when(j == last)(lambda: block(False, True))


def _ffn(x, g, wu, wd, gf, bm, bf):
    m, d = x.shape
    dff = wu.shape[1]
    assert dff // bf >= 2
    return pl.pallas_call(
        _ffn_kernel,
        name="ffn",
        grid=(m // bm, dff // bf),
        in_specs=[
            pl.BlockSpec((bm, d), lambda i, j: (i, 0)),
            pl.BlockSpec((1, d), lambda i, j: (0, 0)),
            pl.BlockSpec((d, bf), lambda i, j: (0, j)),
            pl.BlockSpec((bf, d), lambda i, j: (j, 0)),
            pl.BlockSpec((1, d), lambda i, j: (0, 0)),
        ],
        out_specs=pl.BlockSpec((bm, d), lambda i, j: (i, 0)),
        out_shape=jax.ShapeDtypeStruct((m, d), F32),
        scratch_shapes=[pltpu.VMEM((bm, d), BF16)],
        compiler_params=_cparams(2),
    )(x, g, wu, wd, gf)


def _rope_tables(pos):
    inv = ROPE_BASE ** (-np.arange(0, HEAD_DIM, 2, dtype=np.float64) / HEAD_DIM)
    ang = np.asarray(pos, np.float64)[:, None] * inv[None, :]
    cos, sin = np.cos(ang), np.sin(ang)
    return (jnp.asarray(np.concatenate([cos, cos], axis=1), F32),
            jnp.asarray(np.concatenate([-sin, sin], axis=1), F32))


def _gate_weights(wa, wx, ba, bx):
    nblk, blk, _ = wa.shape
    per = LANES // blk
    ng = nblk // per

    def bd(wt):
        wt = wt.reshape(ng, per, blk, blk)
        eye = jnp.eye(per, dtype=wt.dtype)
        return jnp.einsum("gpcd,pq->gpcqd", wt, eye).reshape(ng, LANES, LANES)

    wbd = jnp.concatenate([bd(wa), bd(wx)], axis=2).astype(BF16)
    bg = jnp.concatenate([ba.reshape(ng, 1, LANES), bx.reshape(ng, 1, LANES)], axis=2)
    return wbd, bg


def kernel(x_prompt, x_sample, state_conv, state_lru, state_ret, meta_tokens, norm_mix_g, w_in, conv_w,
           conv_b, lru_wa, lru_ba, lru_wx, lru_bx, lru_lam, ret_norm_g, p_a, p_b, w_out, norm_ffn_g,
           w_up, w_down, norm_f_g):
    assert w_in.shape[0] == 1
    nb_p, t_p, d = x_prompt.shape
    nb_s, t_s, _ = x_sample.shape
    w = conv_w.shape[-1]
    rows_p, rows_s = nb_p * t_p, nb_s * t_s

    g_mix, g_ffn, g_f = norm_mix_g[0][None], norm_ffn_g[0][None], norm_f_g[None]
    wbd, bg = _gate_weights(lru_wa[0], lru_wx[0], lru_ba[0], lru_bx[0])
    lw = dict(conv_w=conv_w[0], conv_b=conv_b[0][None], wbd=wbd, bg=bg, lam=lru_lam[0][None])
    ret_g = ret_norm_g[0][None]

    cos_p, sin_p = _rope_tables(N_META + np.arange(t_p))
    cos_s, sin_s = _rope_tables(np.concatenate([np.tile(PAST_LEN + np.arange(t_s), nb_s), np.arange(N_META)]))
    x_p2 = x_prompt.reshape(rows_p, d)
    x_s2 = x_sample.reshape(rows_s, d)
    xa_s, act_s, w_in_b = _inproj(x_s2, g_mix, w_in[0], cos_s, sin_s, rows_s + N_META, 512, 1,
                                  tail=meta_tokens.astype(x_sample.dtype))
    bm_p = 1024
    n_i, n_j = rows_p // bm_p, 8
    side_map = lambda i, j: (i, jnp.minimum(j, n_j - 1))
    later = (w_up, w_down, p_a, p_b, w_out)
    sides = tuple((wt[0], (wt.shape[1] // n_i, wt.shape[2] // n_j), side_map) for wt in later)
    xa_p, act_p, wu_b, wd_b, pa_b, pb_b, wo_b = _inproj(x_p2, g_mix, w_in_b, cos_p, sin_p, bm_p, 1024, t_p // bm_p,
                                                        side=sides)


    zc = jnp.zeros((1, CONV_W - 1, w), F32)
    zh = jnp.zeros((1, 1, w), F32)
    zs = jnp.zeros((1, RET_HEADS, HEAD_DIM, HEAD_DIM), F32)
    conv_m, h_m = _call(_lru_seq(xa_s, act_s, rows_s, 1, N_META, N_META, zc, zh, lw, emit_y=False,
                                 first_pos_zero=True))
    (s_m,) = _call(_retention(act_s, rows_s, 1, N_META, 1, N_META, 1, zs, ret_g, emit_y=False))

    lru_tr = 512
    lru_p = _lru_seq(xa_p, act_p, 0, nb_p, t_p, lru_tr, conv_m, h_m, lw, emit_y=True, first_pos_zero=False)
    steps = nb_p * (t_p // lru_tr)
    ret_s = _retention(act_s, 0, nb_s, t_s, nb_s // steps, t_s, 1, state_ret[0], ret_g, emit_y=True,
                       ids=lambda: (pl.program_id(0) * (t_p // lru_tr) + pl.program_id(1), 0, 0))
    (ya_p, conv_p, h_p), (yb_s, s_s) = _call_pair(lru_p, ret_s, "lru_seq_retention")
    yb_p, s_p = _call(_retention(act_p, 0, nb_p, t_p, 1, CHUNK, 4, s_m, ret_g, emit_y=True))
    conv0_t = jnp.transpose(state_conv[0], (1, 0, 2))
    ya_s, conv_s_t, h_s = _lru_step(xa_s, act_s, nb_s, t_s, conv0_t, state_lru[0], lw)

    x1_p = _outproj(x_p2, ya_p, yb_p, act_p, pa_b, pb_b, wo_b, 512)
    y_p = _ffn(x1_p, g_ffn, wu_b, wd_b, g_f, 1024, 1024)
    y_p, ya_s = lax.optimization_barrier((y_p, ya_s))
    x1_s = _outproj(x_s2, ya_s, yb_s, act_s, pa_b, pb_b, wo_b, 512)
    y_s = _ffn(x1_s, g_ffn, wu_b, wd_b, g_f, 1024, 1024)

    return (y_p.reshape(nb_p, t_p, d), y_s.reshape(nb_s, t_s, d),
            conv_p[None], h_p.reshape(1, nb_p, w), s_p[None],
            jnp.transpose(conv_s_t, (1, 0, 2))[None], h_s[None], s_s[None])
```

```python
import functools
import math

import jax
import jax.numpy as jnp
import numpy as np
from jax import lax
from jax.experimental import pallas as pl
from jax.experimental.pallas import tpu as pltpu

F32 = jnp.float32
BF16 = jnp.bfloat16

N_META = 16
PAST_LEN = 16384
CONV_W = 4
LRU_C = 8.0
RET_HEADS = 8
HEAD_DIM = 128
CHUNK = 128
ROPE_BASE = 10000.0
EPS = 1e-6

LANES = 128
VMEM_MAX_MIB = 56


def _cparams(n_axes, vmem_mib):
    assert vmem_mib <= VMEM_MAX_MIB
    return pltpu.CompilerParams(dimension_semantics=("arbitrary",) * n_axes, vmem_limit_bytes=vmem_mib << 20)


def _call(spec):
    return pl.pallas_call(
        spec["kernel"], name=spec["name"], grid=spec["grid"], in_specs=spec["in_specs"],
        out_specs=spec["out_specs"], out_shape=spec["out_shape"], scratch_shapes=spec["scratch_shapes"],
        compiler_params=_cparams(len(spec["grid"]), spec["vmem_mib"]))(*spec["args"])


def _call_pair(a, b, name):
    n0, n1 = a["grid"]
    assert b["grid"] == (n0 * n1, 1)

    def remap(spec):
        return pl.BlockSpec(spec.block_shape, lambda i, j, m=spec.index_map: m(i * n1 + j, 0))

    counts = [len(a["in_specs"]), len(b["in_specs"]), len(a["out_specs"]), len(b["out_specs"]),
              len(a["scratch_shapes"]), len(b["scratch_shapes"])]

    def kernel(*refs):
        groups, pos = [], 0
        for n in counts:
            groups.append(refs[pos:pos + n])
            pos += n
        a["kernel"](*groups[0], *groups[2], *groups[4])
        b["kernel"](*groups[1], *groups[3], *groups[5])

    outs = pl.pallas_call(
        kernel, name=name, grid=a["grid"],
        in_specs=list(a["in_specs"]) + [remap(sp) for sp in b["in_specs"]],
        out_specs=list(a["out_specs"]) + [remap(sp) for sp in b["out_specs"]],
        out_shape=list(a["out_shape"]) + list(b["out_shape"]),
        scratch_shapes=list(a["scratch_shapes"]) + list(b["scratch_shapes"]),
        compiler_params=_cparams(2, min(a["vmem_mib"] + b["vmem_mib"], VMEM_MAX_MIB)))(*a["args"], *b["args"])
    return outs[:counts[2]], outs[counts[2]:]


def _rmsnorm(x, g):
    return x * lax.rsqrt(jnp.mean(x * x, axis=-1, keepdims=True) + EPS) * g


def _sigmoid(x):
    return 0.5 * jnp.tanh(0.5 * x) + 0.5


def _row_chunks(rows, size):
    n = max(rows // size, 1)
    return [(c * size, size if c < n - 1 else rows - c * size) for c in range(n)]


_COL_GATE, _COL_Q, _COL_K, _COL_V, _COL_G, _COL_GATE_A = 1, 2, 3, 4, 5, 6
_NUM_COL_KINDS = 10
_ACT_GELU, _ACT_Q, _ACT_K, _ACT_V, _ACT_SGA, _ACT_SGB, _ACT_SILU = 0, 1, 2, 3, 4, 6, 8


def _act_block(kind):
    return jnp.where(kind <= _COL_V, kind - 1, jnp.where(kind == _COL_G, _ACT_SILU, kind - 2))


def _inproj_kernel(x_ref, g_ref, w_ref, cos_ref, sin_ref, *refs, nper, cast_w, n_side, has_tail):
    if has_tail:
        tail_ref, refs = refs[0], refs[1:]
    side_src, refs = refs[:n_side], refs[n_side:]
    xa_ref, act_ref = refs[0], refs[1]
    refs = refs[2:]
    if cast_w:
        wb_ref, refs = refs[0], refs[1:]
    side_dst, (xn_ref,) = refs[:n_side], refs[n_side:]
    j = pl.program_id(1)
    kind = j // nper
    bn = xa_ref.shape[1]

    for src, dst in zip(side_src, side_dst):
        dst[...] = src[...].astype(BF16)
    if cast_w:
        wb_ref[...] = w_ref[...].astype(BF16)
        w_bf = wb_ref
    else:
        w_bf = w_ref

    def project(epilogue, normalize=False):
        if normalize:
            nx = x_ref.shape[0]
            xn_ref[0:nx, :] = _rmsnorm(x_ref[...], g_ref[...]).astype(BF16)
            if has_tail:
                xn_ref[nx:, :] = _rmsnorm(tail_ref[...], g_ref[...]).astype(BF16)
        epilogue(jnp.dot(xn_ref[...], w_bf[...], preferred_element_type=F32))

    def to_xa(acc):
        xa_ref[...] = acc

    def rope(scale):
        def epilogue(acc):
            cos, sin = cos_ref[...], sin_ref[...]
            for h in range(bn // HEAD_DIM):
                a = acc[:, h * HEAD_DIM:(h + 1) * HEAD_DIM]
                r = a * cos + pltpu.roll(a, HEAD_DIM // 2, axis=1) * sin
                if scale is not None:
                    r = r * scale
                act_ref[:, h * HEAD_DIM:(h + 1) * HEAD_DIM] = r.astype(BF16)
        return epilogue

    def elementwise(fn):
        def epilogue(acc):
            act_ref[...] = fn(acc).astype(BF16)
        return epilogue

    pl.when(j == 0)(lambda: project(to_xa, normalize=True))
    if nper > 1:
        pl.when((j > 0) & (kind == 0))(lambda: project(to_xa))
    pl.when(kind == _COL_GATE)(lambda: project(elementwise(jax.nn.gelu)))
    pl.when(kind == _COL_Q)(lambda: project(rope(None)))
    pl.when(kind == _COL_K)(lambda: project(rope(HEAD_DIM ** -0.5)))
    pl.when(kind == _COL_V)(lambda: project(elementwise(lambda a: a)))
    pl.when(kind == _COL_G)(lambda: project(elementwise(lambda a: a * _sigmoid(a))))
    pl.when(kind >= _COL_GATE_A)(lambda: project(elementwise(_sigmoid)))


def _inproj(x, g, w, cos, sin, bm, bn, table_blocks, side=(), tail=None):
    xrows, d = x.shape
    m = xrows if tail is None else xrows + tail.shape[0]
    assert tail is None or m == bm
    n = w.shape[1]
    width = n // _NUM_COL_KINDS
    nper = width // bn
    cast_w = w.dtype != BF16

    def act_map(i, j):
        kind = j // nper
        return i, jnp.where(kind == 0, 0, _act_block(kind) * nper + j % nper)

    in_specs = [
        pl.BlockSpec((min(bm, xrows), d), lambda i, j: (i, 0)),
        pl.BlockSpec((1, d), lambda i, j: (0, 0)),
        pl.BlockSpec((d, bn), lambda i, j: (0, j)),
        pl.BlockSpec((bm, LANES), lambda i, j: (i % table_blocks, 0)),
        pl.BlockSpec((bm, LANES), lambda i, j: (i % table_blocks, 0)),
    ]
    if tail is not None:
        in_specs.append(pl.BlockSpec(tail.shape, lambda i, j: (0, 0)))
    out_specs = [
        pl.BlockSpec((bm, bn), lambda i, j: (i, jnp.minimum(j, nper - 1))),
        pl.BlockSpec((bm, bn), act_map),
    ]
    out_shape = [jax.ShapeDtypeStruct((m, width), F32), jax.ShapeDtypeStruct((m, n - width), BF16)]
    if cast_w:
        out_specs.append(pl.BlockSpec((d, bn), lambda i, j: (0, j)))
        out_shape.append(jax.ShapeDtypeStruct(w.shape, BF16))
    for arr, block, index_map in side:
        in_specs.append(pl.BlockSpec(block, index_map))
        out_specs.append(pl.BlockSpec(block, index_map))
        out_shape.append(jax.ShapeDtypeStruct(arr.shape, BF16))
    return pl.pallas_call(
        functools.partial(_inproj_kernel, nper=nper, cast_w=cast_w, n_side=len(side), has_tail=tail is not None),
        name="inproj",
        grid=(m // bm, n // bn),
        in_specs=in_specs,
        out_specs=out_specs,
        out_shape=out_shape,
        scratch_shapes=[pltpu.VMEM((bm, d), BF16)],
        compiler_params=_cparams(2, 40 if cast_w else VMEM_MAX_MIB),
    )(x, g, w, cos, sin, *([] if tail is None else [tail]), *[arr for arr, _, _ in side])


def _lru_coeffs(xc, wbd_ref, bg_ref, cl_row, group):
    gates = jnp.dot(xc.astype(BF16), wbd_ref[group], preferred_element_type=F32) + bg_ref[group]
    rec = _sigmoid(gates[:, :LANES])
    ing = _sigmoid(gates[:, LANES:])
    log_a = rec * cl_row
    a = jnp.exp(log_a)
    mult = jnp.sqrt(jnp.tanh(-log_a) * (a * a + 1.0))
    return a, mult, ing


def _lru_seq_kernel(xa_ref, ga_ref, conv0_ref, h0_ref, cw_ref, cb_ref, wbd_ref, bg_ref, lam_ref,
                    *refs, tr, emit_y, first_pos_zero):
    if emit_y:
        ya_ref, convout_ref, hout_ref, halo_ref, h_ref, a_ref, b_ref = refs
    else:
        convout_ref, hout_ref, halo_ref, h_ref, a_ref, b_ref = refs
    t = pl.program_id(1)
    halo = 8

    @pl.when(t == 0)
    def _():
        halo_ref[0:halo - 3, :] = jnp.zeros((halo - 3, halo_ref.shape[1]), F32)
        halo_ref[halo - 3:halo, :] = conv0_ref[0]
        h_ref[...] = h0_ref[0]

    xa = xa_ref[...]
    cw = [cw_ref[CONV_W - 1 - k:CONV_W - k, :] for k in range(CONV_W)]
    xc = cb_ref[...] + cw[0] * xa + sum(cw[k] * pltpu.roll(xa, k, axis=0) for k in range(1, CONV_W))
    window = jnp.concatenate([halo_ref[0:halo, :], xa[0:halo, :]], axis=0)
    first = cb_ref[...] + sum(cw[k] * window[halo - k:2 * halo - k, :] for k in range(CONV_W))
    xc = jnp.concatenate([first, xc[halo:, :]], axis=0)
    halo_ref[0:halo, :] = xa[tr - halo:, :]

    cl = LRU_C * jax.nn.log_sigmoid(lam_ref[...])
    for g in range(xc.shape[1] // LANES):
        sl = slice(g * LANES, (g + 1) * LANES)
        a, mult, ing = _lru_coeffs(xc[:, sl], wbd_ref, bg_ref, cl[:, sl], g)
        if first_pos_zero:
            row = lax.broadcasted_iota(jnp.int32, a.shape, 0) + t * tr
            mult = jnp.where(row == 0, 1.0, mult)
        a_ref[:, sl] = a
        b_ref[:, sl] = mult * ing * xc[:, sl]

    h = h_ref[...]
    for r in range(tr):
        h = a_ref[r:r + 1, :] * h + b_ref[r:r + 1, :]
        halo_ref[halo + r:halo + r + 1, :] = h
    h_ref[...] = h
    if emit_y:
        ya_ref[...] = (halo_ref[halo:halo + tr, :] * ga_ref[...].astype(F32)).astype(BF16)

    @pl.when(t == pl.num_programs(1) - 1)
    def _():
        convout_ref[0] = halo_ref[halo - 3:halo, :]
        hout_ref[0] = h_ref[...]


def _lru_seq(xa, act, row0, nseq, seqlen, tr, conv0, h0, lw, emit_y, first_pos_zero):
    w = xa.shape[1]
    nt = seqlen // tr
    rb0 = row0 // tr
    ng = w // LANES
    state_map = (lambda b, t: (b, 0, 0)) if conv0.shape[0] == nseq else (lambda b, t: (0, 0, 0))
    const2 = lambda b, t: (0, 0)
    const3 = lambda b, t: (0, 0, 0)
    out_specs = [pl.BlockSpec((1, CONV_W - 1, w), lambda b, t: (b, 0, 0)),
                 pl.BlockSpec((1, 1, w), lambda b, t: (b, 0, 0))]
    out_shape = [jax.ShapeDtypeStruct((nseq, CONV_W - 1, w), F32), jax.ShapeDtypeStruct((nseq, 1, w), F32)]
    if emit_y:
        out_specs = [pl.BlockSpec((tr, w), lambda b, t: (b * nt + t, 0))] + out_specs
        out_shape = [jax.ShapeDtypeStruct((nseq * seqlen, w), BF16)] + out_shape
    return dict(
        name="lru_seq",
        kernel=functools.partial(_lru_seq_kernel, tr=tr, emit_y=emit_y, first_pos_zero=first_pos_zero),
        grid=(nseq, nt),
        in_specs=[
            pl.BlockSpec((tr, w), lambda b, t: (rb0 + b * nt + t, 0)),
            pl.BlockSpec((tr, w), lambda b, t: (rb0 + b * nt + t, _ACT_GELU)),
            pl.BlockSpec((1, CONV_W - 1, w), state_map),
            pl.BlockSpec((1, 1, w), state_map),
            pl.BlockSpec((CONV_W, w), const2),
            pl.BlockSpec((1, w), const2),
            pl.BlockSpec((ng, LANES, 2 * LANES), const3),
            pl.BlockSpec((ng, 1, 2 * LANES), const3),
            pl.BlockSpec((1, w), const2),
        ],
        out_specs=out_specs,
        out_shape=out_shape,
        scratch_shapes=[pltpu.VMEM((8 + tr, w), F32), pltpu.VMEM((1, w), F32),
                        pltpu.VMEM((tr, w), F32), pltpu.VMEM((tr, w), F32)],
        args=(xa, act, conv0, h0, lw["conv_w"], lw["conv_b"], lw["wbd"], lw["bg"], lw["lam"]),
        vmem_mib=4 + (10 * tr * w * 4 >> 20))


def _lru_step_kernel(xa_ref, ga_ref, conv0_ref, h0_ref, cw_ref, cb_ref, wbd_ref, bg_ref, lam_ref,
                     ya_ref, convout_ref, hout_ref, hs_ref, *, nseq, seqlen):
    cl = LRU_C * jax.nn.log_sigmoid(lam_ref[...])
    full = [conv0_ref[j] for j in range(CONV_W - 1)]
    full += [xa_ref[pl.ds(t, nseq, stride=seqlen), :] for t in range(seqlen)]
    h = h0_ref[...]
    for t in range(seqlen):
        xc = cb_ref[...] + cw_ref[0:1, :] * full[t]
        for j in range(1, CONV_W):
            xc = xc + cw_ref[j:j + 1, :] * full[t + j]
        a, mult, ing = _lru_coeffs(xc, wbd_ref, bg_ref, cl, 0)
        h = a * h + mult * ing * xc
        hs_ref[pl.ds(t, nseq, stride=seqlen), :] = h
    ya_ref[...] = (hs_ref[...] * ga_ref[...].astype(F32)).astype(BF16)
    for j in range(CONV_W - 1):
        convout_ref[j] = full[seqlen + j]
    hout_ref[...] = h


def _lru_step(xa, act, nseq, seqlen, conv0_t, h0, lw):
    w = xa.shape[1]
    rows = nseq * seqlen
    ng = w // LANES
    col = lambda g: (0, g)
    return pl.pallas_call(
        functools.partial(_lru_step_kernel, nseq=nseq, seqlen=seqlen),
        name="lru_step",
        grid=(ng,),
        in_specs=[
            pl.BlockSpec((rows, LANES), col),
            pl.BlockSpec((rows, LANES), col),
            pl.BlockSpec((CONV_W - 1, nseq, LANES), lambda g: (0, 0, g)),
            pl.BlockSpec((nseq, LANES), col),
            pl.BlockSpec((CONV_W, LANES), col),
            pl.BlockSpec((1, LANES), col),
            pl.BlockSpec((1, LANES, 2 * LANES), lambda g: (g, 0, 0)),
            pl.BlockSpec((1, 1, 2 * LANES), lambda g: (g, 0, 0)),
            pl.BlockSpec((1, LANES), col),
        ],
        out_specs=[
            pl.BlockSpec((rows, LANES), col),
            pl.BlockSpec((CONV_W - 1, nseq, LANES), lambda g: (0, 0, g)),
            pl.BlockSpec((nseq, LANES), col),
        ],
        out_shape=[jax.ShapeDtypeStruct((rows, w), BF16),
                   jax.ShapeDtypeStruct((CONV_W - 1, nseq, w), F32),
                   jax.ShapeDtypeStruct((nseq, w), F32)],
        scratch_shapes=[pltpu.VMEM((rows, LANES), F32)],
        compiler_params=_cparams(1, 8),
    )(xa, act, conv0_t, h0, lw["conv_w"], lw["conv_b"], lw["wbd"], lw["bg"], lw["lam"])


def _log_gamma(h):
    return math.log1p(-(2.0 ** (-5.0 - h)))


def _ret_kernel(q_ref, k_ref, v_ref, sg_ref, s0_ref, ng_ref, *refs, nb, clen, cps, emit_y, ids):
    if emit_y:
        y_ref, sout_ref, s_ref, dec_ref, rdec_ref = refs
    else:
        sout_ref, s_ref, dec_ref, rdec_ref = refs
    group, c, last = ids() if ids is not None else (pl.program_id(0), pl.program_id(1), pl.num_programs(1) - 1)
    rows = nb * clen
    heads = range(RET_HEADS)
    chunks = range(cps)

    @pl.when((group == 0) & (c == 0))
    def _():
        li = lax.broadcasted_iota(jnp.int32, (rows, rows), 0)
        mi = lax.broadcasted_iota(jnp.int32, (rows, rows), 1)
        keep = (li // clen == mi // clen) & (li >= mi)
        diff = jnp.where(keep, li - mi, 0).astype(F32)
        tpos = (lax.broadcasted_iota(jnp.int32, (rows, LANES), 0) % clen).astype(F32)
        for h in heads:
            lg = _log_gamma(h)
            dec_ref[h] = jnp.where(keep, jnp.exp(lg * diff), 0.0)
            rdec_ref[0, h] = jnp.exp(lg * (tpos + 1.0))
            rdec_ref[1, h] = jnp.exp(lg * (clen - 1.0 - tpos))

    @pl.when(c == 0)
    def _():
        for n in range(nb):
            s_ref[n] = s0_ref[n if s0_ref.shape[0] == nb else 0]

    if nb > 1:
        seq_of_row = lax.broadcasted_iota(jnp.int32, (rows, nb * HEAD_DIM), 0) // clen
        seq_of_col = lax.broadcasted_iota(jnp.int32, (rows, nb * HEAD_DIM), 1) // HEAD_DIM
        own = seq_of_row == seq_of_col
    tn = (((0,), (0,)), ((), ()))
    nt = (((1,), (1,)), ((), ()))
    hsl = [slice(h * HEAD_DIM, (h + 1) * HEAD_DIM) for h in heads]
    rsl = [slice(ci * rows, (ci + 1) * rows) for ci in chunks]
    q = [[q_ref[rsl[ci], sl] for sl in hsl] for ci in chunks]
    k = [[k_ref[rsl[ci], sl] for sl in hsl] for ci in chunks]
    v = [[v_ref[rsl[ci], sl] for sl in hsl] for ci in chunks]

    if emit_y:
        scores = [[lax.dot_general(q[ci][h], k[ci][h], nt, preferred_element_type=F32) for h in heads]
                  for ci in chunks]
    upd = []
    for ci in chunks:
        upd.append([])
        for h in heads:
            kd = (k[ci][h].astype(F32) * rdec_ref[1, h]).astype(BF16)
            if nb > 1:
                v_bd = jnp.where(own, jnp.concatenate([v[ci][h].astype(F32)] * nb, axis=1), 0.0).astype(BF16)
            else:
                v_bd = v[ci][h]
            upd[ci].append(lax.dot_general(kd, v_bd, tn, preferred_element_type=F32))

    state = [[[s_ref[n, h] for n in range(nb)] for h in heads]]
    for ci in chunks:
        state.append([[math.exp(clen * _log_gamma(h)) * state[ci][h][n]
                       + upd[ci][h][:, n * HEAD_DIM:(n + 1) * HEAD_DIM] for n in range(nb)] for h in heads])
    for h in heads:
        for n in range(nb):
            s_ref[n, h] = state[cps][h][n]

    if emit_y:
        inter = []
        for ci in chunks:
            inter.append([])
            for h in heads:
                s_cat = jnp.concatenate(state[ci][h], axis=1) if nb > 1 else state[ci][h][0]
                qs = jnp.dot(q[ci][h], s_cat.astype(BF16), preferred_element_type=F32)
                if nb > 1:
                    qs = jnp.where(own, qs, 0.0)
                    qs = sum(qs[:, n * HEAD_DIM:(n + 1) * HEAD_DIM] for n in range(nb))
                inter[ci].append(qs)
        for ci in chunks:
            for h in heads:
                p = (scores[ci][h] * dec_ref[h]).astype(BF16)
                o = jnp.dot(p, v[ci][h], preferred_element_type=F32) + inter[ci][h] * rdec_ref[0, h]
                mu = jnp.mean(o, axis=-1, keepdims=True)
                dev = o - mu
                var = jnp.mean(dev * dev, axis=-1, keepdims=True)
                normed = dev * lax.rsqrt(var + EPS) * ng_ref[:, hsl[h]]
                y_ref[rsl[ci], hsl[h]] = (sg_ref[rsl[ci], hsl[h]].astype(F32) * normed).astype(BF16)

    @pl.when(c == last)
    def _():
        sout_ref[...] = s_ref[...]


def _retention(act, row0, nseq, seqlen, nb, clen, cps, s0, norm_g, emit_y, ids=None):
    w = RET_HEADS * HEAD_DIM
    rows = nb * clen * cps
    nc = seqlen // (clen * cps)
    assert nb == 1 or (nc == 1 and cps == 1)
    rb0 = row0 // rows
    rmap = lambda col: (lambda b, c: (rb0 + b * nc + c, col))
    shared = s0.shape[0] != nseq
    s_map = (lambda b, c: (0, 0, 0, 0)) if shared else (lambda b, c: (b, 0, 0, 0))
    s_block = (1 if shared else nb, RET_HEADS, HEAD_DIM, HEAD_DIM)
    out_specs = [pl.BlockSpec((nb, RET_HEADS, HEAD_DIM, HEAD_DIM), lambda b, c: (b, 0, 0, 0))]
    out_shape = [jax.ShapeDtypeStruct((nseq, RET_HEADS, HEAD_DIM, HEAD_DIM), F32)]
    if emit_y:
        out_specs = [pl.BlockSpec((rows, w), lambda b, c: (b * nc + c, 0))] + out_specs
        out_shape = [jax.ShapeDtypeStruct((nseq * seqlen, w), BF16)] + out_shape
    return dict(
        name="retention",
        kernel=functools.partial(_ret_kernel, nb=nb, clen=clen, cps=cps, emit_y=emit_y, ids=ids),
        grid=(nseq // nb, nc),
        in_specs=[
            pl.BlockSpec((rows, w), rmap(_ACT_Q)),
            pl.BlockSpec((rows, w), rmap(_ACT_K)),
            pl.BlockSpec((rows, w), rmap(_ACT_V)),
            pl.BlockSpec((rows, w), rmap(_ACT_SILU)),
            pl.BlockSpec(s_block, s_map),
            pl.BlockSpec((1, w), lambda b, c: (0, 0)),
        ],
        out_specs=out_specs,
        out_shape=out_shape,
        scratch_shapes=[pltpu.VMEM((nb, RET_HEADS, HEAD_DIM, HEAD_DIM), F32),
                        pltpu.VMEM((RET_HEADS, nb * clen, nb * clen), F32),
                        pltpu.VMEM((2, RET_HEADS, nb * clen, LANES), F32)],
        args=(act, act, act, act, s0, norm_g),
        vmem_mib=12 + (5 * nb * RET_HEADS * HEAD_DIM * HEAD_DIM * 4 + 12 * rows * w * 2 >> 20))


def _outproj_kernel(x_ref, ya_ref, yb_ref, sga_ref, sgb_ref, pa_ref, pb_ref, wo_ref, o_ref):
    for r0, nr in _row_chunks(o_ref.shape[0], 256):
        rows = slice(r0, r0 + nr)
        ma = jnp.dot(ya_ref[rows, :], pa_ref[...], preferred_element_type=F32)
        mb = jnp.dot(yb_ref[rows, :], pb_ref[...], preferred_element_type=F32)
        merged = sga_ref[rows, :].astype(F32) * ma + sgb_ref[rows, :].astype(F32) * mb
        o_ref[rows, :] = x_ref[rows, :] + jnp.dot(merged.astype(BF16), wo_ref[...], preferred_element_type=F32)


def _outproj(x, ya, yb, act, pa, pb, wo, bm):
    m, d = x.shape
    w = ya.shape[1]
    nd = d // 1024
    row = lambda i: (i, 0)
    const = lambda i: (0, 0)
    single = pl.Buffered(1)
    return pl.pallas_call(
        _outproj_kernel,
        name="outproj",
        grid=(m // bm,),
        in_specs=[
            pl.BlockSpec((bm, d), row),
            pl.BlockSpec((bm, w), row),
            pl.BlockSpec((bm, w), row),
            pl.BlockSpec((bm, d), lambda i: (i, _ACT_SGA // nd)),
            pl.BlockSpec((bm, d), lambda i: (i, _ACT_SGB // nd)),
            pl.BlockSpec((w, d), const, pipeline_mode=single),
            pl.BlockSpec((w, d), const, pipeline_mode=single),
            pl.BlockSpec((d, d), const, pipeline_mode=single),
        ],
        out_specs=pl.BlockSpec((bm, d), row),
        out_shape=jax.ShapeDtypeStruct((m, d), F32),
        compiler_params=_cparams(1, 54),
    )(x, ya, yb, act, act, pa, pb, wo)


def _ffn_kernel(x_ref, g_ref, wu_ref, wd_ref, gf_ref, o_ref, hn_ref):
    j = pl.program_id(1)
    last = pl.num_programs(1) - 1

    def block(first, final):
        for r0, nr in _row_chunks(o_ref.shape[0], 512):
            rows = slice(r0, r0 + nr)
            if first:
                base = x_ref[rows, :]
                hn = _rmsnorm(base, g_ref[...]).astype(BF16)
                hn_ref[rows, :] = hn
            else:
                base = o_ref[rows, :]
                hn = hn_ref[rows, :]
            u = jnp.dot(hn, wu_ref[...], preferred_element_type=F32)
            r = jnp.square(jnp.maximum(u, 0.0)).astype(BF16)
            acc = base + jnp.dot(r, wd_ref[...], preferred_element_type=F32)
            o_ref[rows, :] = _rmsnorm(acc, gf_ref[...]) if final else acc

    pl.when(j == 0)(lambda: block(True, False))
    pl.when((j > 0) & (j < last))(lambda: block(False, False))
    pl.when(j == last)(lambda: block(False, True))


def _ffn(x, g, wu, wd, gf, bm, bf):
    m, d = x.shape
    dff = wu.shape[1]
    assert dff // bf >= 2
    return pl.pallas_call(
        _ffn_kernel,
        name="ffn",
        grid=(m // bm, dff // bf),
        in_specs=[
            pl.BlockSpec((bm, d), lambda i, j: (i, 0)),
            pl.BlockSpec((1, d), lambda i, j: (0, 0)),
            pl.BlockSpec((d, bf), lambda i, j: (0, j)),
            pl.BlockSpec((bf, d), lambda i, j: (j, 0)),
            pl.BlockSpec((1, d), lambda i, j: (0, 0)),
        ],
        out_specs=pl.BlockSpec((bm, d), lambda i, j: (i, 0)),
        out_shape=jax.ShapeDtypeStruct((m, d), F32),
        scratch_shapes=[pltpu.VMEM((bm, d), BF16)],
        compiler_params=_cparams(2, VMEM_MAX_MIB),
    )(x, g, wu, wd, gf)


def _rope_tables(pos):
    inv = ROPE_BASE ** (-np.arange(0, HEAD_DIM, 2, dtype=np.float64) / HEAD_DIM)
    ang = np.asarray(pos, np.float64)[:, None] * inv[None, :]
    cos, sin = np.cos(ang), np.sin(ang)
    return (jnp.asarray(np.concatenate([cos, cos], axis=1), F32),
            jnp.asarray(np.concatenate([-sin, sin], axis=1), F32))


def _gate_weights(wa, wx, ba, bx):
    nblk, blk, _ = wa.shape
    per = LANES // blk
    ng = nblk // per

    def bd(wt):
        wt = wt.reshape(ng, per, blk, blk)
        eye = jnp.eye(per, dtype=wt.dtype)
        return jnp.einsum("gpcd,pq->gpcqd", wt, eye).reshape(ng, LANES, LANES)

    wbd = jnp.concatenate([bd(wa), bd(wx)], axis=2).astype(BF16)
    bg = jnp.concatenate([ba.reshape(ng, 1, LANES), bx.reshape(ng, 1, LANES)], axis=2)
    return wbd, bg


def kernel(x_prompt, x_sample, state_conv, state_lru, state_ret, meta_tokens, norm_mix_g, w_in, conv_w,
           conv_b, lru_wa, lru_ba, lru_wx, lru_bx, lru_lam, ret_norm_g, p_a, p_b, w_out, norm_ffn_g,
           w_up, w_down, norm_f_g):
    assert w_in.shape[0] == 1
    nb_p, t_p, d = x_prompt.shape
    nb_s, t_s, _ = x_sample.shape
    w = conv_w.shape[-1]
    rows_p, rows_s = nb_p * t_p, nb_s * t_s

    g_mix, g_ffn, g_f = norm_mix_g[0][None], norm_ffn_g[0][None], norm_f_g[None]
    wbd, bg = _gate_weights(lru_wa[0], lru_wx[0], lru_ba[0], lru_bx[0])
    lw = dict(conv_w=conv_w[0], conv_b=conv_b[0][None], wbd=wbd, bg=bg, lam=lru_lam[0][None])
    ret_g = ret_norm_g[0][None]

    cos_p, sin_p = _rope_tables(N_META + np.arange(t_p))
    cos_s, sin_s = _rope_tables(np.concatenate([np.tile(PAST_LEN + np.arange(t_s), nb_s), np.arange(N_META)]))
    x_p2 = x_prompt.reshape(rows_p, d)
    x_s2 = x_sample.reshape(rows_s, d)
    xa_s, act_s, w_in_b = _inproj(x_s2, g_mix, w_in[0], cos_s, sin_s, rows_s + N_META, 512, 1,
                                  tail=meta_tokens.astype(x_sample.dtype))
    bm_p = 1024
    n_i, n_j = rows_p // bm_p, 8
    side_map = lambda i, j: (i, jnp.minimum(j, n_j - 1))
    later = (w_up, w_down, p_a, p_b, w_out)
    sides = tuple((wt[0], (wt.shape[1] // n_i, wt.shape[2] // n_j), side_map) for wt in later)
    xa_p, act_p, wu_b, wd_b, pa_b, pb_b, wo_b = _inproj(x_p2, g_mix, w_in_b, cos_p, sin_p, bm_p, 1024, t_p // bm_p,
                                                        side=sides)


    zc = jnp.zeros((1, CONV_W - 1, w), F32)
    zh = jnp.zeros((1, 1, w), F32)
    zs = jnp.zeros((1, RET_HEADS, HEAD_DIM, HEAD_DIM), F32)
    conv_m, h_m = _call(_lru_seq(xa_s, act_s, rows_s, 1, N_META, N_META, zc, zh, lw, emit_y=False,
                                 first_pos_zero=True))
    (s_m,) = _call(_retention(act_s, rows_s, 1, N_META, 1, N_META, 1, zs, ret_g, emit_y=False))

    lru_tr = 512
    lru_p = _lru_seq(xa_p, act_p, 0, nb_p, t_p, lru_tr, conv_m, h_m, lw, emit_y=True, first_pos_zero=False)
    steps = nb_p * (t_p // lru_tr)
    ret_s = _retention(act_s, 0, nb_s, t_s, nb_s // steps, t_s, 1, state_ret[0], ret_g, emit_y=True,
                       ids=lambda: (pl.program_id(0) * (t_p // lru_tr) + pl.program_id(1), 0, 0))
    (ya_p, conv_p, h_p), (yb_s, s_s) = _call_pair(lru_p, ret_s, "lru_seq_retention")
    yb_p, s_p = _call(_retention(act_p, 0, nb_p, t_p, 1, CHUNK, 4, s_m, ret_g, emit_y=True))
    conv0_t = jnp.transpose(state_conv[0], (1, 0, 2))
    ya_s, conv_s_t, h_s = _lru_step(xa_s, act_s, nb_s, t_s, conv0_t, state_lru[0], lw)

    x1_p = _outproj(x_p2, ya_p, yb_p, act_p, pa_b, pb_b, wo_b, 512)
    y_p = _ffn(x1_p, g_ffn, wu_b, wd_b, g_f, 1024, 1024)
    y_p, ya_s = lax.optimization_barrier((y_p, ya_s))
    x1_s = _outproj(x_s2, ya_s, yb_s, act_s, pa_b, pb_b, wo_b, 512)
    y_s = _ffn(x1_s, g_ffn, wu_b, wd_b, g_f, 1024, 1024)

    return (y_p.reshape(nb_p, t_p, d), y_s.reshape(nb_s, t_s, d),
            conv_p[None], h_p.reshape(1, nb_p, w), s_p[None],
            jnp.transpose(conv_s_t, (1, 0, 2))[None], h_s[None], s_s[None])
```

```python
import functools
import math

import jax
import jax.numpy as jnp
import numpy as np
from jax import lax
from jax.experimental import pallas as pl
from jax.experimental.pallas import tpu as pltpu

F32 = jnp.float32
BF16 = jnp.bfloat16

N_META = 16
PAST_LEN = 16384
CONV_W = 4
LRU_C = 8.0
RET_HEADS = 8
HEAD_DIM = 128
CHUNK = 128
ROPE_BASE = 10000.0
EPS = 1e-6

LANES = 128
VMEM_LIMIT = 56 << 20


def _cparams(n_axes):
    return pltpu.CompilerParams(dimension_semantics=("arbitrary",) * n_axes, vmem_limit_bytes=VMEM_LIMIT)


def _call(spec):
    return pl.pallas_call(
        spec["kernel"], name=spec["name"], grid=spec["grid"], in_specs=spec["in_specs"],
        out_specs=spec["out_specs"], out_shape=spec["out_shape"], scratch_shapes=spec["scratch_shapes"],
        compiler_params=_cparams(len(spec["grid"])))(*spec["args"])


def _call_pair(a, b, name):
    n0, n1 = a["grid"]
    assert b["grid"] == (n0 * n1, 1)

    def remap(spec):
        return pl.BlockSpec(spec.block_shape, lambda i, j, m=spec.index_map: m(i * n1 + j, 0))

    counts = [len(a["in_specs"]), len(b["in_specs"]), len(a["out_specs"]), len(b["out_specs"]),
              len(a["scratch_shapes"]), len(b["scratch_shapes"])]

    def kernel(*refs):
        groups, pos = [], 0
        for n in counts:
            groups.append(refs[pos:pos + n])
            pos += n
        a["kernel"](*groups[0], *groups[2], *groups[4])
        b["kernel"](*groups[1], *groups[3], *groups[5])

    outs = pl.pallas_call(
        kernel, name=name, grid=a["grid"],
        in_specs=list(a["in_specs"]) + [remap(sp) for sp in b["in_specs"]],
        out_specs=list(a["out_specs"]) + [remap(sp) for sp in b["out_specs"]],
        out_shape=list(a["out_shape"]) + list(b["out_shape"]),
        scratch_shapes=list(a["scratch_shapes"]) + list(b["scratch_shapes"]),
        compiler_params=_cparams(2))(*a["args"], *b["args"])
    return outs[:counts[2]], outs[counts[2]:]


def _rmsnorm(x, g):
    return x * lax.rsqrt(jnp.mean(x * x, axis=-1, keepdims=True) + EPS) * g


def _sigmoid(x):
    return 0.5 * jnp.tanh(0.5 * x) + 0.5


def _row_chunks(rows, size):
    n = max(rows // size, 1)
    return [(c * size, size if c < n - 1 else rows - c * size) for c in range(n)]


_COL_GATE, _COL_Q, _COL_K, _COL_V, _COL_G, _COL_GATE_A = 1, 2, 3, 4, 5, 6
_NUM_COL_KINDS = 10
_ACT_GELU, _ACT_Q, _ACT_K, _ACT_V, _ACT_SGA, _ACT_SGB, _ACT_SILU = 0, 1, 2, 3, 4, 6, 8


def _act_block(kind):
    return jnp.where(kind <= _COL_V, kind - 1, jnp.where(kind == _COL_G, _ACT_SILU, kind - 2))


def _inproj_kernel(x_ref, g_ref, w_ref, cos_ref, sin_ref, *refs, nper, cast_w, n_side, has_tail):
    if has_tail:
        tail_ref, refs = refs[0], refs[1:]
    side_src, refs = refs[:n_side], refs[n_side:]
    xa_ref, act_ref = refs[0], refs[1]
    refs = refs[2:]
    if cast_w:
        wb_ref, refs = refs[0], refs[1:]
    side_dst, (xn_ref,) = refs[:n_side], refs[n_side:]
    j = pl.program_id(1)
    kind = j // nper
    bn = xa_ref.shape[1]

    for src, dst in zip(side_src, side_dst):
        dst[...] = src[...].astype(BF16)
    if cast_w:
        wb_ref[...] = w_ref[...].astype(BF16)
        w_bf = wb_ref
    else:
        w_bf = w_ref

    def project(epilogue, normalize=False):
        if normalize:
            nx = x_ref.shape[0]
            xn_ref[0:nx, :] = _rmsnorm(x_ref[...], g_ref[...]).astype(BF16)
            if has_tail:
                xn_ref[nx:, :] = _rmsnorm(tail_ref[...], g_ref[...]).astype(BF16)
        epilogue(jnp.dot(xn_ref[...], w_bf[...], preferred_element_type=F32))

    def to_xa(acc):
        xa_ref[...] = acc

    def rope(scale):
        def epilogue(acc):
            bm = acc.shape[0]
            t0 = pl.multiple_of((pl.program_id(0) % (cos_ref.shape[0] // bm)) * bm, bm)
            cos, sin = cos_ref[pl.ds(t0, bm), :], sin_ref[pl.ds(t0, bm), :]
            for h in range(bn // HEAD_DIM):
                a = acc[:, h * HEAD_DIM:(h + 1) * HEAD_DIM]
                r = a * cos + pltpu.roll(a, HEAD_DIM // 2, axis=1) * sin
                if scale is not None:
                    r = r * scale
                act_ref[:, h * HEAD_DIM:(h + 1) * HEAD_DIM] = r.astype(BF16)
        return epilogue

    def elementwise(fn):
        def epilogue(acc):
            act_ref[...] = fn(acc).astype(BF16)
        return epilogue

    pl.when(j == 0)(lambda: project(to_xa, normalize=True))
    if nper > 1:
        pl.when((j > 0) & (kind == 0))(lambda: project(to_xa))
    pl.when(kind == _COL_GATE)(lambda: project(elementwise(jax.nn.gelu)))
    pl.when(kind == _COL_Q)(lambda: project(rope(None)))
    pl.when(kind == _COL_K)(lambda: project(rope(HEAD_DIM ** -0.5)))
    pl.when(kind == _COL_V)(lambda: project(elementwise(lambda a: a)))
    pl.when(kind == _COL_G)(lambda: project(elementwise(lambda a: a * _sigmoid(a))))
    pl.when(kind >= _COL_GATE_A)(lambda: project(elementwise(_sigmoid)))


def _inproj(x, g, w, cos, sin, bm, bn, table_blocks, side=(), tail=None):
    xrows, d = x.shape
    m = xrows if tail is None else xrows + tail.shape[0]
    assert tail is None or m == bm
    n = w.shape[1]
    width = n // _NUM_COL_KINDS
    nper = width // bn
    cast_w = w.dtype != BF16

    def act_map(i, j):
        kind = j // nper
        return i, jnp.where(kind == 0, 0, _act_block(kind) * nper + j % nper)

    in_specs = [
        pl.BlockSpec((min(bm, xrows), d), lambda i, j: (i, 0)),
        pl.BlockSpec((1, d), lambda i, j: (0, 0)),
        pl.BlockSpec((d, bn), lambda i, j: (0, j)),
        pl.BlockSpec((table_blocks * bm, LANES), lambda i, j: (0, 0)),
        pl.BlockSpec((table_blocks * bm, LANES), lambda i, j: (0, 0)),
    ]
    if tail is not None:
        in_specs.append(pl.BlockSpec(tail.shape, lambda i, j: (0, 0)))
    out_specs = [
        pl.BlockSpec((bm, bn), lambda i, j: (i, jnp.minimum(j, nper - 1))),
        pl.BlockSpec((bm, bn), act_map),
    ]
    out_shape = [jax.ShapeDtypeStruct((m, width), F32), jax.ShapeDtypeStruct((m, n - width), BF16)]
    if cast_w:
        out_specs.append(pl.BlockSpec((d, bn), lambda i, j: (0, j)))
        out_shape.append(jax.ShapeDtypeStruct(w.shape, BF16))
    for arr, block, index_map in side:
        in_specs.append(pl.BlockSpec(block, index_map))
        out_specs.append(pl.BlockSpec(block, index_map))
        out_shape.append(jax.ShapeDtypeStruct(arr.shape, BF16))
    return pl.pallas_call(
        functools.partial(_inproj_kernel, nper=nper, cast_w=cast_w, n_side=len(side), has_tail=tail is not None),
        name="inproj",
        grid=(m // bm, n // bn),
        in_specs=in_specs,
        out_specs=out_specs,
        out_shape=out_shape,
        scratch_shapes=[pltpu.VMEM((bm, d), BF16)],
        compiler_params=_cparams(2),
    )(x, g, w, cos, sin, *([] if tail is None else [tail]), *[arr for arr, _, _ in side])


def _lru_coeffs(xc, wbd_ref, bg_ref, cl_row, group):
    gates = jnp.dot(xc.astype(BF16), wbd_ref[group], preferred_element_type=F32) + bg_ref[group]
    rec = _sigmoid(gates[:, :LANES])
    ing = _sigmoid(gates[:, LANES:])
    log_a = rec * cl_row
    a = jnp.exp(log_a)
    mult = jnp.sqrt(jnp.tanh(-log_a) * (a * a + 1.0))
    return a, mult, ing


def _lru_seq_kernel(xa_ref, ga_ref, conv0_ref, h0_ref, cw_ref, cb_ref, wbd_ref, bg_ref, lam_ref,
                    *refs, tr, emit_y, first_pos_zero):
    if emit_y:
        ya_ref, convout_ref, hout_ref, halo_ref, h_ref, a_ref, b_ref = refs
    else:
        convout_ref, hout_ref, halo_ref, h_ref, a_ref, b_ref = refs
    t = pl.program_id(1)
    halo = 8

    @pl.when(t == 0)
    def _():
        halo_ref[0:halo - 3, :] = jnp.zeros((halo - 3, halo_ref.shape[1]), F32)
        halo_ref[halo - 3:halo, :] = conv0_ref[0]
        h_ref[...] = h0_ref[0]

    xa = xa_ref[...]
    cw = [cw_ref[CONV_W - 1 - k:CONV_W - k, :] for k in range(CONV_W)]
    xc = cb_ref[...] + cw[0] * xa + sum(cw[k] * pltpu.roll(xa, k, axis=0) for k in range(1, CONV_W))
    window = jnp.concatenate([halo_ref[0:halo, :], xa[0:halo, :]], axis=0)
    first = cb_ref[...] + sum(cw[k] * window[halo - k:2 * halo - k, :] for k in range(CONV_W))
    xc = jnp.concatenate([first, xc[halo:, :]], axis=0)
    halo_ref[0:halo, :] = xa[tr - halo:, :]

    cl = LRU_C * jax.nn.log_sigmoid(lam_ref[...])
    for g in range(xc.shape[1] // LANES):
        sl = slice(g * LANES, (g + 1) * LANES)
        a, mult, ing = _lru_coeffs(xc[:, sl], wbd_ref, bg_ref, cl[:, sl], g)
        if first_pos_zero:
            row = lax.broadcasted_iota(jnp.int32, a.shape, 0) + t * tr
            mult = jnp.where(row == 0, 1.0, mult)
        a_ref[:, sl] = a
        b_ref[:, sl] = mult * ing * xc[:, sl]

    h = h_ref[...]
    for r in range(tr):
        h = a_ref[r:r + 1, :] * h + b_ref[r:r + 1, :]
        halo_ref[halo + r:halo + r + 1, :] = h
    h_ref[...] = h
    if emit_y:
        ya_ref[...] = (halo_ref[halo:halo + tr, :] * ga_ref[...].astype(F32)).astype(BF16)

    @pl.when(t == pl.num_programs(1) - 1)
    def _():
        convout_ref[0] = halo_ref[halo - 3:halo, :]
        hout_ref[0] = h_ref[...]


def _lru_seq(xa, act, row0, nseq, seqlen, tr, conv0, h0, lw, emit_y, first_pos_zero):
    w = xa.shape[1]
    nt = seqlen // tr
    rb0 = row0 // tr
    ng = w // LANES
    state_map = (lambda b, t: (b, 0, 0)) if conv0.shape[0] == nseq else (lambda b, t: (0, 0, 0))
    const2 = lambda b, t: (0, 0)
    const3 = lambda b, t: (0, 0, 0)
    out_specs = [pl.BlockSpec((1, CONV_W - 1, w), lambda b, t: (b, 0, 0)),
                 pl.BlockSpec((1, 1, w), lambda b, t: (b, 0, 0))]
    out_shape = [jax.ShapeDtypeStruct((nseq, CONV_W - 1, w), F32), jax.ShapeDtypeStruct((nseq, 1, w), F32)]
    if emit_y:
        out_specs = [pl.BlockSpec((tr, w), lambda b, t: (b * nt + t, 0))] + out_specs
        out_shape = [jax.ShapeDtypeStruct((nseq * seqlen, w), BF16)] + out_shape
    return dict(
        name="lru_seq",
        kernel=functools.partial(_lru_seq_kernel, tr=tr, emit_y=emit_y, first_pos_zero=first_pos_zero),
        grid=(nseq, nt),
        in_specs=[
            pl.BlockSpec((tr, w), lambda b, t: (rb0 + b * nt + t, 0)),
            pl.BlockSpec((tr, w), lambda b, t: (rb0 + b * nt + t, _ACT_GELU)),
            pl.BlockSpec((1, CONV_W - 1, w), state_map),
            pl.BlockSpec((1, 1, w), state_map),
            pl.BlockSpec((CONV_W, w), const2),
            pl.BlockSpec((1, w), const2),
            pl.BlockSpec((ng, LANES, 2 * LANES), const3),
            pl.BlockSpec((ng, 1, 2 * LANES), const3),
            pl.BlockSpec((1, w), const2),
        ],
        out_specs=out_specs,
        out_shape=out_shape,
        scratch_shapes=[pltpu.VMEM((8 + tr, w), F32), pltpu.VMEM((1, w), F32),
                        pltpu.VMEM((tr, w), F32), pltpu.VMEM((tr, w), F32)],
        args=(xa, act, conv0, h0, lw["conv_w"], lw["conv_b"], lw["wbd"], lw["bg"], lw["lam"]))


def _lru_step_kernel(xa_ref, ga_ref, conv0_ref, h0_ref, cw_ref, cb_ref, wbd_ref, bg_ref, lam_ref,
                     ya_ref, convout_ref, hout_ref, hs_ref, *, nseq, seqlen):
    cl = LRU_C * jax.nn.log_sigmoid(lam_ref[...])
    full = [conv0_ref[j] for j in range(CONV_W - 1)]
    full += [xa_ref[pl.ds(t, nseq, stride=seqlen), :] for t in range(seqlen)]
    h = h0_ref[...]
    for t in range(seqlen):
        xc = cb_ref[...] + cw_ref[0:1, :] * full[t]
        for j in range(1, CONV_W):
            xc = xc + cw_ref[j:j + 1, :] * full[t + j]
        a, mult, ing = _lru_coeffs(xc, wbd_ref, bg_ref, cl, 0)
        h = a * h + mult * ing * xc
        hs_ref[pl.ds(t, nseq, stride=seqlen), :] = h
    ya_ref[...] = (hs_ref[...] * ga_ref[...].astype(F32)).astype(BF16)
    for j in range(CONV_W - 1):
        convout_ref[j] = full[seqlen + j]
    hout_ref[...] = h


def _lru_step(xa, act, nseq, seqlen, conv0_t, h0, lw):
    w = xa.shape[1]
    rows = nseq * seqlen
    ng = w // LANES
    col = lambda g: (0, g)
    return pl.pallas_call(
        functools.partial(_lru_step_kernel, nseq=nseq, seqlen=seqlen),
        name="lru_step",
        grid=(ng,),
        in_specs=[
            pl.BlockSpec((rows, LANES), col),
            pl.BlockSpec((rows, LANES), col),
            pl.BlockSpec((CONV_W - 1, nseq, LANES), lambda g: (0, 0, g)),
            pl.BlockSpec((nseq, LANES), col),
            pl.BlockSpec((CONV_W, LANES), col),
            pl.BlockSpec((1, LANES), col),
            pl.BlockSpec((1, LANES, 2 * LANES), lambda g: (g, 0, 0)),
            pl.BlockSpec((1, 1, 2 * LANES), lambda g: (g, 0, 0)),
            pl.BlockSpec((1, LANES), col),
        ],
        out_specs=[
            pl.BlockSpec((rows, LANES), col),
            pl.BlockSpec((CONV_W - 1, nseq, LANES), lambda g: (0, 0, g)),
            pl.BlockSpec((nseq, LANES), col),
        ],
        out_shape=[jax.ShapeDtypeStruct((rows, w), BF16),
                   jax.ShapeDtypeStruct((CONV_W - 1, nseq, w), F32),
                   jax.ShapeDtypeStruct((nseq, w), F32)],
        scratch_shapes=[pltpu.VMEM((rows, LANES), F32)],
        compiler_params=_cparams(1),
    )(xa, act, conv0_t, h0, lw["conv_w"], lw["conv_b"], lw["wbd"], lw["bg"], lw["lam"])


def _log_gamma(h):
    return math.log1p(-(2.0 ** (-5.0 - h)))


def _ret_kernel(q_ref, k_ref, v_ref, sg_ref, s0_ref, ng_ref, *refs, nb, clen, cps, emit_y, ids):
    if emit_y:
        y_ref, sout_ref, s_ref, dec_ref, rdec_ref = refs
    else:
        sout_ref, s_ref, dec_ref, rdec_ref = refs
    group, c, last = ids() if ids is not None else (pl.program_id(0), pl.program_id(1), pl.num_programs(1) - 1)
    rows = nb * clen
    heads = range(RET_HEADS)
    chunks = range(cps)

    @pl.when((group == 0) & (c == 0))
    def _():
        li = lax.broadcasted_iota(jnp.int32, (rows, rows), 0)
        mi = lax.broadcasted_iota(jnp.int32, (rows, rows), 1)
        keep = (li // clen == mi // clen) & (li >= mi)
        diff = jnp.where(keep, li - mi, 0).astype(F32)
        tpos = (lax.broadcasted_iota(jnp.int32, (rows, LANES), 0) % clen).astype(F32)
        for h in heads:
            lg = _log_gamma(h)
            dec_ref[h] = jnp.where(keep, jnp.exp(lg * diff), 0.0)
            rdec_ref[0, h] = jnp.exp(lg * (tpos + 1.0))
            rdec_ref[1, h] = jnp.exp(lg * (clen - 1.0 - tpos))

    @pl.when(c == 0)
    def _():
        for n in range(nb):
            s_ref[n] = s0_ref[n if s0_ref.shape[0] == nb else 0]

    if nb > 1:
        seq_of_row = lax.broadcasted_iota(jnp.int32, (rows, nb * HEAD_DIM), 0) // clen
        seq_of_col = lax.broadcasted_iota(jnp.int32, (rows, nb * HEAD_DIM), 1) // HEAD_DIM
        own = seq_of_row == seq_of_col
    tn = (((0,), (0,)), ((), ()))
    nt = (((1,), (1,)), ((), ()))
    hsl = [slice(h * HEAD_DIM, (h + 1) * HEAD_DIM) for h in heads]
    rsl = [slice(ci * rows, (ci + 1) * rows) for ci in chunks]
    q = [[q_ref[rsl[ci], sl] for sl in hsl] for ci in chunks]
    k = [[k_ref[rsl[ci], sl] for sl in hsl] for ci in chunks]
    v = [[v_ref[rsl[ci], sl] for sl in hsl] for ci in chunks]

    if emit_y:
        scores = [[lax.dot_general(q[ci][h], k[ci][h], nt, preferred_element_type=F32) for h in heads]
                  for ci in chunks]
    upd = []
    for ci in chunks:
        upd.append([])
        for h in heads:
            kd = (k[ci][h].astype(F32) * rdec_ref[1, h]).astype(BF16)
            if nb > 1:
                v_bd = jnp.where(own, jnp.concatenate([v[ci][h].astype(F32)] * nb, axis=1), 0.0).astype(BF16)
            else:
                v_bd = v[ci][h]
            upd[ci].append(lax.dot_general(kd, v_bd, tn, preferred_element_type=F32))

    state = [[[s_ref[n, h] for n in range(nb)] for h in heads]]
    for ci in chunks:
        state.append([[math.exp(clen * _log_gamma(h)) * state[ci][h][n]
                       + upd[ci][h][:, n * HEAD_DIM:(n + 1) * HEAD_DIM] for n in range(nb)] for h in heads])
    for h in heads:
        for n in range(nb):
            s_ref[n, h] = state[cps][h][n]

    if emit_y:
        inter = []
        for ci in chunks:
            inter.append([])
            for h in heads:
                s_cat = jnp.concatenate(state[ci][h], axis=1) if nb > 1 else state[ci][h][0]
                qs = jnp.dot(q[ci][h], s_cat.astype(BF16), preferred_element_type=F32)
                if nb > 1:
                    qs = jnp.where(own, qs, 0.0)
                    qs = sum(qs[:, n * HEAD_DIM:(n + 1) * HEAD_DIM] for n in range(nb))
                inter[ci].append(qs)
        for ci in chunks:
            for h in heads:
                p = (scores[ci][h] * dec_ref[h]).astype(BF16)
                o = jnp.dot(p, v[ci][h], preferred_element_type=F32) + inter[ci][h] * rdec_ref[0, h]
                mu = jnp.mean(o, axis=-1, keepdims=True)
                dev = o - mu
                var = jnp.mean(dev * dev, axis=-1, keepdims=True)
                normed = dev * lax.rsqrt(var + EPS) * ng_ref[:, hsl[h]]
                y_ref[rsl[ci], hsl[h]] = (sg_ref[rsl[ci], hsl[h]].astype(F32) * normed).astype(BF16)

    @pl.when(c == last)
    def _():
        sout_ref[...] = s_ref[...]


def _retention(act, row0, nseq, seqlen, nb, clen, cps, s0, norm_g, emit_y, ids=None):
    w = RET_HEADS * HEAD_DIM
    rows = nb * clen * cps
    nc = seqlen // (clen * cps)
    assert nb == 1 or (nc == 1 and cps == 1)
    rb0 = row0 // rows
    rmap = lambda col: (lambda b, c: (rb0 + b * nc + c, col))
    shared = s0.shape[0] != nseq
    s_map = (lambda b, c: (0, 0, 0, 0)) if shared else (lambda b, c: (b, 0, 0, 0))
    s_block = (1 if shared else nb, RET_HEADS, HEAD_DIM, HEAD_DIM)
    out_specs = [pl.BlockSpec((nb, RET_HEADS, HEAD_DIM, HEAD_DIM), lambda b, c: (b, 0, 0, 0))]
    out_shape = [jax.ShapeDtypeStruct((nseq, RET_HEADS, HEAD_DIM, HEAD_DIM), F32)]
    if emit_y:
        out_specs = [pl.BlockSpec((rows, w), lambda b, c: (b * nc + c, 0))] + out_specs
        out_shape = [jax.ShapeDtypeStruct((nseq * seqlen, w), BF16)] + out_shape
    return dict(
        name="retention",
        kernel=functools.partial(_ret_kernel, nb=nb, clen=clen, cps=cps, emit_y=emit_y, ids=ids),
        grid=(nseq // nb, nc),
        in_specs=[
            pl.BlockSpec((rows, w), rmap(_ACT_Q)),
            pl.BlockSpec((rows, w), rmap(_ACT_K)),
            pl.BlockSpec((rows, w), rmap(_ACT_V)),
            pl.BlockSpec((rows, w), rmap(_ACT_SILU)),
            pl.BlockSpec(s_block, s_map),
            pl.BlockSpec((1, w), lambda b, c: (0, 0)),
        ],
        out_specs=out_specs,
        out_shape=out_shape,
        scratch_shapes=[pltpu.VMEM((nb, RET_HEADS, HEAD_DIM, HEAD_DIM), F32),
                        pltpu.VMEM((RET_HEADS, nb * clen, nb * clen), F32),
                        pltpu.VMEM((2, RET_HEADS, nb * clen, LANES), F32)],
        args=(act, act, act, act, s0, norm_g))


def _outproj_kernel(x_ref, ya_ref, yb_ref, sga_ref, sgb_ref, pa_ref, pb_ref, wo_ref, o_ref):
    for r0, nr in _row_chunks(o_ref.shape[0], 256):
        rows = slice(r0, r0 + nr)
        ma = jnp.dot(ya_ref[rows, :], pa_ref[...], preferred_element_type=F32)
        mb = jnp.dot(yb_ref[rows, :], pb_ref[...], preferred_element_type=F32)
        merged = sga_ref[rows, :].astype(F32) * ma + sgb_ref[rows, :].astype(F32) * mb
        o_ref[rows, :] = x_ref[rows, :] + jnp.dot(merged.astype(BF16), wo_ref[...], preferred_element_type=F32)


def _outproj(x, ya, yb, act, pa, pb, wo, bm):
    m, d = x.shape
    w = ya.shape[1]
    nd = d // 1024
    row = lambda i: (i, 0)
    const = lambda i: (0, 0)
    single = pl.Buffered(1)
    return pl.pallas_call(
        _outproj_kernel,
        name="outproj",
        grid=(m // bm,),
        in_specs=[
            pl.BlockSpec((bm, d), row),
            pl.BlockSpec((bm, w), row),
            pl.BlockSpec((bm, w), row),
            pl.BlockSpec((bm, d), lambda i: (i, _ACT_SGA // nd)),
            pl.BlockSpec((bm, d), lambda i: (i, _ACT_SGB // nd)),
            pl.BlockSpec((w, d), const, pipeline_mode=single),
            pl.BlockSpec((w, d), const, pipeline_mode=single),
            pl.BlockSpec((d, d), const, pipeline_mode=single),
        ],
        out_specs=pl.BlockSpec((bm, d), row),
        out_shape=jax.ShapeDtypeStruct((m, d), F32),
        compiler_params=_cparams(1),
    )(x, ya, yb, act, act, pa, pb, wo)


def _ffn_kernel(x_ref, g_ref, wu_ref, wd_ref, gf_ref, o_ref, hn_ref):
    j = pl.program_id(1)
    last = pl.num_programs(1) - 1

    def block(first, final):
        for r0, nr in _row_chunks(o_ref.shape[0], 512):
            rows = slice(r0, r0 + nr)
            if first:
                base = x_ref[rows, :]
                hn = _rmsnorm(base, g_ref[...]).astype(BF16)
                hn_ref[rows, :] = hn
            else:
                base = o_ref[rows, :]
                hn = hn_ref[rows, :]
            u = jnp.dot(hn, wu_ref[...], preferred_element_type=F32)
            r = jnp.square(jnp.maximum(u, 0.0)).astype(BF16)
            acc = base + jnp.dot(r, wd_ref[...], preferred_element_type=F32)
            o_ref[rows, :] = _rmsnorm(acc, gf_ref[...]) if final else acc

    pl.when(j == 0)(lambda: block(True, False))
    pl.when((j > 0) & (j < last))(lambda: block(False, False))
    pl.when(j == last)(lambda: block(False, True))


def _ffn(x, g, wu, wd, gf, bm, bf):
    m, d = x.shape
    dff = wu.shape[1]
    assert dff // bf >= 2
    return pl.pallas_call(
        _ffn_kernel,
        name="ffn",
        grid=(m // bm, dff // bf),
        in_specs=[
            pl.BlockSpec((bm, d), lambda i, j: (i, 0)),
            pl.BlockSpec((1, d), lambda i, j: (0, 0)),
            pl.BlockSpec((d, bf), lambda i, j: (0, j)),
            pl.BlockSpec((bf, d), lambda i, j: (j, 0)),
            pl.BlockSpec((1, d), lambda i, j: (0, 0)),
        ],
        out_specs=pl.BlockSpec((bm, d), lambda i, j: (i, 0)),
        out_shape=jax.ShapeDtypeStruct((m, d), F32),
        scratch_shapes=[pltpu.VMEM((bm, d), BF16)],
        compiler_params=_cparams(2),
    )(x, g, wu, wd, gf)


def _rope_tables(pos):
    inv = ROPE_BASE ** (-np.arange(0, HEAD_DIM, 2, dtype=np.float64) / HEAD_DIM)
    ang = np.asarray(pos, np.float64)[:, None] * inv[None, :]
    cos, sin = np.cos(ang), np.sin(ang)
    return (jnp.asarray(np.concatenate([cos, cos], axis=1), F32),
            jnp.asarray(np.concatenate([-sin, sin], axis=1), F32))


def _gate_weights(wa, wx, ba, bx):
    nblk, blk, _ = wa.shape
    per = LANES // blk
    ng = nblk // per

    def bd(wt):
        wt = wt.reshape(ng, per, blk, blk)
        eye = jnp.eye(per, dtype=wt.dtype)
        return jnp.einsum("gpcd,pq->gpcqd", wt, eye).reshape(ng, LANES, LANES)

    wbd = jnp.concatenate([bd(wa), bd(wx)], axis=2).astype(BF16)
    bg = jnp.concatenate([ba.reshape(ng, 1, LANES), bx.reshape(ng, 1, LANES)], axis=2)
    return wbd, bg


def kernel(x_prompt, x_sample, state_conv, state_lru, state_ret, meta_tokens, norm_mix_g, w_in, conv_w,
           conv_b, lru_wa, lru_ba, lru_wx, lru_bx, lru_lam, ret_norm_g, p_a, p_b, w_out, norm_ffn_g,
           w_up, w_down, norm_f_g):
    assert w_in.shape[0] == 1
    nb_p, t_p, d = x_prompt.shape
    nb_s, t_s, _ = x_sample.shape
    w = conv_w.shape[-1]
    rows_p, rows_s = nb_p * t_p, nb_s * t_s

    g_mix, g_ffn, g_f = norm_mix_g[0][None], norm_ffn_g[0][None], norm_f_g[None]
    wbd, bg = _gate_weights(lru_wa[0], lru_wx[0], lru_ba[0], lru_bx[0])
    lw = dict(conv_w=conv_w[0], conv_b=conv_b[0][None], wbd=wbd, bg=bg, lam=lru_lam[0][None])
    ret_g = ret_norm_g[0][None]

    cos_p, sin_p = _rope_tables(N_META + np.arange(t_p))
    cos_s, sin_s = _rope_tables(np.concatenate([np.tile(PAST_LEN + np.arange(t_s), nb_s), np.arange(N_META)]))
    x_p2 = x_prompt.reshape(rows_p, d)
    x_s2 = x_sample.reshape(rows_s, d)
    xa_s, act_s, w_in_b = _inproj(x_s2, g_mix, w_in[0], cos_s, sin_s, rows_s + N_META, 512, 1,
                                  tail=meta_tokens.astype(x_sample.dtype))
    bm_p = 1024
    n_i, n_j = rows_p // bm_p, 8
    side_map = lambda i, j: (i, jnp.minimum(j, n_j - 1))
    later = (w_up, w_down, p_a, p_b, w_out)
    sides = tuple((wt[0], (wt.shape[1] // n_i, wt.shape[2] // n_j), side_map) for wt in later)
    xa_p, act_p, wu_b, wd_b, pa_b, pb_b, wo_b = _inproj(x_p2, g_mix, w_in_b, cos_p, sin_p, bm_p, 1024, t_p // bm_p,
                                                        side=sides)


    zc = jnp.zeros((1, CONV_W - 1, w), F32)
    zh = jnp.zeros((1, 1, w), F32)
    zs = jnp.zeros((1, RET_HEADS, HEAD_DIM, HEAD_DIM), F32)
    conv_m, h_m = _call(_lru_seq(xa_s, act_s, rows_s, 1, N_META, N_META, zc, zh, lw, emit_y=False,
                                 first_pos_zero=True))
    (s_m,) = _call(_retention(act_s, rows_s, 1, N_META, 1, N_META, 1, zs, ret_g, emit_y=False))

    lru_tr = 512
    lru_p = _lru_seq(xa_p, act_p, 0, nb_p, t_p, lru_tr, conv_m, h_m, lw, emit_y=True, first_pos_zero=False)
    steps = nb_p * (t_p // lru_tr)
    ret_s = _retention(act_s, 0, nb_s, t_s, nb_s // steps, t_s, 1, state_ret[0], ret_g, emit_y=True,
                       ids=lambda: (pl.program_id(0) * (t_p // lru_tr) + pl.program_id(1), 0, 0))
    (ya_p, conv_p, h_p), (yb_s, s_s) = _call_pair(lru_p, ret_s, "lru_seq_retention")
    yb_p, s_p = _call(_retention(act_p, 0, nb_p, t_p, 1, CHUNK, 4, s_m, ret_g, emit_y=True))
    conv0_t = jnp.transpose(state_conv[0], (1, 0, 2))
    ya_s, conv_s_t, h_s = _lru_step(xa_s, act_s, nb_s, t_s, conv0_t, state_lru[0], lw)

    x1_p = _outproj(x_p2, ya_p, yb_p, act_p, pa_b, pb_b, wo_b, 512)
    y_p = _ffn(x1_p, g_ffn, wu_b, wd_b, g_f, 1024, 1024)
    y_p, ya_s = lax.optimization_barrier((y_p, ya_s))
    x1_s = _outproj(x_s2, ya_s, yb_s, act_s, pa_b, pb_b, wo_b, 512)
    y_s = _ffn(x1_s, g_ffn, wu_b, wd_b, g_f, 1024, 1024)

    return (y_p.reshape(nb_p, t_p, d), y_s.reshape(nb_s, t_s, d),
            conv_p[None], h_p.reshape(1, nb_p, w), s_p[None],
            jnp.transpose(conv_s_t, (1, 0, 2))[None], h_s[None], s_s[None])
```

```python
import functools
import math

import jax
import jax.numpy as jnp
import numpy as np
from jax import lax
from jax.experimental import pallas as pl
from jax.experimental.pallas import tpu as pltpu

F32 = jnp.float32
BF16 = jnp.bfloat16

N_META = 16
PAST_LEN = 16384
CONV_W = 4
LRU_C = 8.0
RET_HEADS = 8
HEAD_DIM = 128
CHUNK = 128
ROPE_BASE = 10000.0
EPS = 1e-6

LANES = 128
VMEM_LIMIT = 56 << 20


def _cparams(n_axes):
    return pltpu.CompilerParams(dimension_semantics=("arbitrary",) * n_axes, vmem_limit_bytes=VMEM_LIMIT)


def _call(spec):
    return pl.pallas_call(
        spec["kernel"], name=spec["name"], grid=spec["grid"], in_specs=spec["in_specs"],
        out_specs=spec["out_specs"], out_shape=spec["out_shape"], scratch_shapes=spec["scratch_shapes"],
        compiler_params=_cparams(len(spec["grid"])))(*spec["args"])


def _call_pair(a, b, name):
    n0, n1 = a["grid"]
    assert b["grid"] == (n0 * n1, 1)

    def remap(spec):
        return pl.BlockSpec(spec.block_shape, lambda i, j, m=spec.index_map: m(i * n1 + j, 0))

    counts = [len(a["in_specs"]), len(b["in_specs"]), len(a["out_specs"]), len(b["out_specs"]),
              len(a["scratch_shapes"]), len(b["scratch_shapes"])]

    def kernel(*refs):
        groups, pos = [], 0
        for n in counts:
            groups.append(refs[pos:pos + n])
            pos += n
        a["kernel"](*groups[0], *groups[2], *groups[4])
        b["kernel"](*groups[1], *groups[3], *groups[5])

    outs = pl.pallas_call(
        kernel, name=name, grid=a["grid"],
        in_specs=list(a["in_specs"]) + [remap(sp) for sp in b["in_specs"]],
        out_specs=list(a["out_specs"]) + [remap(sp) for sp in b["out_specs"]],
        out_shape=list(a["out_shape"]) + list(b["out_shape"]),
        scratch_shapes=list(a["scratch_shapes"]) + list(b["scratch_shapes"]),
        compiler_params=_cparams(2))(*a["args"], *b["args"])
    return outs[:counts[2]], outs[counts[2]:]


def _rmsnorm(x, g):
    return x * lax.rsqrt(jnp.mean(x * x, axis=-1, keepdims=True) + EPS) * g


def _sigmoid(x):
    return 0.5 * jnp.tanh(0.5 * x) + 0.5


def _row_chunks(rows, size):
    n = max(rows // size, 1)
    return [(c * size, size if c < n - 1 else rows - c * size) for c in range(n)]


_COL_GATE, _COL_Q, _COL_K, _COL_V, _COL_G, _COL_GATE_A = 1, 2, 3, 4, 5, 6
_NUM_COL_KINDS = 10
_ACT_GELU, _ACT_Q, _ACT_K, _ACT_V, _ACT_SGA, _ACT_SGB, _ACT_SILU = 0, 1, 2, 3, 4, 6, 8


def _act_block(kind):
    return jnp.where(kind <= _COL_V, kind - 1, jnp.where(kind == _COL_G, _ACT_SILU, kind - 2))


def _inproj_kernel(x_ref, g_ref, w_ref, cos_ref, sin_ref, *refs, nper, cast_w, n_side, has_tail):
    if has_tail:
        tail_ref, refs = refs[0], refs[1:]
    side_src, refs = refs[:n_side], refs[n_side:]
    xa_ref, act_ref = refs[0], refs[1]
    refs = refs[2:]
    if cast_w:
        wb_ref, refs = refs[0], refs[1:]
    side_dst, (xn_ref,) = refs[:n_side], refs[n_side:]
    j = pl.program_id(1)
    kind = j // nper
    bn = xa_ref.shape[1]

    for src, dst in zip(side_src, side_dst):
        dst[...] = src[...].astype(BF16)
    if cast_w:
        wb_ref[...] = w_ref[...].astype(BF16)
        w_bf = wb_ref
    else:
        w_bf = w_ref

    def project(epilogue, normalize=False):
        if normalize:
            nx = x_ref.shape[0]
            xn_ref[0:nx, :] = _rmsnorm(x_ref[...], g_ref[...]).astype(BF16)
            if has_tail:
                xn_ref[nx:, :] = _rmsnorm(tail_ref[...], g_ref[...]).astype(BF16)
        epilogue(jnp.dot(xn_ref[...], w_bf[...], preferred_element_type=F32))

    def to_xa(acc):
        xa_ref[...] = acc

    def rope(scale):
        def epilogue(acc):
            bm = acc.shape[0]
            t0 = pl.multiple_of((pl.program_id(0) % (cos_ref.shape[0] // bm)) * bm, bm)
            cos, sin = cos_ref[pl.ds(t0, bm), :], sin_ref[pl.ds(t0, bm), :]
            for h in range(bn // HEAD_DIM):
                a = acc[:, h * HEAD_DIM:(h + 1) * HEAD_DIM]
                r = a * cos + pltpu.roll(a, HEAD_DIM // 2, axis=1) * sin
                if scale is not None:
                    r = r * scale
                act_ref[:, h * HEAD_DIM:(h + 1) * HEAD_DIM] = r.astype(BF16)
        return epilogue

    def elementwise(fn):
        def epilogue(acc):
            act_ref[...] = fn(acc).astype(BF16)
        return epilogue

    pl.when(j == 0)(lambda: project(to_xa, normalize=True))
    if nper > 1:
        pl.when((j > 0) & (kind == 0))(lambda: project(to_xa))
    pl.when(kind == _COL_GATE)(lambda: project(elementwise(jax.nn.gelu)))
    pl.when(kind == _COL_Q)(lambda: project(rope(None)))
    pl.when(kind == _COL_K)(lambda: project(rope(HEAD_DIM ** -0.5)))
    pl.when(kind == _COL_V)(lambda: project(elementwise(lambda a: a)))
    pl.when(kind == _COL_G)(lambda: project(elementwise(lambda a: a * _sigmoid(a))))
    pl.when(kind >= _COL_GATE_A)(lambda: project(elementwise(_sigmoid)))


def _inproj(x, g, w, cos, sin, bm, bn, table_blocks, side=(), tail=None):
    xrows, d = x.shape
    m = xrows if tail is None else xrows + tail.shape[0]
    assert tail is None or m == bm
    n = w.shape[1]
    width = n // _NUM_COL_KINDS
    nper = width // bn
    cast_w = w.dtype != BF16

    def act_map(i, j):
        kind = j // nper
        return i, jnp.where(kind == 0, 0, _act_block(kind) * nper + j % nper)

    in_specs = [
        pl.BlockSpec((min(bm, xrows), d), lambda i, j: (i, 0)),
        pl.BlockSpec((1, d), lambda i, j: (0, 0)),
        pl.BlockSpec((d, bn), lambda i, j: (0, j)),
        pl.BlockSpec((table_blocks * bm, LANES), lambda i, j: (0, 0)),
        pl.BlockSpec((table_blocks * bm, LANES), lambda i, j: (0, 0)),
    ]
    if tail is not None:
        in_specs.append(pl.BlockSpec(tail.shape, lambda i, j: (0, 0)))
    out_specs = [
        pl.BlockSpec((bm, bn), lambda i, j: (i, jnp.minimum(j, nper - 1))),
        pl.BlockSpec((bm, bn), act_map),
    ]
    out_shape = [jax.ShapeDtypeStruct((m, width), F32), jax.ShapeDtypeStruct((m, n - width), BF16)]
    if cast_w:
        out_specs.append(pl.BlockSpec((d, bn), lambda i, j: (0, j)))
        out_shape.append(jax.ShapeDtypeStruct(w.shape, BF16))
    for arr, block, index_map in side:
        in_specs.append(pl.BlockSpec(block, index_map))
        out_specs.append(pl.BlockSpec(block, index_map))
        out_shape.append(jax.ShapeDtypeStruct(arr.shape, BF16))
    return pl.pallas_call(
        functools.partial(_inproj_kernel, nper=nper, cast_w=cast_w, n_side=len(side), has_tail=tail is not None),
        name="inproj",
        grid=(m // bm, n // bn),
        in_specs=in_specs,
        out_specs=out_specs,
        out_shape=out_shape,
        scratch_shapes=[pltpu.VMEM((bm, d), BF16)],
        compiler_params=_cparams(2),
    )(x, g, w, cos, sin, *([] if tail is None else [tail]), *[arr for arr, _, _ in side])


def _lru_coeffs(xc, wbd_ref, bg_ref, cl_row, group):
    gates = jnp.dot(xc.astype(BF16), wbd_ref[group], preferred_element_type=F32) + bg_ref[group]
    rec = _sigmoid(gates[:, :LANES])
    ing = _sigmoid(gates[:, LANES:])
    log_a = rec * cl_row
    a = jnp.exp(log_a)
    one_minus_a2 = jnp.tanh(-log_a) * (a * a + 1.0)
    mult = jnp.where(one_minus_a2 > 0.0, one_minus_a2 * lax.rsqrt(one_minus_a2), 0.0)
    return a, mult, ing


def _lru_seq_kernel(xa_ref, ga_ref, conv0_ref, h0_ref, cw_ref, cb_ref, wbd_ref, bg_ref, lam_ref,
                    *refs, tr, emit_y, first_pos_zero):
    if emit_y:
        ya_ref, convout_ref, hout_ref, halo_ref, h_ref, a_ref, b_ref = refs
    else:
        convout_ref, hout_ref, halo_ref, h_ref, a_ref, b_ref = refs
    t = pl.program_id(1)
    halo = 8

    @pl.when(t == 0)
    def _():
        halo_ref[0:halo - 3, :] = jnp.zeros((halo - 3, halo_ref.shape[1]), F32)
        halo_ref[halo - 3:halo, :] = conv0_ref[0]
        h_ref[...] = h0_ref[0]

    xa = xa_ref[...]
    cw = [cw_ref[CONV_W - 1 - k:CONV_W - k, :] for k in range(CONV_W)]
    xc = cb_ref[...] + cw[0] * xa + sum(cw[k] * pltpu.roll(xa, k, axis=0) for k in range(1, CONV_W))
    window = jnp.concatenate([halo_ref[0:halo, :], xa[0:halo, :]], axis=0)
    first = cb_ref[...] + sum(cw[k] * window[halo - k:2 * halo - k, :] for k in range(CONV_W))
    xc = jnp.concatenate([first, xc[halo:, :]], axis=0)
    halo_ref[0:halo, :] = xa[tr - halo:, :]

    cl = LRU_C * jax.nn.log_sigmoid(lam_ref[...])
    for g in range(xc.shape[1] // LANES):
        sl = slice(g * LANES, (g + 1) * LANES)
        a, mult, ing = _lru_coeffs(xc[:, sl], wbd_ref, bg_ref, cl[:, sl], g)
        if first_pos_zero:
            row = lax.broadcasted_iota(jnp.int32, a.shape, 0) + t * tr
            mult = jnp.where(row == 0, 1.0, mult)
        a_ref[:, sl] = a
        b_ref[:, sl] = mult * ing * xc[:, sl]

    h = h_ref[...]
    for r in range(tr):
        h = a_ref[r:r + 1, :] * h + b_ref[r:r + 1, :]
        halo_ref[halo + r:halo + r + 1, :] = h
    h_ref[...] = h
    if emit_y:
        ya_ref[...] = (halo_ref[halo:halo + tr, :] * ga_ref[...].astype(F32)).astype(BF16)

    @pl.when(t == pl.num_programs(1) - 1)
    def _():
        convout_ref[0] = halo_ref[halo - 3:halo, :]
        hout_ref[0] = h_ref[...]


def _lru_seq(xa, act, row0, nseq, seqlen, tr, conv0, h0, lw, emit_y, first_pos_zero):
    w = xa.shape[1]
    nt = seqlen // tr
    rb0 = row0 // tr
    ng = w // LANES
    state_map = (lambda b, t: (b, 0, 0)) if conv0.shape[0] == nseq else (lambda b, t: (0, 0, 0))
    const2 = lambda b, t: (0, 0)
    const3 = lambda b, t: (0, 0, 0)
    out_specs = [pl.BlockSpec((1, CONV_W - 1, w), lambda b, t: (b, 0, 0)),
                 pl.BlockSpec((1, 1, w), lambda b, t: (b, 0, 0))]
    out_shape = [jax.ShapeDtypeStruct((nseq, CONV_W - 1, w), F32), jax.ShapeDtypeStruct((nseq, 1, w), F32)]
    if emit_y:
        out_specs = [pl.BlockSpec((tr, w), lambda b, t: (b * nt + t, 0))] + out_specs
        out_shape = [jax.ShapeDtypeStruct((nseq * seqlen, w), BF16)] + out_shape
    return dict(
        name="lru_seq",
        kernel=functools.partial(_lru_seq_kernel, tr=tr, emit_y=emit_y, first_pos_zero=first_pos_zero),
        grid=(nseq, nt),
        in_specs=[
            pl.BlockSpec((tr, w), lambda b, t: (rb0 + b * nt + t, 0)),
            pl.BlockSpec((tr, w), lambda b, t: (rb0 + b * nt + t, _ACT_GELU)),
            pl.BlockSpec((1, CONV_W - 1, w), state_map),
            pl.BlockSpec((1, 1, w), state_map),
            pl.BlockSpec((CONV_W, w), const2),
            pl.BlockSpec((1, w), const2),
            pl.BlockSpec((ng, LANES, 2 * LANES), const3),
            pl.BlockSpec((ng, 1, 2 * LANES), const3),
            pl.BlockSpec((1, w), const2),
        ],
        out_specs=out_specs,
        out_shape=out_shape,
        scratch_shapes=[pltpu.VMEM((8 + tr, w), F32), pltpu.VMEM((1, w), F32),
                        pltpu.VMEM((tr, w), F32), pltpu.VMEM((tr, w), F32)],
        args=(xa, act, conv0, h0, lw["conv_w"], lw["conv_b"], lw["wbd"], lw["bg"], lw["lam"]))


def _lru_step_kernel(xa_ref, ga_ref, conv0_ref, h0_ref, cw_ref, cb_ref, wbd_ref, bg_ref, lam_ref,
                     ya_ref, convout_ref, hout_ref, hs_ref, *, nseq, seqlen):
    cl = LRU_C * jax.nn.log_sigmoid(lam_ref[...])
    full = [conv0_ref[j] for j in range(CONV_W - 1)]
    full += [xa_ref[pl.ds(t, nseq, stride=seqlen), :] for t in range(seqlen)]
    h = h0_ref[...]
    for t in range(seqlen):
        xc = cb_ref[...] + cw_ref[0:1, :] * full[t]
        for j in range(1, CONV_W):
            xc = xc + cw_ref[j:j + 1, :] * full[t + j]
        a, mult, ing = _lru_coeffs(xc, wbd_ref, bg_ref, cl, 0)
        h = a * h + mult * ing * xc
        hs_ref[pl.ds(t, nseq, stride=seqlen), :] = h
    ya_ref[...] = (hs_ref[...] * ga_ref[...].astype(F32)).astype(BF16)
    for j in range(CONV_W - 1):
        convout_ref[j] = full[seqlen + j]
    hout_ref[...] = h


def _lru_step(xa, act, nseq, seqlen, conv0_t, h0, lw):
    w = xa.shape[1]
    rows = nseq * seqlen
    ng = w // LANES
    col = lambda g: (0, g)
    return pl.pallas_call(
        functools.partial(_lru_step_kernel, nseq=nseq, seqlen=seqlen),
        name="lru_step",
        grid=(ng,),
        in_specs=[
            pl.BlockSpec((rows, LANES), col),
            pl.BlockSpec((rows, LANES), col),
            pl.BlockSpec((CONV_W - 1, nseq, LANES), lambda g: (0, 0, g)),
            pl.BlockSpec((nseq, LANES), col),
            pl.BlockSpec((CONV_W, LANES), col),
            pl.BlockSpec((1, LANES), col),
            pl.BlockSpec((1, LANES, 2 * LANES), lambda g: (g, 0, 0)),
            pl.BlockSpec((1, 1, 2 * LANES), lambda g: (g, 0, 0)),
            pl.BlockSpec((1, LANES), col),
        ],
        out_specs=[
            pl.BlockSpec((rows, LANES), col),
            pl.BlockSpec((CONV_W - 1, nseq, LANES), lambda g: (0, 0, g)),
            pl.BlockSpec((nseq, LANES), col),
        ],
        out_shape=[jax.ShapeDtypeStruct((rows, w), BF16),
                   jax.ShapeDtypeStruct((CONV_W - 1, nseq, w), F32),
                   jax.ShapeDtypeStruct((nseq, w), F32)],
        scratch_shapes=[pltpu.VMEM((rows, LANES), F32)],
        compiler_params=_cparams(1),
    )(xa, act, conv0_t, h0, lw["conv_w"], lw["conv_b"], lw["wbd"], lw["bg"], lw["lam"])


def _log_gamma(h):
    return math.log1p(-(2.0 ** (-5.0 - h)))


def _ret_kernel(q_ref, k_ref, v_ref, sg_ref, s0_ref, ng_ref, *refs, nb, clen, cps, emit_y, ids):
    if emit_y:
        y_ref, sout_ref, s_ref, dec_ref, rdec_ref = refs
    else:
        sout_ref, s_ref, dec_ref, rdec_ref = refs
    group, c, last = ids() if ids is not None else (pl.program_id(0), pl.program_id(1), pl.num_programs(1) - 1)
    rows = nb * clen
    heads = range(RET_HEADS)
    chunks = range(cps)

    @pl.when((group == 0) & (c == 0))
    def _():
        li = lax.broadcasted_iota(jnp.int32, (rows, rows), 0)
        mi = lax.broadcasted_iota(jnp.int32, (rows, rows), 1)
        keep = (li // clen == mi // clen) & (li >= mi)
        diff = jnp.where(keep, li - mi, 0).astype(F32)
        tpos = (lax.broadcasted_iota(jnp.int32, (rows, LANES), 0) % clen).astype(F32)
        for h in heads:
            lg = _log_gamma(h)
            dec_ref[h] = jnp.where(keep, jnp.exp(lg * diff), 0.0)
            rdec_ref[0, h] = jnp.exp(lg * (tpos + 1.0))
            rdec_ref[1, h] = jnp.exp(lg * (clen - 1.0 - tpos))

    @pl.when(c == 0)
    def _():
        for n in range(nb):
            s_ref[n] = s0_ref[n if s0_ref.shape[0] == nb else 0]

    if nb > 1:
        seq_of_row = lax.broadcasted_iota(jnp.int32, (rows, nb * HEAD_DIM), 0) // clen
        seq_of_col = lax.broadcasted_iota(jnp.int32, (rows, nb * HEAD_DIM), 1) // HEAD_DIM
        own = seq_of_row == seq_of_col
    tn = (((0,), (0,)), ((), ()))
    nt = (((1,), (1,)), ((), ()))
    hsl = [slice(h * HEAD_DIM, (h + 1) * HEAD_DIM) for h in heads]
    rsl = [slice(ci * rows, (ci + 1) * rows) for ci in chunks]
    q = [[q_ref[rsl[ci], sl] for sl in hsl] for ci in chunks]
    k = [[k_ref[rsl[ci], sl] for sl in hsl] for ci in chunks]
    v = [[v_ref[rsl[ci], sl] for sl in hsl] for ci in chunks]

    if emit_y:
        scores = [[lax.dot_general(q[ci][h], k[ci][h], nt, preferred_element_type=F32) for h in heads]
                  for ci in chunks]
    upd = []
    for ci in chunks:
        upd.append([])
        for h in heads:
            kd = (k[ci][h].astype(F32) * rdec_ref[1, h]).astype(BF16)
            if nb > 1:
                v_bd = jnp.where(own, jnp.concatenate([v[ci][h].astype(F32)] * nb, axis=1), 0.0).astype(BF16)
            else:
                v_bd = v[ci][h]
            upd[ci].append(lax.dot_general(kd, v_bd, tn, preferred_element_type=F32))

    state = [[[s_ref[n, h] for n in range(nb)] for h in heads]]
    for ci in chunks:
        state.append([[math.exp(clen * _log_gamma(h)) * state[ci][h][n]
                       + upd[ci][h][:, n * HEAD_DIM:(n + 1) * HEAD_DIM] for n in range(nb)] for h in heads])
    for h in heads:
        for n in range(nb):
            s_ref[n, h] = state[cps][h][n]

    if emit_y:
        inter = []
        for ci in chunks:
            inter.append([])
            for h in heads:
                s_cat = jnp.concatenate(state[ci][h], axis=1) if nb > 1 else state[ci][h][0]
                qs = jnp.dot(q[ci][h], s_cat.astype(BF16), preferred_element_type=F32)
                if nb > 1:
                    qs = jnp.where(own, qs, 0.0)
                    qs = sum(qs[:, n * HEAD_DIM:(n + 1) * HEAD_DIM] for n in range(nb))
                inter[ci].append(qs)
        for ci in chunks:
            for h in heads:
                p = (scores[ci][h] * dec_ref[h]).astype(BF16)
                o = jnp.dot(p, v[ci][h], preferred_element_type=F32) + inter[ci][h] * rdec_ref[0, h]
                mu = jnp.mean(o, axis=-1, keepdims=True)
                dev = o - mu
                var = jnp.mean(dev * dev, axis=-1, keepdims=True)
                normed = dev * lax.rsqrt(var + EPS) * ng_ref[:, hsl[h]]
                y_ref[rsl[ci], hsl[h]] = (sg_ref[rsl[ci], hsl[h]].astype(F32) * normed).astype(BF16)

    @pl.when(c == last)
    def _():
        sout_ref[...] = s_ref[...]


def _retention(act, row0, nseq, seqlen, nb, clen, cps, s0, norm_g, emit_y, ids=None):
    w = RET_HEADS * HEAD_DIM
    rows = nb * clen * cps
    nc = seqlen // (clen * cps)
    assert nb == 1 or (nc == 1 and cps == 1)
    rb0 = row0 // rows
    rmap = lambda col: (lambda b, c: (rb0 + b * nc + c, col))
    shared = s0.shape[0] != nseq
    s_map = (lambda b, c: (0, 0, 0, 0)) if shared else (lambda b, c: (b, 0, 0, 0))
    s_block = (1 if shared else nb, RET_HEADS, HEAD_DIM, HEAD_DIM)
    out_specs = [pl.BlockSpec((nb, RET_HEADS, HEAD_DIM, HEAD_DIM), lambda b, c: (b, 0, 0, 0))]
    out_shape = [jax.ShapeDtypeStruct((nseq, RET_HEADS, HEAD_DIM, HEAD_DIM), F32)]
    if emit_y:
        out_specs = [pl.BlockSpec((rows, w), lambda b, c: (b * nc + c, 0))] + out_specs
        out_shape = [jax.ShapeDtypeStruct((nseq * seqlen, w), BF16)] + out_shape
    return dict(
        name="retention",
        kernel=functools.partial(_ret_kernel, nb=nb, clen=clen, cps=cps, emit_y=emit_y, ids=ids),
        grid=(nseq // nb, nc),
        in_specs=[
            pl.BlockSpec((rows, w), rmap(_ACT_Q)),
            pl.BlockSpec((rows, w), rmap(_ACT_K)),
            pl.BlockSpec((rows, w), rmap(_ACT_V)),
            pl.BlockSpec((rows, w), rmap(_ACT_SILU)),
            pl.BlockSpec(s_block, s_map),
            pl.BlockSpec((1, w), lambda b, c: (0, 0)),
        ],
        out_specs=out_specs,
        out_shape=out_shape,
        scratch_shapes=[pltpu.VMEM((nb, RET_HEADS, HEAD_DIM, HEAD_DIM), F32),
                        pltpu.VMEM((RET_HEADS, nb * clen, nb * clen), F32),
                        pltpu.VMEM((2, RET_HEADS, nb * clen, LANES), F32)],
        args=(act, act, act, act, s0, norm_g))


def _outproj_kernel(x_ref, ya_ref, yb_ref, sga_ref, sgb_ref, pa_ref, pb_ref, wo_ref, o_ref):
    for r0, nr in _row_chunks(o_ref.shape[0], 256):
        rows = slice(r0, r0 + nr)
        ma = jnp.dot(ya_ref[rows, :], pa_ref[...], preferred_element_type=F32)
        mb = jnp.dot(yb_ref[rows, :], pb_ref[...], preferred_element_type=F32)
        merged = sga_ref[rows, :].astype(F32) * ma + sgb_ref[rows, :].astype(F32) * mb
        o_ref[rows, :] = x_ref[rows, :] + jnp.dot(merged.astype(BF16), wo_ref[...], preferred_element_type=F32)


def _outproj(x, ya, yb, act, pa, pb, wo, bm):
    m, d = x.shape
    w = ya.shape[1]
    nd = d // 1024
    row = lambda i: (i, 0)
    const = lambda i: (0, 0)
    single = pl.Buffered(1)
    return pl.pallas_call(
        _outproj_kernel,
        name="outproj",
        grid=(m // bm,),
        in_specs=[
            pl.BlockSpec((bm, d), row),
            pl.BlockSpec((bm, w), row),
            pl.BlockSpec((bm, w), row),
            pl.BlockSpec((bm, d), lambda i: (i, _ACT_SGA // nd)),
            pl.BlockSpec((bm, d), lambda i: (i, _ACT_SGB // nd)),
            pl.BlockSpec((w, d), const, pipeline_mode=single),
            pl.BlockSpec((w, d), const, pipeline_mode=single),
            pl.BlockSpec((d, d), const, pipeline_mode=single),
        ],
        out_specs=pl.BlockSpec((bm, d), row),
        out_shape=jax.ShapeDtypeStruct((m, d), F32),
        compiler_params=_cparams(1),
    )(x, ya, yb, act, act, pa, pb, wo)


def _ffn_kernel(x_ref, g_ref, wu_ref, wd_ref, gf_ref, o_ref, hn_ref):
    j = pl.program_id(1)
    last = pl.num_programs(1) - 1

    def block(first, final):
        for r0, nr in _row_chunks(o_ref.shape[0], 512):
            rows = slice(r0, r0 + nr)
            if first:
                base = x_ref[rows, :]
                hn = _rmsnorm(base, g_ref[...]).astype(BF16)
                hn_ref[rows, :] = hn
            else:
                base = o_ref[rows, :]
                hn = hn_ref[rows, :]
            u = jnp.dot(hn, wu_ref[...], preferred_element_type=F32)
            r = jnp.square(jnp.maximum(u, 0.0)).astype(BF16)
            acc = base + jnp.dot(r, wd_ref[...], preferred_element_type=F32)
            o_ref[rows, :] = _rmsnorm(acc, gf_ref[...]) if final else acc

    pl.when(j == 0)(lambda: block(True, False))
    pl.when((j > 0) & (j < last))(lambda: block(False, False))
    pl.when(j == last)(lambda: block(False, True))


def _ffn(x, g, wu, wd, gf, bm, bf):
    m, d = x.shape
    dff = wu.shape[1]
    assert dff // bf >= 2
    return pl.pallas_call(
        _ffn_kernel,
        name="ffn",
        grid=(m // bm, dff // bf),
        in_specs=[
            pl.BlockSpec((bm, d), lambda i, j: (i, 0)),
            pl.BlockSpec((1, d), lambda i, j: (0, 0)),
            pl.BlockSpec((d, bf), lambda i, j: (0, j)),
            pl.BlockSpec((bf, d), lambda i, j: (j, 0)),
            pl.BlockSpec((1, d), lambda i, j: (0, 0)),
        ],
        out_specs=pl.BlockSpec((bm, d), lambda i, j: (i, 0)),
        out_shape=jax.ShapeDtypeStruct((m, d), F32),
        scratch_shapes=[pltpu.VMEM((bm, d), BF16)],
        compiler_params=_cparams(2),
    )(x, g, wu, wd, gf)


def _rope_tables(pos):
    inv = ROPE_BASE ** (-np.arange(0, HEAD_DIM, 2, dtype=np.float64) / HEAD_DIM)
    ang = np.asarray(pos, np.float64)[:, None] * inv[None, :]
    cos, sin = np.cos(ang), np.sin(ang)
    return (jnp.asarray(np.concatenate([cos, cos], axis=1), F32),
            jnp.asarray(np.concatenate([-sin, sin], axis=1), F32))


def _gate_weights(wa, wx, ba, bx):
    nblk, blk, _ = wa.shape
    per = LANES // blk
    ng = nblk // per

    def bd(wt):
        wt = wt.reshape(ng, per, blk, blk)
        eye = jnp.eye(per, dtype=wt.dtype)
        return jnp.einsum("gpcd,pq->gpcqd", wt, eye).reshape(ng, LANES, LANES)

    wbd = jnp.concatenate([bd(wa), bd(wx)], axis=2).astype(BF16)
    bg = jnp.concatenate([ba.reshape(ng, 1, LANES), bx.reshape(ng, 1, LANES)], axis=2)
    return wbd, bg


def kernel(x_prompt, x_sample, state_conv, state_lru, state_ret, meta_tokens, norm_mix_g, w_in, conv_w,
           conv_b, lru_wa, lru_ba, lru_wx, lru_bx, lru_lam, ret_norm_g, p_a, p_b, w_out, norm_ffn_g,
           w_up, w_down, norm_f_g):
    assert w_in.shape[0] == 1
    nb_p, t_p, d = x_prompt.shape
    nb_s, t_s, _ = x_sample.shape
    w = conv_w.shape[-1]
    rows_p, rows_s = nb_p * t_p, nb_s * t_s

    g_mix, g_ffn, g_f = norm_mix_g[0][None], norm_ffn_g[0][None], norm_f_g[None]
    wbd, bg = _gate_weights(lru_wa[0], lru_wx[0], lru_ba[0], lru_bx[0])
    lw = dict(conv_w=conv_w[0], conv_b=conv_b[0][None], wbd=wbd, bg=bg, lam=lru_lam[0][None])
    ret_g = ret_norm_g[0][None]

    cos_p, sin_p = _rope_tables(N_META + np.arange(t_p))
    cos_s, sin_s = _rope_tables(np.concatenate([np.tile(PAST_LEN + np.arange(t_s), nb_s), np.arange(N_META)]))
    x_p2 = x_prompt.reshape(rows_p, d)
    x_s2 = x_sample.reshape(rows_s, d)
    xa_s, act_s, w_in_b = _inproj(x_s2, g_mix, w_in[0], cos_s, sin_s, rows_s + N_META, 512, 1,
                                  tail=meta_tokens.astype(x_sample.dtype))
    bm_p = 1024
    n_i, n_j = rows_p // bm_p, 8
    side_map = lambda i, j: (i, jnp.minimum(j, n_j - 1))
    later = (w_up, w_down, p_a, p_b, w_out)
    sides = tuple((wt[0], (wt.shape[1] // n_i, wt.shape[2] // n_j), side_map) for wt in later)
    xa_p, act_p, wu_b, wd_b, pa_b, pb_b, wo_b = _inproj(x_p2, g_mix, w_in_b, cos_p, sin_p, bm_p, 1024, t_p // bm_p,
                                                        side=sides)


    zc = jnp.zeros((1, CONV_W - 1, w), F32)
    zh = jnp.zeros((1, 1, w), F32)
    zs = jnp.zeros((1, RET_HEADS, HEAD_DIM, HEAD_DIM), F32)
    conv_m, h_m = _call(_lru_seq(xa_s, act_s, rows_s, 1, N_META, N_META, zc, zh, lw, emit_y=False,
                                 first_pos_zero=True))
    (s_m,) = _call(_retention(act_s, rows_s, 1, N_META, 1, N_META, 1, zs, ret_g, emit_y=False))

    lru_tr = 512
    lru_p = _lru_seq(xa_p, act_p, 0, nb_p, t_p, lru_tr, conv_m, h_m, lw, emit_y=True, first_pos_zero=False)
    steps = nb_p * (t_p // lru_tr)
    ret_s = _retention(act_s, 0, nb_s, t_s, nb_s // steps, t_s, 1, state_ret[0], ret_g, emit_y=True,
                       ids=lambda: (pl.program_id(0) * (t_p // lru_tr) + pl.program_id(1), 0, 0))
    (ya_p, conv_p, h_p), (yb_s, s_s) = _call_pair(lru_p, ret_s, "lru_seq_retention")
    yb_p, s_p = _call(_retention(act_p, 0, nb_p, t_p, 1, CHUNK, 4, s_m, ret_g, emit_y=True))
    conv0_t = jnp.transpose(state_conv[0], (1, 0, 2))
    ya_s, conv_s_t, h_s = _lru_step(xa_s, act_s, nb_s, t_s, conv0_t, state_lru[0], lw)

    x1_p = _outproj(x_p2, ya_p, yb_p, act_p, pa_b, pb_b, wo_b, 512)
    y_p = _ffn(x1_p, g_ffn, wu_b, wd_b, g_f, 1024, 1024)
    y_p, ya_s = lax.optimization_barrier((y_p, ya_s))
    x1_s = _outproj(x_s2, ya_s, yb_s, act_s, pa_b, pb_b, wo_b, 512)
    y_s = _ffn(x1_s, g_ffn, wu_b, wd_b, g_f, 1024, 1024)

    return (y_p.reshape(nb_p, t_p, d), y_s.reshape(nb_s, t_s, d),
            conv_p[None], h_p.reshape(1, nb_p, w), s_p[None],
            jnp.transpose(conv_s_t, (1, 0, 2))[None], h_s[None], s_s[None])
```

```python
import functools
import math

import jax
import jax.numpy as jnp
import numpy as np
from jax import lax
from jax.experimental import pallas as pl
from jax.experimental.pallas import tpu as pltpu

F32 = jnp.float32
BF16 = jnp.bfloat16

N_META = 16
PAST_LEN = 16384
CONV_W = 4
LRU_C = 8.0
RET_HEADS = 8
HEAD_DIM = 128
CHUNK = 128
ROPE_BASE = 10000.0
EPS = 1e-6

LANES = 128
VMEM_LIMIT = 56 << 20


def _cparams(n_axes):
    return pltpu.CompilerParams(dimension_semantics=("arbitrary",) * n_axes, vmem_limit_bytes=VMEM_LIMIT)


def _call(spec):
    return pl.pallas_call(
        spec["kernel"], name=spec["name"], grid=spec["grid"], in_specs=spec["in_specs"],
        out_specs=spec["out_specs"], out_shape=spec["out_shape"], scratch_shapes=spec["scratch_shapes"],
        compiler_params=_cparams(len(spec["grid"])))(*spec["args"])


def _call_pair(a, b, name):
    n0, n1 = a["grid"]
    assert b["grid"] == (n0 * n1, 1)

    def remap(spec):
        return pl.BlockSpec(spec.block_shape, lambda i, j, m=spec.index_map: m(i * n1 + j, 0))

    counts = [len(a["in_specs"]), len(b["in_specs"]), len(a["out_specs"]), len(b["out_specs"]),
              len(a["scratch_shapes"]), len(b["scratch_shapes"])]

    def kernel(*refs):
        groups, pos = [], 0
        for n in counts:
            groups.append(refs[pos:pos + n])
            pos += n
        a["kernel"](*groups[0], *groups[2], *groups[4])
        b["kernel"](*groups[1], *groups[3], *groups[5])

    outs = pl.pallas_call(
        kernel, name=name, grid=a["grid"],
        in_specs=list(a["in_specs"]) + [remap(sp) for sp in b["in_specs"]],
        out_specs=list(a["out_specs"]) + [remap(sp) for sp in b["out_specs"]],
        out_shape=list(a["out_shape"]) + list(b["out_shape"]),
        scratch_shapes=list(a["scratch_shapes"]) + list(b["scratch_shapes"]),
        compiler_params=_cparams(2))(*a["args"], *b["args"])
    return outs[:counts[2]], outs[counts[2]:]


def _rmsnorm(x, g):
    return x * lax.rsqrt(jnp.mean(x * x, axis=-1, keepdims=True) + EPS) * g


def _sigmoid(x):
    return 0.5 * jnp.tanh(0.5 * x) + 0.5


def _row_chunks(rows, size):
    n = max(rows // size, 1)
    return [(c * size, size if c < n - 1 else rows - c * size) for c in range(n)]


_COL_GATE, _COL_Q, _COL_K, _COL_V, _COL_G, _COL_GATE_A = 1, 2, 3, 4, 5, 6
_NUM_COL_KINDS = 10
_ACT_GELU, _ACT_Q, _ACT_K, _ACT_V, _ACT_SGA, _ACT_SGB, _ACT_SILU = 0, 1, 2, 3, 4, 6, 8


def _act_block(kind):
    return jnp.where(kind <= _COL_V, kind - 1, jnp.where(kind == _COL_G, _ACT_SILU, kind - 2))


def _inproj_kernel(x_ref, g_ref, w_ref, cos_ref, sin_ref, *refs, nper, cast_w, n_side, has_tail):
    if has_tail:
        tail_ref, refs = refs[0], refs[1:]
    side_src, refs = refs[:n_side], refs[n_side:]
    xa_ref, act_ref = refs[0], refs[1]
    refs = refs[2:]
    if cast_w:
        wb_ref, refs = refs[0], refs[1:]
    side_dst, (xn_ref,) = refs[:n_side], refs[n_side:]
    j = pl.program_id(1)
    kind = j // nper
    bn = xa_ref.shape[1]

    for src, dst in zip(side_src, side_dst):
        dst[...] = src[...].astype(BF16)
    if cast_w:
        wb_ref[...] = w_ref[...].astype(BF16)
        w_bf = wb_ref
    else:
        w_bf = w_ref

    def project(epilogue, normalize=False):
        if normalize:
            nx = x_ref.shape[0]
            xn_ref[0:nx, :] = _rmsnorm(x_ref[...], g_ref[...]).astype(BF16)
            if has_tail:
                xn_ref[nx:, :] = _rmsnorm(tail_ref[...], g_ref[...]).astype(BF16)
        epilogue(jnp.dot(xn_ref[...], w_bf[...], preferred_element_type=F32))

    def to_xa(acc):
        xa_ref[...] = acc

    def rope(scale):
        def epilogue(acc):
            bm = acc.shape[0]
            t0 = pl.multiple_of((pl.program_id(0) % (cos_ref.shape[0] // bm)) * bm, bm)
            cos, sin = cos_ref[pl.ds(t0, bm), :], sin_ref[pl.ds(t0, bm), :]
            for h in range(bn // HEAD_DIM):
                a = acc[:, h * HEAD_DIM:(h + 1) * HEAD_DIM]
                r = a * cos + pltpu.roll(a, HEAD_DIM // 2, axis=1) * sin
                if scale is not None:
                    r = r * scale
                act_ref[:, h * HEAD_DIM:(h + 1) * HEAD_DIM] = r.astype(BF16)
        return epilogue

    def elementwise(fn):
        def epilogue(acc):
            act_ref[...] = fn(acc).astype(BF16)
        return epilogue

    pl.when(j == 0)(lambda: project(to_xa, normalize=True))
    if nper > 1:
        pl.when((j > 0) & (kind == 0))(lambda: project(to_xa))
    pl.when(kind == _COL_GATE)(lambda: project(elementwise(jax.nn.gelu)))
    pl.when(kind == _COL_Q)(lambda: project(rope(None)))
    pl.when(kind == _COL_K)(lambda: project(rope(HEAD_DIM ** -0.5)))
    pl.when(kind == _COL_V)(lambda: project(elementwise(lambda a: a)))
    pl.when(kind == _COL_G)(lambda: project(elementwise(lambda a: a * _sigmoid(a))))
    pl.when(kind >= _COL_GATE_A)(lambda: project(elementwise(_sigmoid)))


def _inproj(x, g, w, cos, sin, bm, bn, table_blocks, side=(), tail=None):
    xrows, d = x.shape
    m = xrows if tail is None else xrows + tail.shape[0]
    assert tail is None or m == bm
    n = w.shape[1]
    width = n // _NUM_COL_KINDS
    nper = width // bn
    cast_w = w.dtype != BF16

    def act_map(i, j):
        kind = j // nper
        return i, jnp.where(kind == 0, 0, _act_block(kind) * nper + j % nper)

    in_specs = [
        pl.BlockSpec((min(bm, xrows), d), lambda i, j: (i, 0)),
        pl.BlockSpec((1, d), lambda i, j: (0, 0)),
        pl.BlockSpec((d, bn), lambda i, j: (0, j)),
        pl.BlockSpec((table_blocks * bm, LANES), lambda i, j: (0, 0)),
        pl.BlockSpec((table_blocks * bm, LANES), lambda i, j: (0, 0)),
    ]
    if tail is not None:
        in_specs.append(pl.BlockSpec(tail.shape, lambda i, j: (0, 0)))
    out_specs = [
        pl.BlockSpec((bm, bn), lambda i, j: (i, jnp.minimum(j, nper - 1))),
        pl.BlockSpec((bm, bn), act_map),
    ]
    out_shape = [jax.ShapeDtypeStruct((m, width), F32), jax.ShapeDtypeStruct((m, n - width), BF16)]
    if cast_w:
        out_specs.append(pl.BlockSpec((d, bn), lambda i, j: (0, j)))
        out_shape.append(jax.ShapeDtypeStruct(w.shape, BF16))
    for arr, block, index_map in side:
        in_specs.append(pl.BlockSpec(block, index_map))
        out_specs.append(pl.BlockSpec(block, index_map))
        out_shape.append(jax.ShapeDtypeStruct(arr.shape, BF16))
    return pl.pallas_call(
        functools.partial(_inproj_kernel, nper=nper, cast_w=cast_w, n_side=len(side), has_tail=tail is not None),
        name="inproj",
        grid=(m // bm, n // bn),
        in_specs=in_specs,
        out_specs=out_specs,
        out_shape=out_shape,
        scratch_shapes=[pltpu.VMEM((bm, d), BF16)],
        compiler_params=_cparams(2),
    )(x, g, w, cos, sin, *([] if tail is None else [tail]), *[arr for arr, _, _ in side])


def _lru_coeffs(xc, wbd_ref, bg_ref, cl_row, group):
    gates = jnp.dot(xc.astype(BF16), wbd_ref[group], preferred_element_type=F32) + bg_ref[group]
    rec = _sigmoid(gates[:, :LANES])
    ing = _sigmoid(gates[:, LANES:])
    log_a = rec * cl_row
    a = jnp.exp(log_a)
    one_minus_a2 = jnp.tanh(-log_a) * (a * a + 1.0)
    mult = jnp.where(one_minus_a2 > 0.0, one_minus_a2 * lax.rsqrt(one_minus_a2), 0.0)
    return a, mult, ing


def _lru_seq_kernel(xa_ref, ga_ref, conv0_ref, h0_ref, cw_ref, cb_ref, wbd_ref, bg_ref, lam_ref,
                    *refs, tr, emit_y, first_pos_zero):
    if emit_y:
        ya_ref, convout_ref, hout_ref, halo_ref, h_ref, a_ref, b_ref = refs
    else:
        convout_ref, hout_ref, halo_ref, h_ref, a_ref, b_ref = refs
    t = pl.program_id(1)
    halo = 8

    @pl.when(t == 0)
    def _():
        halo_ref[0:halo - 3, :] = jnp.zeros((halo - 3, halo_ref.shape[1]), F32)
        halo_ref[halo - 3:halo, :] = conv0_ref[0]
        h_ref[...] = h0_ref[0]

    xa = xa_ref[...]
    cw = [cw_ref[CONV_W - 1 - k:CONV_W - k, :] for k in range(CONV_W)]
    xc = cb_ref[...] + cw[0] * xa + sum(cw[k] * pltpu.roll(xa, k, axis=0) for k in range(1, CONV_W))
    window = jnp.concatenate([halo_ref[0:halo, :], xa[0:halo, :]], axis=0)
    first = cb_ref[...] + sum(cw[k] * window[halo - k:2 * halo - k, :] for k in range(CONV_W))
    xc = jnp.concatenate([first, xc[halo:, :]], axis=0)
    halo_ref[0:halo, :] = xa[tr - halo:, :]

    cl = LRU_C * jax.nn.log_sigmoid(lam_ref[...])
    for g in range(xc.shape[1] // LANES):
        sl = slice(g * LANES, (g + 1) * LANES)
        a, mult, ing = _lru_coeffs(xc[:, sl], wbd_ref, bg_ref, cl[:, sl], g)
        if first_pos_zero:
            row = lax.broadcasted_iota(jnp.int32, a.shape, 0) + t * tr
            mult = jnp.where(row == 0, 1.0, mult)
        a_ref[:, sl] = a
        b_ref[:, sl] = mult * ing * xc[:, sl]

    h = h_ref[...]
    for r in range(tr):
        h = a_ref[r:r + 1, :] * h + b_ref[r:r + 1, :]
        halo_ref[halo + r:halo + r + 1, :] = h
    h_ref[...] = h
    if emit_y:
        ya_ref[...] = (halo_ref[halo:halo + tr, :] * ga_ref[...].astype(F32)).astype(BF16)

    @pl.when(t == pl.num_programs(1) - 1)
    def _():
        convout_ref[0] = halo_ref[halo - 3:halo, :]
        hout_ref[0] = h_ref[...]


def _lru_seq(xa, act, row0, nseq, seqlen, tr, conv0, h0, lw, emit_y, first_pos_zero):
    w = xa.shape[1]
    nt = seqlen // tr
    rb0 = row0 // tr
    ng = w // LANES
    state_map = (lambda b, t: (b, 0, 0)) if conv0.shape[0] == nseq else (lambda b, t: (0, 0, 0))
    const2 = lambda b, t: (0, 0)
    const3 = lambda b, t: (0, 0, 0)
    out_specs = [pl.BlockSpec((1, CONV_W - 1, w), lambda b, t: (b, 0, 0)),
                 pl.BlockSpec((1, 1, w), lambda b, t: (b, 0, 0))]
    out_shape = [jax.ShapeDtypeStruct((nseq, CONV_W - 1, w), F32), jax.ShapeDtypeStruct((nseq, 1, w), F32)]
    if emit_y:
        out_specs = [pl.BlockSpec((tr, w), lambda b, t: (b * nt + t, 0))] + out_specs
        out_shape = [jax.ShapeDtypeStruct((nseq * seqlen, w), BF16)] + out_shape
    return dict(
        name="lru_seq",
        kernel=functools.partial(_lru_seq_kernel, tr=tr, emit_y=emit_y, first_pos_zero=first_pos_zero),
        grid=(nseq, nt),
        in_specs=[
            pl.BlockSpec((tr, w), lambda b, t: (rb0 + b * nt + t, 0)),
            pl.BlockSpec((tr, w), lambda b, t: (rb0 + b * nt + t, _ACT_GELU)),
            pl.BlockSpec((1, CONV_W - 1, w), state_map),
            pl.BlockSpec((1, 1, w), state_map),
            pl.BlockSpec((CONV_W, w), const2),
            pl.BlockSpec((1, w), const2),
            pl.BlockSpec((ng, LANES, 2 * LANES), const3),
            pl.BlockSpec((ng, 1, 2 * LANES), const3),
            pl.BlockSpec((1, w), const2),
        ],
        out_specs=out_specs,
        out_shape=out_shape,
        scratch_shapes=[pltpu.VMEM((8 + tr, w), F32), pltpu.VMEM((1, w), F32),
                        pltpu.VMEM((tr, w), F32), pltpu.VMEM((tr, w), F32)],
        args=(xa, act, conv0, h0, lw["conv_w"], lw["conv_b"], lw["wbd"], lw["bg"], lw["lam"]))


def _lru_step_kernel(xa_ref, ga_ref, conv0_ref, h0_ref, cw_ref, cb_ref, wbd_ref, bg_ref, lam_ref,
                     ya_ref, convout_ref, hout_ref, hs_ref, *, nseq, seqlen):
    cl = LRU_C * jax.nn.log_sigmoid(lam_ref[...])
    full = [conv0_ref[j] for j in range(CONV_W - 1)]
    full += [xa_ref[pl.ds(t, nseq, stride=seqlen), :] for t in range(seqlen)]
    h = h0_ref[...]
    for t in range(seqlen):
        xc = cb_ref[...] + cw_ref[0:1, :] * full[t]
        for j in range(1, CONV_W):
            xc = xc + cw_ref[j:j + 1, :] * full[t + j]
        a, mult, ing = _lru_coeffs(xc, wbd_ref, bg_ref, cl, 0)
        h = a * h + mult * ing * xc
        hs_ref[pl.ds(t, nseq, stride=seqlen), :] = h
    ya_ref[...] = (hs_ref[...] * ga_ref[...].astype(F32)).astype(BF16)
    for j in range(CONV_W - 1):
        convout_ref[j] = full[seqlen + j]
    hout_ref[...] = h


def _lru_step(xa, act, nseq, seqlen, conv0_t, h0, lw):
    w = xa.shape[1]
    rows = nseq * seqlen
    ng = w // LANES
    col = lambda g: (0, g)
    return pl.pallas_call(
        functools.partial(_lru_step_kernel, nseq=nseq, seqlen=seqlen),
        name="lru_step",
        grid=(ng,),
        in_specs=[
            pl.BlockSpec((rows, LANES), col),
            pl.BlockSpec((rows, LANES), col),
            pl.BlockSpec((CONV_W - 1, nseq, LANES), lambda g: (0, 0, g)),
            pl.BlockSpec((nseq, LANES), col),
            pl.BlockSpec((CONV_W, LANES), col),
            pl.BlockSpec((1, LANES), col),
            pl.BlockSpec((1, LANES, 2 * LANES), lambda g: (g, 0, 0)),
            pl.BlockSpec((1, 1, 2 * LANES), lambda g: (g, 0, 0)),
            pl.BlockSpec((1, LANES), col),
        ],
        out_specs=[
            pl.BlockSpec((rows, LANES), col),
            pl.BlockSpec((CONV_W - 1, nseq, LANES), lambda g: (0, 0, g)),
            pl.BlockSpec((nseq, LANES), col),
        ],
        out_shape=[jax.ShapeDtypeStruct((rows, w), BF16),
                   jax.ShapeDtypeStruct((CONV_W - 1, nseq, w), F32),
                   jax.ShapeDtypeStruct((nseq, w), F32)],
        scratch_shapes=[pltpu.VMEM((rows, LANES), F32)],
        compiler_params=_cparams(1),
    )(xa, act, conv0_t, h0, lw["conv_w"], lw["conv_b"], lw["wbd"], lw["bg"], lw["lam"])


def _log_gamma(h):
    return math.log1p(-(2.0 ** (-5.0 - h)))


def _ret_kernel(q_ref, k_ref, v_ref, sg_ref, s0_ref, ng_ref, *refs, nb, clen, cps, emit_y, ids, n_side):
    side_src, refs = refs[:n_side], refs[n_side:]
    if emit_y:
        y_ref, refs = refs[0], refs[1:]
    sout_ref, refs = refs[0], refs[1:]
    side_dst, (s_ref, dec_ref, rdec_ref) = refs[:n_side], refs[n_side:]
    for src, dst in zip(side_src, side_dst):
        dst[...] = src[...].astype(BF16)
    group, c, last = ids() if ids is not None else (pl.program_id(0), pl.program_id(1), pl.num_programs(1) - 1)
    rows = nb * clen
    heads = range(RET_HEADS)
    chunks = range(cps)

    @pl.when((group == 0) & (c == 0))
    def _():
        li = lax.broadcasted_iota(jnp.int32, (rows, rows), 0)
        mi = lax.broadcasted_iota(jnp.int32, (rows, rows), 1)
        keep = (li // clen == mi // clen) & (li >= mi)
        diff = jnp.where(keep, li - mi, 0).astype(F32)
        tpos = (lax.broadcasted_iota(jnp.int32, (rows, LANES), 0) % clen).astype(F32)
        for h in heads:
            lg = _log_gamma(h)
            dec_ref[h] = jnp.where(keep, jnp.exp(lg * diff), 0.0)
            rdec_ref[0, h] = jnp.exp(lg * (tpos + 1.0))
            rdec_ref[1, h] = jnp.exp(lg * (clen - 1.0 - tpos))

    @pl.when(c == 0)
    def _():
        for n in range(nb):
            s_ref[n] = s0_ref[n if s0_ref.shape[0] == nb else 0]

    if nb > 1:
        seq_of_row = lax.broadcasted_iota(jnp.int32, (rows, nb * HEAD_DIM), 0) // clen
        seq_of_col = lax.broadcasted_iota(jnp.int32, (rows, nb * HEAD_DIM), 1) // HEAD_DIM
        own = seq_of_row == seq_of_col
    tn = (((0,), (0,)), ((), ()))
    nt = (((1,), (1,)), ((), ()))
    hsl = [slice(h * HEAD_DIM, (h + 1) * HEAD_DIM) for h in heads]
    rsl = [slice(ci * rows, (ci + 1) * rows) for ci in chunks]
    q = [[q_ref[rsl[ci], sl] for sl in hsl] for ci in chunks]
    k = [[k_ref[rsl[ci], sl] for sl in hsl] for ci in chunks]
    v = [[v_ref[rsl[ci], sl] for sl in hsl] for ci in chunks]

    if emit_y:
        scores = [[lax.dot_general(q[ci][h], k[ci][h], nt, preferred_element_type=F32) for h in heads]
                  for ci in chunks]
    upd = []
    for ci in chunks:
        upd.append([])
        for h in heads:
            kd = (k[ci][h].astype(F32) * rdec_ref[1, h]).astype(BF16)
            if nb > 1:
                v_bd = jnp.where(own, jnp.concatenate([v[ci][h].astype(F32)] * nb, axis=1), 0.0).astype(BF16)
            else:
                v_bd = v[ci][h]
            upd[ci].append(lax.dot_general(kd, v_bd, tn, preferred_element_type=F32))

    state = [[[s_ref[n, h] for n in range(nb)] for h in heads]]
    for ci in chunks:
        state.append([[math.exp(clen * _log_gamma(h)) * state[ci][h][n]
                       + upd[ci][h][:, n * HEAD_DIM:(n + 1) * HEAD_DIM] for n in range(nb)] for h in heads])
    for h in heads:
        for n in range(nb):
            s_ref[n, h] = state[cps][h][n]

    if emit_y:
        inter = []
        for ci in chunks:
            inter.append([])
            for h in heads:
                s_cat = jnp.concatenate(state[ci][h], axis=1) if nb > 1 else state[ci][h][0]
                qs = jnp.dot(q[ci][h], s_cat.astype(BF16), preferred_element_type=F32)
                if nb > 1:
                    qs = jnp.where(own, qs, 0.0)
                    qs = sum(qs[:, n * HEAD_DIM:(n + 1) * HEAD_DIM] for n in range(nb))
                inter[ci].append(qs)
        for ci in chunks:
            for h in heads:
                p = (scores[ci][h] * dec_ref[h]).astype(BF16)
                o = jnp.dot(p, v[ci][h], preferred_element_type=F32) + inter[ci][h] * rdec_ref[0, h]
                mu = jnp.mean(o, axis=-1, keepdims=True)
                dev = o - mu
                var = jnp.mean(dev * dev, axis=-1, keepdims=True)
                normed = dev * lax.rsqrt(var + EPS) * ng_ref[:, hsl[h]]
                y_ref[rsl[ci], hsl[h]] = (sg_ref[rsl[ci], hsl[h]].astype(F32) * normed).astype(BF16)

    @pl.when(c == last)
    def _():
        sout_ref[...] = s_ref[...]


def _retention(act, row0, nseq, seqlen, nb, clen, cps, s0, norm_g, emit_y, ids=None, side=()):
    w = RET_HEADS * HEAD_DIM
    rows = nb * clen * cps
    nc = seqlen // (clen * cps)
    assert nb == 1 or (nc == 1 and cps == 1)
    rb0 = row0 // rows
    rmap = lambda col: (lambda b, c: (rb0 + b * nc + c, col))
    shared = s0.shape[0] != nseq
    s_map = (lambda b, c: (0, 0, 0, 0)) if shared else (lambda b, c: (b, 0, 0, 0))
    s_block = (1 if shared else nb, RET_HEADS, HEAD_DIM, HEAD_DIM)
    out_specs = [pl.BlockSpec((nb, RET_HEADS, HEAD_DIM, HEAD_DIM), lambda b, c: (b, 0, 0, 0))]
    out_shape = [jax.ShapeDtypeStruct((nseq, RET_HEADS, HEAD_DIM, HEAD_DIM), F32)]
    if emit_y:
        out_specs = [pl.BlockSpec((rows, w), lambda b, c: (b * nc + c, 0))] + out_specs
        out_shape = [jax.ShapeDtypeStruct((nseq * seqlen, w), BF16)] + out_shape
    steps = (nseq // nb) * nc
    side_specs = [pl.BlockSpec((wt.shape[0] // steps, wt.shape[1]), lambda b, c: (b * nc + c, 0)) for wt in side]
    return dict(
        name="retention",
        kernel=functools.partial(_ret_kernel, nb=nb, clen=clen, cps=cps, emit_y=emit_y, ids=ids, n_side=len(side)),
        grid=(nseq // nb, nc),
        in_specs=[
            pl.BlockSpec((rows, w), rmap(_ACT_Q)),
            pl.BlockSpec((rows, w), rmap(_ACT_K)),
            pl.BlockSpec((rows, w), rmap(_ACT_V)),
            pl.BlockSpec((rows, w), rmap(_ACT_SILU)),
            pl.BlockSpec(s_block, s_map),
            pl.BlockSpec((1, w), lambda b, c: (0, 0)),
        ] + side_specs,
        out_specs=out_specs + side_specs,
        out_shape=out_shape + [jax.ShapeDtypeStruct(wt.shape, BF16) for wt in side],
        scratch_shapes=[pltpu.VMEM((nb, RET_HEADS, HEAD_DIM, HEAD_DIM), F32),
                        pltpu.VMEM((RET_HEADS, nb * clen, nb * clen), F32),
                        pltpu.VMEM((2, RET_HEADS, nb * clen, LANES), F32)],
        args=(act, act, act, act, s0, norm_g) + tuple(side))


def _outproj_kernel(x_ref, ya_ref, yb_ref, sga_ref, sgb_ref, pa_ref, pb_ref, wo_ref, o_ref):
    for r0, nr in _row_chunks(o_ref.shape[0], 256):
        rows = slice(r0, r0 + nr)
        ma = jnp.dot(ya_ref[rows, :], pa_ref[...], preferred_element_type=F32)
        mb = jnp.dot(yb_ref[rows, :], pb_ref[...], preferred_element_type=F32)
        merged = sga_ref[rows, :].astype(F32) * ma + sgb_ref[rows, :].astype(F32) * mb
        o_ref[rows, :] = x_ref[rows, :] + jnp.dot(merged.astype(BF16), wo_ref[...], preferred_element_type=F32)


def _outproj(x, ya, yb, act, pa, pb, wo, bm):
    m, d = x.shape
    w = ya.shape[1]
    nd = d // 1024
    row = lambda i: (i, 0)
    const = lambda i: (0, 0)
    single = pl.Buffered(1)
    return pl.pallas_call(
        _outproj_kernel,
        name="outproj",
        grid=(m // bm,),
        in_specs=[
            pl.BlockSpec((bm, d), row),
            pl.BlockSpec((bm, w), row),
            pl.BlockSpec((bm, w), row),
            pl.BlockSpec((bm, d), lambda i: (i, _ACT_SGA // nd)),
            pl.BlockSpec((bm, d), lambda i: (i, _ACT_SGB // nd)),
            pl.BlockSpec((w, d), const, pipeline_mode=single),
            pl.BlockSpec((w, d), const, pipeline_mode=single),
            pl.BlockSpec((d, d), const, pipeline_mode=single),
        ],
        out_specs=pl.BlockSpec((bm, d), row),
        out_shape=jax.ShapeDtypeStruct((m, d), F32),
        compiler_params=_cparams(1),
    )(x, ya, yb, act, act, pa, pb, wo)


def _ffn_kernel(x_ref, g_ref, wu_ref, wd_ref, gf_ref, o_ref, hn_ref):
    j = pl.program_id(1)
    last = pl.num_programs(1) - 1

    def block(first, final):
        for r0, nr in _row_chunks(o_ref.shape[0], 512):
            rows = slice(r0, r0 + nr)
            if first:
                base = x_ref[rows, :]
                hn = _rmsnorm(base, g_ref[...]).astype(BF16)
                hn_ref[rows, :] = hn
            else:
                base = o_ref[rows, :]
                hn = hn_ref[rows, :]
            u = jnp.dot(hn, wu_ref[...], preferred_element_type=F32)
            r = jnp.square(jnp.maximum(u, 0.0)).astype(BF16)
            acc = base + jnp.dot(r, wd_ref[...], preferred_element_type=F32)
            o_ref[rows, :] = _rmsnorm(acc, gf_ref[...]) if final else acc

    pl.when(j == 0)(lambda: block(True, False))
    pl.when((j > 0) & (j < last))(lambda: block(False, False))
    pl.when(j == last)(lambda: block(False, True))


def _ffn(x, g, wu, wd, gf, bm, bf):
    m, d = x.shape
    dff = wu.shape[1]
    assert dff // bf >= 2
    return pl.pallas_call(
        _ffn_kernel,
        name="ffn",
        grid=(m // bm, dff // bf),
        in_specs=[
            pl.BlockSpec((bm, d), lambda i, j: (i, 0)),
            pl.BlockSpec((1, d), lambda i, j: (0, 0)),
            pl.BlockSpec((d, bf), lambda i, j: (0, j)),
            pl.BlockSpec((bf, d), lambda i, j: (j, 0)),
            pl.BlockSpec((1, d), lambda i, j: (0, 0)),
        ],
        out_specs=pl.BlockSpec((bm, d), lambda i, j: (i, 0)),
        out_shape=jax.ShapeDtypeStruct((m, d), F32),
        scratch_shapes=[pltpu.VMEM((bm, d), BF16)],
        compiler_params=_cparams(2),
    )(x, g, wu, wd, gf)


def _rope_tables(pos):
    inv = ROPE_BASE ** (-np.arange(0, HEAD_DIM, 2, dtype=np.float64) / HEAD_DIM)
    ang = np.asarray(pos, np.float64)[:, None] * inv[None, :]
    cos, sin = np.cos(ang), np.sin(ang)
    return (jnp.asarray(np.concatenate([cos, cos], axis=1), F32),
            jnp.asarray(np.concatenate([-sin, sin], axis=1), F32))


def _gate_weights(wa, wx, ba, bx):
    nblk, blk, _ = wa.shape
    per = LANES // blk
    ng = nblk // per

    def bd(wt):
        wt = wt.reshape(ng, per, blk, blk)
        eye = jnp.eye(per, dtype=wt.dtype)
        return jnp.einsum("gpcd,pq->gpcqd", wt, eye).reshape(ng, LANES, LANES)

    wbd = jnp.concatenate([bd(wa), bd(wx)], axis=2).astype(BF16)
    bg = jnp.concatenate([ba.reshape(ng, 1, LANES), bx.reshape(ng, 1, LANES)], axis=2)
    return wbd, bg


def kernel(x_prompt, x_sample, state_conv, state_lru, state_ret, meta_tokens, norm_mix_g, w_in, conv_w,
           conv_b, lru_wa, lru_ba, lru_wx, lru_bx, lru_lam, ret_norm_g, p_a, p_b, w_out, norm_ffn_g,
           w_up, w_down, norm_f_g):
    assert w_in.shape[0] == 1
    nb_p, t_p, d = x_prompt.shape
    nb_s, t_s, _ = x_sample.shape
    w = conv_w.shape[-1]
    rows_p, rows_s = nb_p * t_p, nb_s * t_s

    g_mix, g_ffn, g_f = norm_mix_g[0][None], norm_ffn_g[0][None], norm_f_g[None]
    wbd, bg = _gate_weights(lru_wa[0], lru_wx[0], lru_ba[0], lru_bx[0])
    lw = dict(conv_w=conv_w[0], conv_b=conv_b[0][None], wbd=wbd, bg=bg, lam=lru_lam[0][None])
    ret_g = ret_norm_g[0][None]

    cos_p, sin_p = _rope_tables(N_META + np.arange(t_p))
    cos_s, sin_s = _rope_tables(np.concatenate([np.tile(PAST_LEN + np.arange(t_s), nb_s), np.arange(N_META)]))
    x_p2 = x_prompt.reshape(rows_p, d)
    x_s2 = x_sample.reshape(rows_s, d)
    xa_s, act_s, w_in_b = _inproj(x_s2, g_mix, w_in[0], cos_s, sin_s, rows_s + N_META, 512, 1,
                                  tail=meta_tokens.astype(x_sample.dtype))
    bm_p = 1024
    n_i, n_j = rows_p // bm_p, 8
    side_map = lambda i, j: (i, jnp.minimum(j, n_j - 1))
    sides = tuple((wt[0], (wt.shape[1] // n_i, wt.shape[2] // n_j), side_map) for wt in (w_up, w_down))
    xa_p, act_p, wu_b, wd_b = _inproj(x_p2, g_mix, w_in_b, cos_p, sin_p, bm_p, 1024, t_p // bm_p,
                                                        side=sides)


    zc = jnp.zeros((1, CONV_W - 1, w), F32)
    zh = jnp.zeros((1, 1, w), F32)
    zs = jnp.zeros((1, RET_HEADS, HEAD_DIM, HEAD_DIM), F32)
    conv_m, h_m = _call(_lru_seq(xa_s, act_s, rows_s, 1, N_META, N_META, zc, zh, lw, emit_y=False,
                                 first_pos_zero=True))
    (s_m,) = _call(_retention(act_s, rows_s, 1, N_META, 1, N_META, 1, zs, ret_g, emit_y=False))

    lru_tr = 512
    lru_p = _lru_seq(xa_p, act_p, 0, nb_p, t_p, lru_tr, conv_m, h_m, lw, emit_y=True, first_pos_zero=False)
    steps = nb_p * (t_p // lru_tr)
    ret_s = _retention(act_s, 0, nb_s, t_s, nb_s // steps, t_s, 1, state_ret[0], ret_g, emit_y=True,
                       ids=lambda: (pl.program_id(0) * (t_p // lru_tr) + pl.program_id(1), 0, 0))
    (ya_p, conv_p, h_p), (yb_s, s_s) = _call_pair(lru_p, ret_s, "lru_seq_retention")
    yb_p, s_p, pa_b, pb_b, wo_b = _call(_retention(act_p, 0, nb_p, t_p, 1, CHUNK, 4, s_m, ret_g, emit_y=True,
                                                   side=(p_a[0], p_b[0], w_out[0])))
    conv0_t = jnp.transpose(state_conv[0], (1, 0, 2))
    ya_s, conv_s_t, h_s = _lru_step(xa_s, act_s, nb_s, t_s, conv0_t, state_lru[0], lw)

    x1_p = _outproj(x_p2, ya_p, yb_p, act_p, pa_b, pb_b, wo_b, 512)
    y_p = _ffn(x1_p, g_ffn, wu_b, wd_b, g_f, 1024, 1024)
    y_p, ya_s = lax.optimization_barrier((y_p, ya_s))
    x1_s = _outproj(x_s2, ya_s, yb_s, act_s, pa_b, pb_b, wo_b, 512)
    y_s = _ffn(x1_s, g_ffn, wu_b, wd_b, g_f, 1024, 1024)

    return (y_p.reshape(nb_p, t_p, d), y_s.reshape(nb_s, t_s, d),
            conv_p[None], h_p.reshape(1, nb_p, w), s_p[None],
            jnp.transpose(conv_s_t, (1, 0, 2))[None], h_s[None], s_s[None])
```
